```python
import jax, jax.numpy as jnp
from jax import lax
import numpy as np

D_MODEL = 4096
BATCH = 2
SEQ = 8192
DEPTH = 2
DEC_BATCH = 32
DEC_SEQ = 32
PAST_LEN = 4096

CHUNK = 64
N_EVEN = (DEPTH + 1) // 2
N_ODD = DEPTH // 2
A_DK = 128
A_DV = 128
A_HEADS = D_MODEL // 256
A_W = A_HEADS * A_DK
A_V = A_HEADS * A_DV
B_DK = 128
B_DV = 2 * B_DK
B_HEADS = D_MODEL // 512
B_QK = B_HEADS * B_DK
B_V = B_HEADS * B_DV
EVEN_IN = 2 * A_W + 2 * A_V + 2 * B_QK + 2 * B_V
EVEN_SPLITS = [A_W, 2 * A_W, 2 * A_W + A_V, 2 * A_W + 2 * A_V,
               2 * A_W + 2 * A_V + B_QK, 2 * A_W + 2 * A_V + 2 * B_QK,
               2 * A_W + 2 * A_V + 2 * B_QK + B_V]
ROPE_BASE = 10000.0
D_RNN = D_MODEL * 5 // 4
RG_BLOCKS = 16
RG_BS = D_RNN // RG_BLOCKS
CONV_W = 4
RG_C = 8.0
FF = 4 * D_MODEL
EPS = 1e-6

kernel_name = 'hgrn2_retention_rglru_stream_step'


def rms_norm(x, g):
    xf = x.astype(jnp.float32)
    y = xf * lax.rsqrt(jnp.mean(xf * xf, axis=-1, keepdims=True) + EPS)
    return (y * g.astype(jnp.float32)).astype(x.dtype)


def head_rms(o):
    return o * lax.rsqrt(jnp.mean(o * o, axis=-1, keepdims=True) + EPS)


def head_group_norm(o):
    mu = jnp.mean(o, axis=-1, keepdims=True)
    c = o - mu
    return c * lax.rsqrt(jnp.mean(c * c, axis=-1, keepdims=True) + EPS)


def rotary(x, pos):
    half = x.shape[-1] // 2
    inv = 1.0 / (ROPE_BASE ** jnp.linspace(0.0, 1.0, half, dtype=jnp.float32))
    ang = pos.astype(jnp.float32)[:, None] * inv[None, :]
    cos = jnp.cos(ang)[None, :, None, :]
    sin = jnp.sin(ang)[None, :, None, :]
    x1, x2 = x[..., :half], x[..., half:]
    return jnp.concatenate([x1 * cos - x2 * sin, x2 * cos + x1 * sin], axis=-1)


def ret_log_gamma():
    return jnp.log(1.0 - jnp.exp2(-5.0 - jnp.arange(B_HEADS, dtype=jnp.float32)))


def chunked_gla(q, k, v, log_f, s0):
    B, T, H, K = q.shape
    V = v.shape[-1]
    L = min(CHUNK, T)
    N = T // L

    def blocks(a):
        return a.reshape(B, N, L, H, a.shape[-1]).transpose(1, 0, 3, 2, 4)

    qc, kc, vc, gc = blocks(q), blocks(k), blocks(v), blocks(log_f)
    b = jnp.cumsum(gc, axis=3)
    b_last = b[:, :, :, -1:, :]
    q_in = qc * jnp.exp(b)
    k_in = kc * jnp.exp(-b)
    k_out = kc * jnp.exp(b_last - b)
    mask = jnp.tril(jnp.ones((L, L), dtype=bool))
    attn = jnp.where(mask, jnp.einsum('nbhlk,nbhmk->nbhlm', q_in, k_in), 0.0)
    o_intra = jnp.einsum('nbhlm,nbhmv->nbhlv', attn, vc)
    kv = jnp.einsum('nbhlk,nbhlv->nbhkv', k_out, vc)
    decay = jnp.exp(b_last[:, :, :, 0, :])

    def step(s, inp):
        q_i, kv_i, d_i = inp
        o = jnp.einsum('bhlk,bhkv->bhlv', q_i, s)
        return d_i[..., None] * s + kv_i, o

    s_final, o_inter = lax.scan(step, s0, (q_in, kv, decay))
    o = (o_intra + o_inter).transpose(1, 0, 3, 2, 4).reshape(B, T, H, V)
    return o, s_final


def even_mixer(x, pos, s_a, s_b, w_in, lb, g_a, g_b, w_out):
    B, T, _ = x.shape
    f32 = jnp.float32
    proj = jnp.einsum('btd,de->bte', x, w_in)
    qa, fa, ia, ga, qb, kb, vb, gb = jnp.split(proj.astype(f32), EVEN_SPLITS, axis=-1)

    def heads(a, h):
        return a.reshape(B, T, h, -1)

    lbf = lb.astype(f32)
    f = lbf + (1.0 - lbf) * jax.nn.sigmoid(fa)
    o_a, s_a_new = chunked_gla(heads(jax.nn.silu(qa), A_HEADS), heads(1.0 - f, A_HEADS),
                               heads(ia, A_HEADS), heads(jnp.log(f), A_HEADS), s_a.astype(f32))
    o_a = head_rms(o_a).reshape(B, T, A_V) * g_a.astype(f32) * jax.nn.silu(ga)
    q = rotary(heads(qb, B_HEADS), pos)
    k = rotary(heads(kb, B_HEADS), pos) * (B_DK ** -0.5)
    log_g = jnp.broadcast_to(ret_log_gamma()[None, None, :, None], (B, T, B_HEADS, B_DK))
    o_b, s_b_new = chunked_gla(q, k, heads(vb, B_HEADS), log_g, s_b.astype(f32))
    o_b = head_group_norm(o_b).reshape(B, T, B_V) * g_b.astype(f32) * jax.nn.silu(gb)
    o = jnp.concatenate([o_a, o_b], axis=-1).astype(x.dtype)
    y = jnp.einsum('bte,ed->btd', o, w_out)
    return y, s_a_new.astype(s_a.dtype), s_b_new.astype(s_b.dtype)


def odd_mixer(x, pos, h0, s_conv, w_in, conv_w, conv_b, ga_w, ga_b, gx_w, gx_b, lam, w_out):
    B, T, _ = x.shape
    f32 = jnp.float32
    xy = jnp.einsum('btd,de->bte', x, w_in)
    xb, yb = xy[..., :D_RNN], xy[..., D_RNN:]
    xpad = jnp.concatenate([s_conv.astype(xb.dtype), xb], axis=1)
    new_conv = xpad[:, -(CONV_W - 1):]
    conv = conv_b.astype(f32) + sum(xpad[:, j:j + T].astype(f32) * conv_w[j].astype(f32)
                                    for j in range(CONV_W))
    xc = conv.reshape(B, T, RG_BLOCKS, RG_BS)
    r = jax.nn.sigmoid(jnp.einsum('btnc,ncd->btnd', xc, ga_w.astype(f32)).reshape(B, T, D_RNN)
                       + ga_b.astype(f32))
    i = jax.nn.sigmoid(jnp.einsum('btnc,ncd->btnd', xc, gx_w.astype(f32)).reshape(B, T, D_RNN)
                       + gx_b.astype(f32))
    log_a = -RG_C * r * jax.nn.softplus(-lam.astype(f32))
    a = jnp.exp(log_a)
    first = (pos == 0)[None, :, None]
    mult = jnp.where(first, 1.0, jnp.sqrt(-jnp.expm1(2.0 * log_a)))
    b = mult * (i * conv)
    b = b.at[:, 0].add(a[:, 0] * h0.astype(f32))

    def combine(lhs, rhs):
        a_l, b_l = lhs
        a_r, b_r = rhs
        return a_l * a_r, a_r * b_l + b_r

    _, hs = lax.associative_scan(combine, (a, b), axis=1)
    o = (jax.nn.gelu(yb.astype(f32), approximate=True) * hs).astype(x.dtype)
    y = jnp.einsum('bte,ed->btd', o, w_out)
    return y, hs[:, -1].astype(h0.dtype), new_conv.astype(s_conv.dtype)


def mlp(x, w_up, w_down):
    h = jnp.einsum('btd,df->btf', x, w_up)
    h = jnp.square(jax.nn.relu(h))
    return jnp.einsum('btf,fd->btd', h, w_down)


def trunk(x, pos, s_hgrn, s_ret, s_rglru, s_conv, norm_mix, norm_mlp, norm_final, w_in_even,
          hgrn_lb_logits, hgrn_norm, ret_norm, w_out_even, w_in_odd, conv_w, conv_b, gate_a_w,
          gate_a_b, gate_x_w, gate_x_b, rglru_lambda, w_out_odd, w_up, w_down):
    lb_all = jnp.cumsum(jax.nn.softmax(hgrn_lb_logits.astype(jnp.float32), axis=0), axis=0)
    h = x
    n_hgrn, n_ret, n_rg, n_conv = [], [], [], []
    for l in range(DEPTH):
        hn = rms_norm(h, norm_mix[l])
        if l % 2 == 0:
            e = l // 2
            y, sa, sb = even_mixer(hn, pos, s_hgrn[e], s_ret[e], w_in_even[e], lb_all[l],
                                   hgrn_norm[e], ret_norm[e], w_out_even[e])
            n_hgrn.append(sa)
            n_ret.append(sb)
        else:
            o = l // 2
            y, sh, sc = odd_mixer(hn, pos, s_rglru[o], s_conv[o], w_in_odd[o], conv_w[o], conv_b[o],
                                  gate_a_w[o], gate_a_b[o], gate_x_w[o], gate_x_b[o],
                                  rglru_lambda[o], w_out_odd[o])
            n_rg.append(sh)
            n_conv.append(sc)
        h = h + y
        h = h + mlp(rms_norm(h, norm_mlp[l]), w_up[l], w_down[l])
    out = rms_norm(h, norm_final)
    return out, jnp.stack(n_hgrn), jnp.stack(n_ret), jnp.stack(n_rg), jnp.stack(n_conv)


def setup_inputs(seed: int = 0) -> dict:
    key = jax.random.key(seed)
    ks = jax.random.split(key, 32)
    f32 = jnp.float32

    def nrm(k, shape, scale):
        return jax.random.normal(k, shape, f32) * scale

    u = jax.random.uniform(ks[24], (N_ODD, D_RNN), f32, minval=0.9, maxval=0.999)
    sa = u ** (1.0 / RG_C)
    lam = jnp.log(sa) - jnp.log1p(-sa)
    return {
        'x_prompt': nrm(ks[0], (BATCH, SEQ, D_MODEL), 1.0),
        'x_sample': nrm(ks[1], (DEC_BATCH, DEC_SEQ, D_MODEL), 1.0),
        'state_hgrn': nrm(ks[2], (N_EVEN, DEC_BATCH, A_HEADS, A_DK, A_DV), 0.5),
        'state_ret': nrm(ks[3], (N_EVEN, DEC_BATCH, B_HEADS, B_DK, B_DV), 1.0),
        'state_rglru': nrm(ks[4], (N_ODD, DEC_BATCH, D_RNN), 0.5),
        'state_conv': nrm(ks[5], (N_ODD, DEC_BATCH, CONV_W - 1, D_RNN), 1.0),
        'norm_mix': 1.0 + nrm(ks[6], (DEPTH, D_MODEL), 0.02),
        'norm_mlp': 1.0 + nrm(ks[7], (DEPTH, D_MODEL), 0.02),
        'norm_final': 1.0 + nrm(ks[8], (D_MODEL,), 0.02),
        'w_in_even': nrm(ks[9], (N_EVEN, D_MODEL, EVEN_IN), D_MODEL ** -0.5),
        'hgrn_lb_logits': nrm(ks[10], (DEPTH + 1, A_W), 0.1),
        'hgrn_norm': 1.0 + nrm(ks[11], (N_EVEN, A_V), 0.02),
        'ret_norm': 1.0 + nrm(ks[12], (N_EVEN, B_V), 0.02),
        'w_out_even': nrm(ks[13], (N_EVEN, A_V + B_V, D_MODEL), (A_V + B_V) ** -0.5),
        'w_in_odd': nrm(ks[14], (N_ODD, D_MODEL, 2 * D_RNN), D_MODEL ** -0.5),
        'conv_w': nrm(ks[15], (N_ODD, CONV_W, D_RNN), CONV_W ** -0.5),
        'conv_b': nrm(ks[16], (N_ODD, D_RNN), 0.01),
        'gate_a_w': nrm(ks[17], (N_ODD, RG_BLOCKS, RG_BS, RG_BS), RG_BS ** -0.5),
        'gate_a_b': nrm(ks[18], (N_ODD, D_RNN), 0.01),
        'gate_x_w': nrm(ks[19], (N_ODD, RG_BLOCKS, RG_BS, RG_BS), RG_BS ** -0.5),
        'gate_x_b': nrm(ks[20], (N_ODD, D_RNN), 0.01),
        'rglru_lambda': lam,
        'w_out_odd': nrm(ks[21], (N_ODD, D_RNN, D_MODEL), D_RNN ** -0.5),
        'w_up': nrm(ks[22], (DEPTH, D_MODEL, FF), D_MODEL ** -0.5),
        'w_down': nrm(ks[23], (DEPTH, FF, D_MODEL), FF ** -0.5),
    }


def reference(x_prompt, x_sample, state_hgrn, state_ret, state_rglru, state_conv, norm_mix,
              norm_mlp, norm_final, w_in_even, hgrn_lb_logits, hgrn_norm, ret_norm, w_out_even,
              w_in_odd, conv_w, conv_b, gate_a_w, gate_a_b, gate_x_w, gate_x_b, rglru_lambda,
              w_out_odd, w_up, w_down):
    dt = x_prompt.dtype
    z_hgrn = jnp.zeros((N_EVEN, BATCH, A_HEADS, A_DK, A_DV), dt)
    z_ret = jnp.zeros((N_EVEN, BATCH, B_HEADS, B_DK, B_DV), dt)
    z_rg = jnp.zeros((N_ODD, BATCH, D_RNN), dt)
    z_conv = jnp.zeros((N_ODD, BATCH, CONV_W - 1, D_RNN), dt)
    pos_p = jnp.arange(SEQ, dtype=jnp.int32)
    pos_s = PAST_LEN + jnp.arange(DEC_SEQ, dtype=jnp.int32)
    y_prompt, hg_p, rt_p, rg_p, cv_p = trunk(
        x_prompt, pos_p, z_hgrn, z_ret, z_rg, z_conv, norm_mix, norm_mlp, norm_final, w_in_even,
        hgrn_lb_logits, hgrn_norm, ret_norm, w_out_even, w_in_odd, conv_w, conv_b, gate_a_w,
        gate_a_b, gate_x_w, gate_x_b, rglru_lambda, w_out_odd, w_up, w_down)
    y_sample, hg_s, rt_s, rg_s, cv_s = trunk(
        x_sample, pos_s, state_hgrn, state_ret, state_rglru, state_conv, norm_mix, norm_mlp,
        norm_final, w_in_even, hgrn_lb_logits, hgrn_norm, ret_norm, w_out_even, w_in_odd, conv_w,
        conv_b, gate_a_w, gate_a_b, gate_x_w, gate_x_b, rglru_lambda, w_out_odd, w_up, w_down)
    return (y_prompt, y_sample, hg_p, rt_p, rg_p, cv_p, hg_s, rt_s, rg_s, cv_s)
```

```python
import functools

import jax
import jax.numpy as jnp
from jax import lax
from jax.experimental import pallas as pl
from jax.experimental.pallas import tpu as pltpu

EPS = 1e-6
CHUNK = 64
ROPE_BASE = 10000.0
RG_C = 8.0
HEAD_LANES = 128
RG_PAIR = 2

V7X_VMEM_BYTES = 64 * 1024 * 1024
VMEM_CAP_BYTES = V7X_VMEM_BYTES - 6 * 1024 * 1024

F32 = jnp.float32
BF16 = jnp.bfloat16


def _cparams(semantics, vmem_bytes):
    return pltpu.CompilerParams(dimension_semantics=semantics,
                                vmem_limit_bytes=int(min(max(vmem_bytes, 32 * 1024 * 1024), VMEM_CAP_BYTES)))


def _rms_kernel(x_ref, g_ref, o_ref):
    x = x_ref[...]
    y = x * lax.rsqrt(jnp.mean(x * x, axis=-1, keepdims=True) + EPS)
    o_ref[...] = (y * g_ref[...]).astype(o_ref.dtype)


def _rms_norm(x, g, out_dtype, tm=256):
    m, d = x.shape
    tm = min(tm, m)
    io_bytes = 2 * tm * d * (4 + jnp.dtype(out_dtype).itemsize)
    return pl.pallas_call(
        _rms_kernel,
        out_shape=jax.ShapeDtypeStruct((m, d), out_dtype),
        grid=(m // tm,),
        in_specs=[pl.BlockSpec((tm, d), lambda i: (i, 0)),
                  pl.BlockSpec((1, d), lambda i: (0, 0))],
        out_specs=pl.BlockSpec((tm, d), lambda i: (i, 0)),
        compiler_params=_cparams(("parallel",), io_bytes + 3 * tm * d * 4),
        name="rms_norm",
    )(x, g.reshape(1, d))


def _epilogue(acc, mode, res_ref, o_ref):
    if mode == "relu2":
        acc = jnp.square(jnp.maximum(acc, 0.0))
    elif mode == "residual":
        acc = res_ref[...] + acc
    o_ref[...] = acc.astype(o_ref.dtype)


def _mm_kernel_single(*refs, mode):
    if mode == "residual":
        x_ref, w_ref, res_ref, o_ref = refs
    else:
        (x_ref, w_ref, o_ref), res_ref = refs, None
    acc = jnp.dot(x_ref[...], w_ref[...], preferred_element_type=F32)
    _epilogue(acc, mode, res_ref, o_ref)


def _mm_kernel_multi(*refs, mode):
    if mode == "residual":
        x_ref, w_ref, res_ref, o_ref, acc_ref = refs
    else:
        (x_ref, w_ref, o_ref, acc_ref), res_ref = refs, None
    k = pl.program_id(2)

    @pl.when(k == 0)
    def _():
        acc_ref[...] = jnp.zeros_like(acc_ref)

    acc_ref[...] += jnp.dot(x_ref[...], w_ref[...], preferred_element_type=F32)

    @pl.when(k == pl.num_programs(2) - 1)
    def _():
        _epilogue(acc_ref[...], mode, res_ref, o_ref)


def _matmul(x, w, *, mode="none", res=None, out_dtype=F32, tm=1024, tn=1024, tk=4096):
    m, kdim = x.shape
    n = w.shape[1]
    tm, tn, tk = min(tm, m), min(tn, n), min(tk, kdim)
    nk = kdim // tk
    assert m % tm == 0 and n % tn == 0 and kdim % tk == 0
    out_bytes = jnp.dtype(out_dtype).itemsize
    vmem = 2 * (tm * tk * 2 + tk * tn * 2 + tm * tn * out_bytes) + tm * tn * 4 + (4 << 20)
    if nk == 1:
        grid = (m // tm, n // tn)
        in_specs = [pl.BlockSpec((tm, tk), lambda i, j: (i, 0)),
                    pl.BlockSpec((tk, tn), lambda i, j: (0, j))]
        res_spec = pl.BlockSpec((tm, tn), lambda i, j: (i, j))
        out_spec = pl.BlockSpec((tm, tn), lambda i, j: (i, j))
        body, scratch, sem = _mm_kernel_single, [], ("parallel", "parallel")
    else:
        grid = (m // tm, n // tn, nk)
        in_specs = [pl.BlockSpec((tm, tk), lambda i, j, k: (i, k)),
                    pl.BlockSpec((tk, tn), lambda i, j, k: (k, j))]
        res_spec = pl.BlockSpec((tm, tn), lambda i, j, k: (i, j))
        out_spec = pl.BlockSpec((tm, tn), lambda i, j, k: (i, j))
        body, scratch, sem = _mm_kernel_multi, [pltpu.VMEM((tm, tn), F32)], ("parallel", "parallel", "arbitrary")
        vmem += tm * tn * 4
    args = [x, w]
    if mode == "residual":
        in_specs.append(res_spec)
        args.append(res)
        vmem += 2 * tm * tn * 4
    return pl.pallas_call(
        functools.partial(body, mode=mode),
        out_shape=jax.ShapeDtypeStruct((m, n), out_dtype),
        grid=grid,
        in_specs=in_specs,
        out_specs=out_spec,
        scratch_shapes=scratch,
        compiler_params=_cparams(sem, vmem),
        name="proj_" + mode,
    )(*args)


def _chunk_pos(shape, chunk):
    return lax.broadcasted_iota(jnp.int32, shape, 0) % chunk


def _cumsum_in_chunks(x, chunk):
    pos = _chunk_pos(x.shape, chunk)
    shift = 1
    while shift < chunk:
        x = x + jnp.where(pos >= shift, pltpu.roll(x, shift, 0), 0.0)
        shift *= 2
    return x


def _causal_mask(chunk):
    r = lax.broadcasted_iota(jnp.int32, (chunk, chunk), 0)
    c = lax.broadcasted_iota(jnp.int32, (chunk, chunk), 1)
    return r >= c


_NT = (((1,), (1,)), ((), ()))
_TN = (((0,), (0,)), ((), ()))


def _hgrn_kernel(*refs, chunk, n_chunks, layer, has_state):
    if has_state:
        q_ref, f_ref, v_ref, g_ref, lbl_ref, gn_ref, s0_ref, o_ref, sout_ref, st_ref = refs
    else:
        (q_ref, f_ref, v_ref, g_ref, lbl_ref, gn_ref, o_ref, sout_ref, st_ref), s0_ref = refs, None
    t = pl.program_id(2)

    @pl.when(t == 0)
    def _():
        if has_state:
            st_ref[...] = s0_ref[0, 0].T
        else:
            st_ref[...] = jnp.zeros_like(st_ref)

    logits = lbl_ref[...]
    rows = [logits[i:i + 1] for i in range(logits.shape[0])]
    mx = functools.reduce(jnp.maximum, rows)
    es = [jnp.exp(r - mx) for r in rows]
    den = functools.reduce(lambda a, b: a + b, es)
    lb = es[0] / den
    for i in range(1, layer + 1):
        lb = lb + es[i] / den

    f = lb + (1.0 - lb) * jax.nn.sigmoid(f_ref[...])
    q = jax.nn.silu(q_ref[...])
    k = 1.0 - f
    b = _cumsum_in_chunks(jnp.log(f), chunk)
    q_in = (q * jnp.exp(b)).astype(BF16)
    k_in = (k * jnp.exp(-b)).astype(BF16)
    v = v_ref[...].astype(BF16)
    mask = _causal_mask(chunk)
    st = st_ref[...]
    outs = []
    for c in range(n_chunks):
        sl = slice(c * chunk, (c + 1) * chunk)
        b_c = b[sl]
        b_last = b_c[chunk - 1:chunk]
        k_out = (k[sl] * jnp.exp(b_last - b_c)).astype(BF16)
        attn = lax.dot_general(q_in[sl], k_in[sl], _NT, preferred_element_type=F32)
        attn = jnp.where(mask, attn, 0.0).astype(BF16)
        o_intra = jnp.dot(attn, v[sl], preferred_element_type=F32)
        o_inter = lax.dot_general(q_in[sl], st.astype(BF16), _NT, preferred_element_type=F32)
        kv_t = lax.dot_general(v[sl], k_out, _TN, preferred_element_type=F32)
        st = jnp.exp(b_last) * st + kv_t
        outs.append(o_intra + o_inter)
    st_ref[...] = st
    o = outs[0] if n_chunks == 1 else jnp.concatenate(outs, axis=0)
    o = o * lax.rsqrt(jnp.mean(o * o, axis=-1, keepdims=True) + EPS)
    o = o * gn_ref[...] * jax.nn.silu(g_ref[...])
    o_ref[...] = o.astype(o_ref.dtype)

    @pl.when(t == pl.num_programs(2) - 1)
    def _():
        sout_ref[0, 0] = st.T


def _ret_kernel(*refs, chunk, n_chunks, has_state):
    if has_state:
        q_ref, k_ref, v_ref, g_ref, cos_ref, sin_ref, lg_ref, gn_ref, s0_ref, o_ref, sout_ref, st_ref = refs
    else:
        (q_ref, k_ref, v_ref, g_ref, cos_ref, sin_ref, lg_ref, gn_ref, o_ref, sout_ref, st_ref), s0_ref = refs, None
    t = pl.program_id(2)

    @pl.when(t == 0)
    def _():
        if has_state:
            st_ref[...] = s0_ref[0, 0]
        else:
            st_ref[...] = jnp.zeros_like(st_ref)

    cosf, sinf = cos_ref[...], sin_ref[...]
    q, k = q_ref[...], k_ref[...]
    half = HEAD_LANES // 2
    qr = q * cosf + pltpu.roll(q, half, 1) * sinf
    kr = (k * cosf + pltpu.roll(k, half, 1) * sinf) * (HEAD_LANES ** -0.5)
    lg_wide = lg_ref[0]
    lg = lg_wide[:, :HEAD_LANES]
    b = (_chunk_pos(q.shape, chunk) + 1).astype(F32) * lg
    b_last = float(chunk) * lg
    q_in = (qr * jnp.exp(b)).astype(BF16)
    k_in = (kr * jnp.exp(-b)).astype(BF16)
    k_out = (kr * jnp.exp(b_last - b)).astype(BF16)
    decay = jnp.exp(float(chunk) * lg_wide)
    v = v_ref[...].astype(BF16)
    mask = _causal_mask(chunk)
    st = st_ref[...]
    outs = []
    for c in range(n_chunks):
        sl = slice(c * chunk, (c + 1) * chunk)
        attn = lax.dot_general(q_in[sl], k_in[sl], _NT, preferred_element_type=F32)
        attn = jnp.where(mask, attn, 0.0).astype(BF16)
        o_intra = jnp.dot(attn, v[sl], preferred_element_type=F32)
        o_inter = jnp.dot(q_in[sl], st.astype(BF16), preferred_element_type=F32)
        kv = lax.dot_general(k_out[sl], v[sl], _TN, preferred_element_type=F32)
        st = decay * st + kv
        outs.append(o_intra + o_inter)
    st_ref[...] = st
    o = outs[0] if n_chunks == 1 else jnp.concatenate(outs, axis=0)
    c0 = o - jnp.mean(o, axis=-1, keepdims=True)
    o = c0 * lax.rsqrt(jnp.mean(c0 * c0, axis=-1, keepdims=True) + EPS)
    o = o * gn_ref[...] * jax.nn.silu(g_ref[...])
    o_ref[...] = o.astype(o_ref.dtype)

    @pl.when(t == pl.num_programs(2) - 1)
    def _():
        sout_ref[0, 0] = st


def _rotary_tables(pos):
    half = HEAD_LANES // 2
    inv = 1.0 / (ROPE_BASE ** jnp.linspace(0.0, 1.0, half, dtype=F32))
    ang = pos.astype(F32)[:, None] * inv[None, :]
    cos, sin = jnp.cos(ang), jnp.sin(ang)
    return jnp.concatenate([cos, cos], axis=-1), jnp.concatenate([-sin, sin], axis=-1)


def _even_mixer(proj, batch, seq, pos, layer, s_a, s_b, a_shape, b_shape, lb_logits, g_a, g_b, rows_per_step=512):
    m = proj.shape[0]
    a_heads, dk, dv_a = a_shape
    b_heads, dk_b, dv_b = b_shape
    assert dk == HEAD_LANES and dv_a == HEAD_LANES and dk_b == HEAD_LANES
    chunk = min(CHUNK, seq)
    rows = min(rows_per_step, seq)
    assert seq % rows == 0 and rows % chunk == 0
    nt = seq // rows
    has_state = s_a is not None
    a_w, a_v = a_heads * dk, a_heads * dv_a
    b_qk = b_heads * HEAD_LANES
    cq, cf, ci, cg = 0, a_w // dk, 2 * a_w // dk, (2 * a_w + a_v) // dk
    off_b = 2 * a_w + 2 * a_v
    cqb, ckb = off_b // dk, (off_b + b_qk) // dk
    cvb, cgb = (off_b + 2 * b_qk) // dv_b, (off_b + 2 * b_qk + b_heads * dv_b) // dv_b

    def col(c0):
        return pl.BlockSpec((rows, HEAD_LANES), lambda b, h, t: (b * nt + t, c0 + h))

    def colw(c0):
        return pl.BlockSpec((rows, dv_b), lambda b, h, t: (b * nt + t, c0 + h))

    grid_sem = ("parallel", "parallel", "arbitrary")

    n_layers = lb_logits.shape[0]
    in_specs = [col(cq), col(cf), col(ci), col(cg),
                pl.BlockSpec((n_layers, HEAD_LANES), lambda b, h, t: (0, h)),
                pl.BlockSpec((1, HEAD_LANES), lambda b, h, t: (0, h))]
    args = [proj, proj, proj, proj, lb_logits, g_a.reshape(1, a_v)]
    if has_state:
        in_specs.append(pl.BlockSpec((1, 1, dk, dv_a), lambda b, h, t: (b, h, 0, 0)))
        args.append(s_a)
    o_a, new_a = pl.pallas_call(
        functools.partial(_hgrn_kernel, chunk=chunk, n_chunks=rows // chunk, layer=layer, has_state=has_state),
        out_shape=(jax.ShapeDtypeStruct((m, a_v), BF16),
                   jax.ShapeDtypeStruct((batch, a_heads, dk, dv_a), F32)),
        grid=(batch, a_heads, nt),
        in_specs=in_specs,
        out_specs=(pl.BlockSpec((rows, HEAD_LANES), lambda b, h, t: (b * nt + t, h)),
                   pl.BlockSpec((1, 1, dk, dv_a), lambda b, h, t: (b, h, 0, 0))),
        scratch_shapes=[pltpu.VMEM((dv_a, dk), F32)],
        compiler_params=_cparams(grid_sem, 32 << 20),
        name="hgrn2",
    )(*args)

    cosf, sinf = _rotary_tables(pos)
    log_gamma = jnp.log(1.0 - jnp.exp2(-5.0 - jnp.arange(b_heads, dtype=F32)))
    lg = jnp.broadcast_to(log_gamma[:, None, None], (b_heads, 1, dv_b))
    in_specs = [col(cqb), col(ckb), colw(cvb), colw(cgb),
                pl.BlockSpec((rows, HEAD_LANES), lambda b, h, t: (t, 0)),
                pl.BlockSpec((rows, HEAD_LANES), lambda b, h, t: (t, 0)),
                pl.BlockSpec((1, 1, dv_b), lambda b, h, t: (h, 0, 0)),
                pl.BlockSpec((1, dv_b), lambda b, h, t: (0, h))]
    args = [proj, proj, proj, proj, cosf, sinf, lg, g_b.reshape(1, b_heads * dv_b)]
    if has_state:
        in_specs.append(pl.BlockSpec((1, 1, HEAD_LANES, dv_b), lambda b, h, t: (b, h, 0, 0)))
        args.append(s_b)
    o_b, new_b = pl.pallas_call(
        functools.partial(_ret_kernel, chunk=chunk, n_chunks=rows // chunk, has_state=has_state),
        out_shape=(jax.ShapeDtypeStruct((m, b_heads * dv_b), BF16),
                   jax.ShapeDtypeStruct((batch, b_heads, HEAD_LANES, dv_b), F32)),
        grid=(batch, b_heads, nt),
        in_specs=in_specs,
        out_specs=(pl.BlockSpec((rows, dv_b), lambda b, h, t: (b * nt + t, h)),
                   pl.BlockSpec((1, 1, HEAD_LANES, dv_b), lambda b, h, t: (b, h, 0, 0))),
        scratch_shapes=[pltpu.VMEM((HEAD_LANES, dv_b), F32)],
        compiler_params=_cparams(grid_sem, 32 << 20),
        name="retention",
    )(*args)
    return o_a, o_b, new_a, new_b


SUBLANES = 8


def _griffin_kernel(*refs, rows, first_pos_zero, has_state):
    if has_state:
        (xb_ref, yb_ref, cw_ref, cb_ref, wa_ref, wx_ref, gab_ref, gxb_ref, lam_ref, sconv_ref, h0_ref,
         o_ref, hlast_ref, nconv_ref, xpad_ref, h_ref, a_ref, b_ref) = refs
    else:
        (xb_ref, yb_ref, cw_ref, cb_ref, wa_ref, wx_ref, gab_ref, gxb_ref, lam_ref,
         o_ref, hlast_ref, nconv_ref, xpad_ref, h_ref, a_ref, b_ref) = refs
        sconv_ref = h0_ref = None
    t = pl.program_id(2)
    width = cw_ref.shape[0]
    lead = SUBLANES - (width - 1)

    @pl.when(t == 0)
    def _():
        if has_state:
            xpad_ref[lead:SUBLANES, :] = sconv_ref[0]
            h_ref[...] = h0_ref[0]
        else:
            xpad_ref[0:SUBLANES, :] = jnp.zeros((SUBLANES, xpad_ref.shape[1]), F32)
            h_ref[...] = jnp.zeros_like(h_ref)

    xpad_ref[SUBLANES:SUBLANES + rows, :] = xb_ref[...]
    cw = cw_ref[...]
    acc = xpad_ref[lead:lead + rows, :] * cw[0:1]
    for j in range(1, width):
        acc = acc + xpad_ref[lead + j:lead + j + rows, :] * cw[j:j + 1]
    conv = cb_ref[...] + acc
    tail = xpad_ref[rows + lead:rows + SUBLANES, :]
    xpad_ref[lead:SUBLANES, :] = tail

    conv_bf = conv.astype(BF16)
    r = jax.nn.sigmoid(jnp.dot(conv_bf, wa_ref[0], preferred_element_type=F32) + gab_ref[...])
    i = jax.nn.sigmoid(jnp.dot(conv_bf, wx_ref[0], preferred_element_type=F32) + gxb_ref[...])
    log_a = -RG_C * r * jax.nn.softplus(-lam_ref[...])
    a = jnp.exp(log_a)
    mult = jnp.sqrt(1.0 - jnp.exp(2.0 * log_a))
    row = lax.broadcasted_iota(jnp.int32, a.shape, 0)
    if first_pos_zero:
        mult = jnp.where(row + t * rows == 0, 1.0, mult)
    bb = mult * (i * conv)

    sub = row % SUBLANES
    shift = 1
    while shift < SUBLANES:
        take = sub >= shift
        bb = jnp.where(take, a * pltpu.roll(bb, shift, 0) + bb, bb)
        a = jnp.where(take, a * pltpu.roll(a, shift, 0), a)
        shift *= 2
    a_ref[...] = a
    b_ref[...] = bb

    def group(gidx, carry):
        r0 = pl.multiple_of(gidx * SUBLANES, SUBLANES)
        hg = a_ref[pl.ds(r0, SUBLANES), :] * carry + b_ref[pl.ds(r0, SUBLANES), :]
        b_ref[pl.ds(r0, SUBLANES), :] = hg
        return hg[SUBLANES - 1:SUBLANES, :]

    h_ref[...] = lax.fori_loop(0, rows // SUBLANES, group, h_ref[...])
    hs = b_ref[...]
    o_ref[...] = (jax.nn.gelu(yb_ref[...], approximate=True) * hs).astype(o_ref.dtype)

    @pl.when(t == pl.num_programs(2) - 1)
    def _():
        hlast_ref[0] = h_ref[...]
        nconv_ref[0] = tail


def _pair_blockdiag(w, pair):
    n, c, _ = w.shape
    w = w.reshape(n // pair, pair, c, c)
    rows = []
    for p in range(pair):
        blocks = [w[:, p] if q == p else jnp.zeros_like(w[:, p]) for q in range(pair)]
        rows.append(jnp.concatenate(blocks, axis=-1))
    return jnp.concatenate(rows, axis=-2)


def _odd_mixer(xy, batch, seq, pos0, h0, s_conv, conv_w, conv_b, wa, wx, ga_b, gx_b, lam, rows_per_step=256):
    m = xy.shape[0]
    d_rnn = conv_w.shape[1]
    width = conv_w.shape[0]
    blk = wa.shape[1]
    nblk = d_rnn // blk
    rows = min(rows_per_step, seq)
    assert seq % rows == 0 and rows % SUBLANES == 0 and rows >= width - 1
    nt = seq // rows
    has_state = h0 is not None

    def vec(a):
        return a.reshape(1, d_rnn)

    def vspec():
        return pl.BlockSpec((1, blk), lambda b, j, t: (0, j))

    in_specs = [pl.BlockSpec((rows, blk), lambda b, j, t: (b * nt + t, j)),
                pl.BlockSpec((rows, blk), lambda b, j, t: (b * nt + t, nblk + j)),
                pl.BlockSpec((width, blk), lambda b, j, t: (0, j)),
                vspec(),
                pl.BlockSpec((1, blk, blk), lambda b, j, t: (j, 0, 0)),
                pl.BlockSpec((1, blk, blk), lambda b, j, t: (j, 0, 0)),
                vspec(), vspec(), vspec()]
    args = [xy, xy, conv_w, vec(conv_b), wa, wx, vec(ga_b), vec(gx_b), vec(lam)]
    if has_state:
        in_specs += [pl.BlockSpec((1, width - 1, blk), lambda b, j, t: (b, 0, j)),
                     pl.BlockSpec((1, 1, blk), lambda b, j, t: (b, 0, j))]
        args += [s_conv, h0.reshape(batch, 1, d_rnn)]
    o, h_last, new_conv = pl.pallas_call(
        functools.partial(_griffin_kernel, rows=rows, first_pos_zero=(pos0 == 0), has_state=has_state),
        out_shape=(jax.ShapeDtypeStruct((m, d_rnn), BF16),
                   jax.ShapeDtypeStruct((batch, 1, d_rnn), F32),
                   jax.ShapeDtypeStruct((batch, width - 1, d_rnn), F32)),
        grid=(batch, nblk, nt),
        in_specs=in_specs,
        out_specs=(pl.BlockSpec((rows, blk), lambda b, j, t: (b * nt + t, j)),
                   pl.BlockSpec((1, 1, blk), lambda b, j, t: (b, 0, j)),
                   pl.BlockSpec((1, width - 1, blk), lambda b, j, t: (b, 0, j))),
        scratch_shapes=[pltpu.VMEM((rows + SUBLANES, blk), F32),
                        pltpu.VMEM((1, blk), F32),
                        pltpu.VMEM((rows, blk), F32),
                        pltpu.VMEM((rows, blk), F32)],
        compiler_params=_cparams(("parallel", "parallel", "arbitrary"), 32 << 20),
        name="griffin",
    )(*args)
    return o, h_last.reshape(batch, d_rnn), new_conv


def _trunk(x, pos0, states, p):
    batch, seq, d = x.shape
    depth = p["norm_mix"].shape[0]
    pos = pos0 + jnp.arange(seq, dtype=jnp.int32)
    h = x.reshape(batch * seq, d)
    n_hgrn, n_ret, n_rg, n_conv = [], [], [], []
    hn = _rms_norm(h, p["norm_mix"][0], BF16)
    for l in range(depth):
        if l % 2 == 0:
            e = l // 2
            proj = _matmul(hn, p["w_in_even"][e])
            s_a, s_b = (None, None) if states is None else (states[0][e], states[1][e])
            o_a, o_b, sa, sb = _even_mixer(proj, batch, seq, pos, l, s_a, s_b, p["hgrn_shape"], p["ret_shape"],
                                           p["hgrn_lb_logits"], p["hgrn_norm"][e], p["ret_norm"][e])
            n_hgrn.append(sa)
            n_ret.append(sb)
            o = jnp.concatenate([o_a, o_b], axis=-1)
            h = _matmul(o, p["w_out_even"][e], mode="residual", res=h)
        else:
            od = l // 2
            xy = _matmul(hn, p["w_in_odd"][od])
            h0, sc = (None, None) if states is None else (states[2][od], states[3][od])
            o, sh, scn = _odd_mixer(xy, batch, seq, pos0, h0, sc, p["conv_w"][od], p["conv_b"][od],
                                    p["gate_a_w"][od], p["gate_x_w"][od], p["gate_a_b"][od], p["gate_x_b"][od],
                                    p["rglru_lambda"][od])
            n_rg.append(sh)
            n_conv.append(scn)
            h = _matmul(o, p["w_out_odd"][od], mode="residual", res=h, tk=2560)
        hm = _rms_norm(h, p["norm_mlp"][l], BF16)
        up = _matmul(hm, p["w_up"][l], mode="relu2", out_dtype=BF16)
        h = _matmul(up, p["w_down"][l], mode="residual", res=h, tk=2048)
        if l + 1 < depth:
            hn = _rms_norm(h, p["norm_mix"][l + 1], BF16)
    out = _rms_norm(h, p["norm_final"], x.dtype).reshape(batch, seq, d)
    return out, jnp.stack(n_hgrn), jnp.stack(n_ret), jnp.stack(n_rg), jnp.stack(n_conv)


def kernel(x_prompt, x_sample, state_hgrn, state_ret, state_rglru, state_conv, norm_mix, norm_mlp, norm_final,
           w_in_even, hgrn_lb_logits, hgrn_norm, ret_norm, w_out_even, w_in_odd, conv_w, conv_b, gate_a_w,
           gate_a_b, gate_x_w, gate_x_b, rglru_lambda, w_out_odd, w_up, w_down):
    past_len = 4096
    p = dict(
        norm_mix=norm_mix, norm_mlp=norm_mlp, norm_final=norm_final,
        w_in_even=w_in_even.astype(BF16), w_out_even=w_out_even.astype(BF16),
        w_in_odd=w_in_odd.astype(BF16), w_out_odd=w_out_odd.astype(BF16),
        w_up=w_up.astype(BF16), w_down=w_down.astype(BF16),
        hgrn_lb_logits=hgrn_lb_logits, hgrn_norm=hgrn_norm, ret_norm=ret_norm,
        conv_w=conv_w, conv_b=conv_b,
        gate_a_w=jnp.stack([_pair_blockdiag(w, RG_PAIR) for w in gate_a_w.astype(BF16)]),
        gate_x_w=jnp.stack([_pair_blockdiag(w, RG_PAIR) for w in gate_x_w.astype(BF16)]),
        gate_a_b=gate_a_b, gate_x_b=gate_x_b, rglru_lambda=rglru_lambda,
        hgrn_shape=state_hgrn.shape[2:], ret_shape=state_ret.shape[2:],
    )
    y_p, hg_p, rt_p, rg_p, cv_p = _trunk(x_prompt, 0, None, p)
    y_s, hg_s, rt_s, rg_s, cv_s = _trunk(x_sample, past_len, (state_hgrn, state_ret, state_rglru, state_conv), p)
    return (y_p, y_s, hg_p, rt_p, rg_p, cv_p, hg_s, rt_s, rg_s, cv_s)
```

```python
import functools

import jax
import jax.numpy as jnp
from jax import lax
from jax.experimental import pallas as pl
from jax.experimental.pallas import tpu as pltpu

EPS = 1e-6
CHUNK = 64
ROPE_BASE = 10000.0
RG_C = 8.0
HEAD_LANES = 128
RG_PAIR = 2

V7X_VMEM_BYTES = 64 * 1024 * 1024
VMEM_CAP_BYTES = V7X_VMEM_BYTES - 6 * 1024 * 1024

F32 = jnp.float32
BF16 = jnp.bfloat16


def _cparams(semantics, vmem_bytes):
    return pltpu.CompilerParams(dimension_semantics=semantics,
                                vmem_limit_bytes=int(min(max(vmem_bytes, 32 * 1024 * 1024), VMEM_CAP_BYTES)))


def _rms_kernel(x_ref, g_ref, o_ref):
    x = x_ref[...]
    y = x * lax.rsqrt(jnp.mean(x * x, axis=-1, keepdims=True) + EPS)
    o_ref[...] = (y * g_ref[...]).astype(o_ref.dtype)


def _rms_norm(x, g, out_dtype, tm=256):
    m, d = x.shape
    tm = min(tm, m)
    io_bytes = 2 * tm * d * (4 + jnp.dtype(out_dtype).itemsize)
    return pl.pallas_call(
        _rms_kernel,
        out_shape=jax.ShapeDtypeStruct((m, d), out_dtype),
        grid=(m // tm,),
        in_specs=[pl.BlockSpec((tm, d), lambda i: (i, 0)),
                  pl.BlockSpec((1, d), lambda i: (0, 0))],
        out_specs=pl.BlockSpec((tm, d), lambda i: (i, 0)),
        compiler_params=_cparams(("parallel",), io_bytes + 3 * tm * d * 4),
        name="rms_norm",
    )(x, g.reshape(1, d))


def _epilogue(acc, mode, res_ref, o_ref):
    if mode == "relu2":
        acc = jnp.square(jnp.maximum(acc, 0.0))
    elif mode == "residual":
        acc = res_ref[...] + acc
    o_ref[...] = acc.astype(o_ref.dtype)


def _mm_kernel_single(*refs, mode, k_sizes):
    n_x = len(k_sizes)
    x_refs, w_ref = refs[:n_x], refs[n_x]
    if mode == "residual":
        res_ref, o_ref = refs[n_x + 1:]
    else:
        (o_ref,), res_ref = refs[n_x + 1:], None
    acc, off = None, 0
    for x_ref, ks in zip(x_refs, k_sizes):
        part = jnp.dot(x_ref[...], w_ref[off:off + ks, :], preferred_element_type=F32)
        acc = part if acc is None else acc + part
        off += ks
    _epilogue(acc, mode, res_ref, o_ref)


def _mm_kernel_multi(*refs, mode):
    if mode == "residual":
        x_ref, w_ref, res_ref, o_ref, acc_ref = refs
    else:
        (x_ref, w_ref, o_ref, acc_ref), res_ref = refs, None
    k = pl.program_id(2)

    @pl.when(k == 0)
    def _():
        acc_ref[...] = jnp.zeros_like(acc_ref)

    acc_ref[...] += jnp.dot(x_ref[...], w_ref[...], preferred_element_type=F32)

    @pl.when(k == pl.num_programs(2) - 1)
    def _():
        _epilogue(acc_ref[...], mode, res_ref, o_ref)


def _matmul(x, w, *, mode="none", res=None, out_dtype=F32, tm=1024, tn=1024, tk=4096):
    xs = x if isinstance(x, (tuple, list)) else (x,)
    m = xs[0].shape[0]
    k_sizes = tuple(xi.shape[1] for xi in xs)
    kdim, n = w.shape
    assert sum(k_sizes) == kdim
    tm, tn, tk = min(tm, m), min(tn, n), min(tk, kdim)
    nk = kdim // tk
    assert m % tm == 0 and n % tn == 0 and kdim % tk == 0
    out_bytes = jnp.dtype(out_dtype).itemsize
    vmem = 2 * (tm * tk * 2 + tk * tn * 2 + tm * tn * out_bytes) + tm * tn * 4 + (4 << 20)
    if nk == 1:
        grid = (m // tm, n // tn)
        in_specs = [pl.BlockSpec((tm, ks), lambda i, j: (i, 0)) for ks in k_sizes]
        in_specs.append(pl.BlockSpec((tk, tn), lambda i, j: (0, j)))
        res_spec = pl.BlockSpec((tm, tn), lambda i, j: (i, j))
        out_spec = pl.BlockSpec((tm, tn), lambda i, j: (i, j))
        body = functools.partial(_mm_kernel_single, mode=mode, k_sizes=k_sizes)
        scratch, sem = [], ("parallel", "parallel")
    else:
        assert len(xs) == 1
        grid = (m // tm, n // tn, nk)
        in_specs = [pl.BlockSpec((tm, tk), lambda i, j, k: (i, k)),
                    pl.BlockSpec((tk, tn), lambda i, j, k: (k, j))]
        res_spec = pl.BlockSpec((tm, tn), lambda i, j, k: (i, j))
        out_spec = pl.BlockSpec((tm, tn), lambda i, j, k: (i, j))
        body = functools.partial(_mm_kernel_multi, mode=mode)
        scratch, sem = [pltpu.VMEM((tm, tn), F32)], ("parallel", "parallel", "arbitrary")
        vmem += tm * tn * 4
    args = list(xs) + [w]
    if mode == "residual":
        in_specs.append(res_spec)
        args.append(res)
        vmem += 2 * tm * tn * 4
    return pl.pallas_call(
        body,
        out_shape=jax.ShapeDtypeStruct((m, n), out_dtype),
        grid=grid,
        in_specs=in_specs,
        out_specs=out_spec,
        scratch_shapes=scratch,
        compiler_params=_cparams(sem, vmem),
        name="proj_" + mode,
    )(*args)


def _chunk_pos(shape, chunk):
    return lax.broadcasted_iota(jnp.int32, shape, 0) % chunk


def _cumsum_in_chunks(x, chunk):
    pos = _chunk_pos(x.shape, chunk)
    shift = 1
    while shift < chunk:
        x = x + jnp.where(pos >= shift, pltpu.roll(x, shift, 0), 0.0)
        shift *= 2
    return x


def _causal_mask(chunk):
    r = lax.broadcasted_iota(jnp.int32, (chunk, chunk), 0)
    c = lax.broadcasted_iota(jnp.int32, (chunk, chunk), 1)
    return r >= c


_NT = (((1,), (1,)), ((), ()))
_TN = (((0,), (0,)), ((), ()))


def _lower_bound(logits, layer):
    rows = [logits[i:i + 1] for i in range(logits.shape[0])]
    mx = functools.reduce(jnp.maximum, rows)
    es = [jnp.exp(r - mx) for r in rows]
    den = functools.reduce(lambda a, b: a + b, es)
    lb = es[0] / den
    for i in range(1, layer + 1):
        lb = lb + es[i] / den
    return lb


def _hgrn_kernel(*refs, chunk, n_chunks, layer, has_state, heads):
    if has_state:
        q_ref, f_ref, v_ref, g_ref, lbl_ref, gn_ref, s0_ref, o_ref, sout_ref, st_ref = refs
    else:
        (q_ref, f_ref, v_ref, g_ref, lbl_ref, gn_ref, o_ref, sout_ref, st_ref), s0_ref = refs, None
    t = pl.program_id(2)
    last = t == pl.num_programs(2) - 1
    mask = _causal_mask(chunk)

    @pl.when(t == 0)
    def _():
        for hh in range(heads):
            if has_state:
                st_ref[hh] = s0_ref[0, hh].T
            else:
                st_ref[hh] = jnp.zeros(st_ref.shape[1:], F32)

    finals = []
    for hh in range(heads):
        ls = slice(hh * HEAD_LANES, (hh + 1) * HEAD_LANES)
        lb = _lower_bound(lbl_ref[:, ls], layer)
        f = lb + (1.0 - lb) * jax.nn.sigmoid(f_ref[:, ls])
        q = jax.nn.silu(q_ref[:, ls])
        k = 1.0 - f
        b = _cumsum_in_chunks(jnp.log(f), chunk)
        q_in = (q * jnp.exp(b)).astype(BF16)
        k_in = (k * jnp.exp(-b)).astype(BF16)
        v = v_ref[:, ls].astype(BF16)
        st = st_ref[hh]
        outs = []
        for c in range(n_chunks):
            sl = slice(c * chunk, (c + 1) * chunk)
            b_c = b[sl]
            b_last = b_c[chunk - 1:chunk]
            k_out = (k[sl] * jnp.exp(b_last - b_c)).astype(BF16)
            attn = lax.dot_general(q_in[sl], k_in[sl], _NT, preferred_element_type=F32)
            attn = jnp.where(mask, attn, 0.0).astype(BF16)
            o_intra = jnp.dot(attn, v[sl], preferred_element_type=F32)
            o_inter = lax.dot_general(q_in[sl], st.astype(BF16), _NT, preferred_element_type=F32)
            kv_t = lax.dot_general(v[sl], k_out, _TN, preferred_element_type=F32)
            st = jnp.exp(b_last) * st + kv_t
            outs.append(o_intra + o_inter)
        st_ref[hh] = st
        o = outs[0] if n_chunks == 1 else jnp.concatenate(outs, axis=0)
        o = o * lax.rsqrt(jnp.mean(o * o, axis=-1, keepdims=True) + EPS)
        o = o * gn_ref[:, ls] * jax.nn.silu(g_ref[:, ls])
        o_ref[:, ls] = o.astype(o_ref.dtype)
        finals.append(st)

    @pl.when(last)
    def _():
        for hh in range(heads):
            sout_ref[0, hh] = finals[hh].T


def _ret_kernel(*refs, chunk, n_chunks, has_state, heads, dv):
    if has_state:
        q_ref, k_ref, v_ref, g_ref, cos_ref, sin_ref, lg_ref, gn_ref, s0_ref, o_ref, sout_ref, st_ref = refs
    else:
        (q_ref, k_ref, v_ref, g_ref, cos_ref, sin_ref, lg_ref, gn_ref, o_ref, sout_ref, st_ref), s0_ref = refs, None
    t = pl.program_id(2)
    last = t == pl.num_programs(2) - 1
    cosf, sinf = cos_ref[...], sin_ref[...]
    half = HEAD_LANES // 2
    mask = _causal_mask(chunk)
    pos1 = (_chunk_pos(cosf.shape, chunk) + 1).astype(F32)

    @pl.when(t == 0)
    def _():
        for hh in range(heads):
            if has_state:
                st_ref[hh] = s0_ref[0, hh]
            else:
                st_ref[hh] = jnp.zeros(st_ref.shape[1:], F32)

    finals = []
    for hh in range(heads):
        ls = slice(hh * HEAD_LANES, (hh + 1) * HEAD_LANES)
        ws = slice(hh * dv, (hh + 1) * dv)
        q, k = q_ref[:, ls], k_ref[:, ls]
        qr = q * cosf + pltpu.roll(q, half, 1) * sinf
        kr = (k * cosf + pltpu.roll(k, half, 1) * sinf) * (HEAD_LANES ** -0.5)
        lg_wide = lg_ref[hh]
        lg = lg_wide[:, :HEAD_LANES]
        b = pos1 * lg
        b_last = float(chunk) * lg
        q_in = (qr * jnp.exp(b)).astype(BF16)
        k_in = (kr * jnp.exp(-b)).astype(BF16)
        k_out = (kr * jnp.exp(b_last - b)).astype(BF16)
        decay = jnp.exp(float(chunk) * lg_wide)
        v = v_ref[:, ws].astype(BF16)
        st = st_ref[hh]
        outs = []
        for c in range(n_chunks):
            sl = slice(c * chunk, (c + 1) * chunk)
            attn = lax.dot_general(q_in[sl], k_in[sl], _NT, preferred_element_type=F32)
            attn = jnp.where(mask, attn, 0.0).astype(BF16)
            o_intra = jnp.dot(attn, v[sl], preferred_element_type=F32)
            o_inter = jnp.dot(q_in[sl], st.astype(BF16), preferred_element_type=F32)
            kv = lax.dot_general(k_out[sl], v[sl], _TN, preferred_element_type=F32)
            st = decay * st + kv
            outs.append(o_intra + o_inter)
        st_ref[hh] = st
        o = outs[0] if n_chunks == 1 else jnp.concatenate(outs, axis=0)
        c0 = o - jnp.mean(o, axis=-1, keepdims=True)
        o = c0 * lax.rsqrt(jnp.mean(c0 * c0, axis=-1, keepdims=True) + EPS)
        o = o * gn_ref[:, ws] * jax.nn.silu(g_ref[:, ws])
        o_ref[:, ws] = o.astype(o_ref.dtype)
        finals.append(st)

    @pl.when(last)
    def _():
        for hh in range(heads):
            sout_ref[0, hh] = finals[hh]


def _rotary_tables(pos):
    half = HEAD_LANES // 2
    inv = 1.0 / (ROPE_BASE ** jnp.linspace(0.0, 1.0, half, dtype=F32))
    ang = pos.astype(F32)[:, None] * inv[None, :]
    cos, sin = jnp.cos(ang), jnp.sin(ang)
    return jnp.concatenate([cos, cos], axis=-1), jnp.concatenate([-sin, sin], axis=-1)


def _even_mixer(proj, batch, seq, pos, layer, s_a, s_b, a_shape, b_shape, lb_logits, g_a, g_b,
                rows_per_step=512, heads_per_step=4):
    m = proj.shape[0]
    a_heads, dk, dv_a = a_shape
    b_heads, dk_b, dv_b = b_shape
    assert dk == HEAD_LANES and dv_a == HEAD_LANES and dk_b == HEAD_LANES
    chunk = min(CHUNK, seq)
    rows = min(rows_per_step, seq)
    assert seq % rows == 0 and rows % chunk == 0
    nt = seq // rows
    has_state = s_a is not None
    hp_a = a_heads if nt == 1 else heads_per_step
    hp_b = b_heads if nt == 1 else heads_per_step
    assert a_heads % hp_a == 0 and b_heads % hp_b == 0
    a_w, a_v = a_heads * dk, a_heads * dv_a
    b_qk, b_v = b_heads * HEAD_LANES, b_heads * dv_b
    off_b = 2 * a_w + 2 * a_v
    grid_sem = ("parallel", "parallel", "arbitrary")

    def cols(offset, width, hp):
        blk = hp * width
        assert offset % blk == 0
        return pl.BlockSpec((rows, blk), lambda b, h, t: (b * nt + t, offset // blk + h))

    n_layers = lb_logits.shape[0]
    wa = hp_a * HEAD_LANES
    in_specs = [cols(0, dk, hp_a), cols(a_w, dk, hp_a), cols(2 * a_w, dv_a, hp_a), cols(2 * a_w + a_v, dv_a, hp_a),
                pl.BlockSpec((n_layers, wa), lambda b, h, t: (0, h)),
                pl.BlockSpec((1, wa), lambda b, h, t: (0, h))]
    args = [proj, proj, proj, proj, lb_logits, g_a.reshape(1, a_v)]
    if has_state:
        in_specs.append(pl.BlockSpec((1, hp_a, dk, dv_a), lambda b, h, t: (b, h, 0, 0)))
        args.append(s_a)
    o_a, new_a = pl.pallas_call(
        functools.partial(_hgrn_kernel, chunk=chunk, n_chunks=rows // chunk, layer=layer, has_state=has_state,
                          heads=hp_a),
        out_shape=(jax.ShapeDtypeStruct((m, a_v), BF16),
                   jax.ShapeDtypeStruct((batch, a_heads, dk, dv_a), F32)),
        grid=(batch, a_heads // hp_a, nt),
        in_specs=in_specs,
        out_specs=(pl.BlockSpec((rows, wa), lambda b, h, t: (b * nt + t, h)),
                   pl.BlockSpec((1, hp_a, dk, dv_a), lambda b, h, t: (b, h, 0, 0))),
        scratch_shapes=[pltpu.VMEM((hp_a, dv_a, dk), F32)],
        compiler_params=_cparams(grid_sem, 32 << 20),
        name="hgrn2",
    )(*args)

    cosf, sinf = _rotary_tables(pos)
    log_gamma = jnp.log(1.0 - jnp.exp2(-5.0 - jnp.arange(b_heads, dtype=F32)))
    lg = jnp.broadcast_to(log_gamma[:, None, None], (b_heads, 1, dv_b))
    wv = hp_b * dv_b
    in_specs = [cols(off_b, HEAD_LANES, hp_b), cols(off_b + b_qk, HEAD_LANES, hp_b),
                cols(off_b + 2 * b_qk, dv_b, hp_b), cols(off_b + 2 * b_qk + b_v, dv_b, hp_b),
                pl.BlockSpec((rows, HEAD_LANES), lambda b, h, t: (t, 0)),
                pl.BlockSpec((rows, HEAD_LANES), lambda b, h, t: (t, 0)),
                pl.BlockSpec((hp_b, 1, dv_b), lambda b, h, t: (h, 0, 0)),
                pl.BlockSpec((1, wv), lambda b, h, t: (0, h))]
    args = [proj, proj, proj, proj, cosf, sinf, lg, g_b.reshape(1, b_v)]
    if has_state:
        in_specs.append(pl.BlockSpec((1, hp_b, HEAD_LANES, dv_b), lambda b, h, t: (b, h, 0, 0)))
        args.append(s_b)
    o_b, new_b = pl.pallas_call(
        functools.partial(_ret_kernel, chunk=chunk, n_chunks=rows // chunk, has_state=has_state, heads=hp_b,
                          dv=dv_b),
        out_shape=(jax.ShapeDtypeStruct((m, b_v), BF16),
                   jax.ShapeDtypeStruct((batch, b_heads, HEAD_LANES, dv_b), F32)),
        grid=(batch, b_heads // hp_b, nt),
        in_specs=in_specs,
        out_specs=(pl.BlockSpec((rows, wv), lambda b, h, t: (b * nt + t, h)),
                   pl.BlockSpec((1, hp_b, HEAD_LANES, dv_b), lambda b, h, t: (b, h, 0, 0))),
        scratch_shapes=[pltpu.VMEM((hp_b, HEAD_LANES, dv_b), F32)],
        compiler_params=_cparams(grid_sem, 32 << 20),
        name="retention",
    )(*args)
    return o_a, o_b, new_a, new_b


SUBLANES = 8


def _griffin_kernel(*refs, rows, first_pos_zero, has_state):
    if has_state:
        (xb_ref, yb_ref, cw_ref, cb_ref, wa_ref, wx_ref, gab_ref, gxb_ref, lam_ref, sconv_ref, h0_ref,
         o_ref, hlast_ref, nconv_ref, xpad_ref, h_ref, a_ref, b_ref) = refs
    else:
        (xb_ref, yb_ref, cw_ref, cb_ref, wa_ref, wx_ref, gab_ref, gxb_ref, lam_ref,
         o_ref, hlast_ref, nconv_ref, xpad_ref, h_ref, a_ref, b_ref) = refs
        sconv_ref = h0_ref = None
    t = pl.program_id(2)
    width = cw_ref.shape[0]
    lead = SUBLANES - (width - 1)

    @pl.when(t == 0)
    def _():
        if has_state:
            xpad_ref[lead:SUBLANES, :] = sconv_ref[0]
            h_ref[...] = h0_ref[0]
        else:
            xpad_ref[0:SUBLANES, :] = jnp.zeros((SUBLANES, xpad_ref.shape[1]), F32)
            h_ref[...] = jnp.zeros_like(h_ref)

    xpad_ref[SUBLANES:SUBLANES + rows, :] = xb_ref[...]
    cw = cw_ref[...]
    acc = xpad_ref[lead:lead + rows, :] * cw[0:1]
    for j in range(1, width):
        acc = acc + xpad_ref[lead + j:lead + j + rows, :] * cw[j:j + 1]
    conv = cb_ref[...] + acc
    tail = xpad_ref[rows + lead:rows + SUBLANES, :]
    xpad_ref[lead:SUBLANES, :] = tail

    conv_bf = conv.astype(BF16)
    r = jax.nn.sigmoid(jnp.dot(conv_bf, wa_ref[0], preferred_element_type=F32) + gab_ref[...])
    i = jax.nn.sigmoid(jnp.dot(conv_bf, wx_ref[0], preferred_element_type=F32) + gxb_ref[...])
    log_a = -RG_C * r * jax.nn.softplus(-lam_ref[...])
    a = jnp.exp(log_a)
    mult = jnp.sqrt(1.0 - a * a)
    if first_pos_zero:
        row = lax.broadcasted_iota(jnp.int32, a.shape, 0)
        mult = jnp.where(row + t * rows == 0, 1.0, mult)
    bb = mult * (i * conv)

    grouped = (rows // SUBLANES, SUBLANES, a.shape[1])
    a3, b3 = a.reshape(grouped), bb.reshape(grouped)
    sub = lax.broadcasted_iota(jnp.int32, grouped, 1)
    shift = 1
    while shift < SUBLANES:
        take = sub >= shift
        b3 = jnp.where(take, a3 * pltpu.roll(b3, shift, 1) + b3, b3)
        a3 = jnp.where(take, a3 * pltpu.roll(a3, shift, 1), a3)
        shift *= 2
    a_ref[...] = a3.reshape(a.shape)
    b_ref[...] = b3.reshape(a.shape)

    def group(gidx, carry):
        r0 = pl.multiple_of(gidx * SUBLANES, SUBLANES)
        hg = a_ref[pl.ds(r0, SUBLANES), :] * carry + b_ref[pl.ds(r0, SUBLANES), :]
        b_ref[pl.ds(r0, SUBLANES), :] = hg
        return hg[SUBLANES - 1:SUBLANES, :]

    h_ref[...] = lax.fori_loop(0, rows // SUBLANES, group, h_ref[...], unroll=True)
    hs = b_ref[...]
    o_ref[...] = (jax.nn.gelu(yb_ref[...], approximate=True) * hs).astype(o_ref.dtype)

    @pl.when(t == pl.num_programs(2) - 1)
    def _():
        hlast_ref[0] = h_ref[...]
        nconv_ref[0] = tail


def _pair_blockdiag(w, pair):
    n, c, _ = w.shape
    w = w.reshape(n // pair, pair, c, c)
    rows = []
    for p in range(pair):
        blocks = [w[:, p] if q == p else jnp.zeros_like(w[:, p]) for q in range(pair)]
        rows.append(jnp.concatenate(blocks, axis=-1))
    return jnp.concatenate(rows, axis=-2)


def _odd_mixer(xy, batch, seq, pos0, h0, s_conv, conv_w, conv_b, wa, wx, ga_b, gx_b, lam, rows_per_step=256):
    m = xy.shape[0]
    d_rnn = conv_w.shape[1]
    width = conv_w.shape[0]
    blk = wa.shape[1]
    nblk = d_rnn // blk
    rows = min(rows_per_step, seq)
    assert seq % rows == 0 and rows % SUBLANES == 0 and rows >= width - 1
    nt = seq // rows
    has_state = h0 is not None

    def vec(a):
        return a.reshape(1, d_rnn)

    def vspec():
        return pl.BlockSpec((1, blk), lambda b, j, t: (0, j))

    in_specs = [pl.BlockSpec((rows, blk), lambda b, j, t: (b * nt + t, j)),
                pl.BlockSpec((rows, blk), lambda b, j, t: (b * nt + t, nblk + j)),
                pl.BlockSpec((width, blk), lambda b, j, t: (0, j)),
                vspec(),
                pl.BlockSpec((1, blk, blk), lambda b, j, t: (j, 0, 0)),
                pl.BlockSpec((1, blk, blk), lambda b, j, t: (j, 0, 0)),
                vspec(), vspec(), vspec()]
    args = [xy, xy, conv_w, vec(conv_b), wa, wx, vec(ga_b), vec(gx_b), vec(lam)]
    if has_state:
        in_specs += [pl.BlockSpec((1, width - 1, blk), lambda b, j, t: (b, 0, j)),
                     pl.BlockSpec((1, 1, blk), lambda b, j, t: (b, 0, j))]
        args += [s_conv, h0.reshape(batch, 1, d_rnn)]
    o, h_last, new_conv = pl.pallas_call(
        functools.partial(_griffin_kernel, rows=rows, first_pos_zero=(pos0 == 0), has_state=has_state),
        out_shape=(jax.ShapeDtypeStruct((m, d_rnn), BF16),
                   jax.ShapeDtypeStruct((batch, 1, d_rnn), F32),
                   jax.ShapeDtypeStruct((batch, width - 1, d_rnn), F32)),
        grid=(batch, nblk, nt),
        in_specs=in_specs,
        out_specs=(pl.BlockSpec((rows, blk), lambda b, j, t: (b * nt + t, j)),
                   pl.BlockSpec((1, 1, blk), lambda b, j, t: (b, 0, j)),
                   pl.BlockSpec((1, width - 1, blk), lambda b, j, t: (b, 0, j))),
        scratch_shapes=[pltpu.VMEM((rows + SUBLANES, blk), F32),
                        pltpu.VMEM((1, blk), F32),
                        pltpu.VMEM((rows, blk), F32),
                        pltpu.VMEM((rows, blk), F32)],
        compiler_params=_cparams(("parallel", "parallel", "arbitrary"), 32 << 20),
        name="griffin",
    )(*args)
    return o, h_last.reshape(batch, d_rnn), new_conv


def _trunk(x, pos0, states, p):
    batch, seq, d = x.shape
    depth = p["norm_mix"].shape[0]
    pos = pos0 + jnp.arange(seq, dtype=jnp.int32)
    h = x.reshape(batch * seq, d)
    n_hgrn, n_ret, n_rg, n_conv = [], [], [], []
    hn = _rms_norm(h, p["norm_mix"][0], BF16)
    for l in range(depth):
        if l % 2 == 0:
            e = l // 2
            proj = _matmul(hn, p["w_in_even"][e])
            s_a, s_b = (None, None) if states is None else (states[0][e], states[1][e])
            o_a, o_b, sa, sb = _even_mixer(proj, batch, seq, pos, l, s_a, s_b, p["hgrn_shape"], p["ret_shape"],
                                           p["hgrn_lb_logits"], p["hgrn_norm"][e], p["ret_norm"][e])
            n_hgrn.append(sa)
            n_ret.append(sb)
            h = _matmul((o_a, o_b), p["w_out_even"][e], mode="residual", res=h)
        else:
            od = l // 2
            xy = _matmul(hn, p["w_in_odd"][od])
            h0, sc = (None, None) if states is None else (states[2][od], states[3][od])
            o, sh, scn = _odd_mixer(xy, batch, seq, pos0, h0, sc, p["conv_w"][od], p["conv_b"][od],
                                    p["gate_a_w"][od], p["gate_x_w"][od], p["gate_a_b"][od], p["gate_x_b"][od],
                                    p["rglru_lambda"][od])
            n_rg.append(sh)
            n_conv.append(scn)
            h = _matmul(o, p["w_out_odd"][od], mode="residual", res=h, tn=512, tk=p["w_out_odd"][od].shape[0])
        hm = _rms_norm(h, p["norm_mlp"][l], BF16)
        up = _matmul(hm, p["w_up"][l], mode="relu2", out_dtype=BF16)
        h = _matmul(up, p["w_down"][l], mode="residual", res=h, tk=2048)
        if l + 1 < depth:
            hn = _rms_norm(h, p["norm_mix"][l + 1], BF16)
    out = _rms_norm(h, p["norm_final"], x.dtype).reshape(batch, seq, d)
    return out, jnp.stack(n_hgrn), jnp.stack(n_ret), jnp.stack(n_rg), jnp.stack(n_conv)


def kernel(x_prompt, x_sample, state_hgrn, state_ret, state_rglru, state_conv, norm_mix, norm_mlp, norm_final,
           w_in_even, hgrn_lb_logits, hgrn_norm, ret_norm, w_out_even, w_in_odd, conv_w, conv_b, gate_a_w,
           gate_a_b, gate_x_w, gate_x_b, rglru_lambda, w_out_odd, w_up, w_down):
    past_len = 4096

    def per_layer_bf16(w):
        return [w[i].astype(BF16) for i in range(w.shape[0])]

    p = dict(
        norm_mix=norm_mix, norm_mlp=norm_mlp, norm_final=norm_final,
        w_in_even=per_layer_bf16(w_in_even), w_out_even=per_layer_bf16(w_out_even),
        w_in_odd=per_layer_bf16(w_in_odd), w_out_odd=per_layer_bf16(w_out_odd),
        w_up=per_layer_bf16(w_up), w_down=per_layer_bf16(w_down),
        hgrn_lb_logits=hgrn_lb_logits, hgrn_norm=hgrn_norm, ret_norm=ret_norm,
        conv_w=conv_w, conv_b=conv_b,
        gate_a_w=jnp.stack([_pair_blockdiag(w, RG_PAIR) for w in gate_a_w.astype(BF16)]),
        gate_x_w=jnp.stack([_pair_blockdiag(w, RG_PAIR) for w in gate_x_w.astype(BF16)]),
        gate_a_b=gate_a_b, gate_x_b=gate_x_b, rglru_lambda=rglru_lambda,
        hgrn_shape=state_hgrn.shape[2:], ret_shape=state_ret.shape[2:],
    )
    y_p, hg_p, rt_p, rg_p, cv_p = _trunk(x_prompt, 0, None, p)
    y_s, hg_s, rt_s, rg_s, cv_s = _trunk(x_sample, past_len, (state_hgrn, state_ret, state_rglru, state_conv), p)
    return (y_p, y_s, hg_p, rt_p, rg_p, cv_p, hg_s, rt_s, rg_s, cv_s)
```

```python
import functools

import jax
import jax.numpy as jnp
from jax import lax
from jax.experimental import pallas as pl
from jax.experimental.pallas import tpu as pltpu

EPS = 1e-6
CHUNK = 64
ROPE_BASE = 10000.0
RG_C = 8.0
HEAD_LANES = 128
RG_PAIR = 2

V7X_VMEM_BYTES = 64 * 1024 * 1024
VMEM_CAP_BYTES = V7X_VMEM_BYTES - 6 * 1024 * 1024

F32 = jnp.float32
BF16 = jnp.bfloat16


def _cparams(semantics, vmem_bytes):
    return pltpu.CompilerParams(dimension_semantics=semantics,
                                vmem_limit_bytes=int(min(max(vmem_bytes, 32 * 1024 * 1024), VMEM_CAP_BYTES)))


def _rms_kernel(x_ref, g_ref, o_ref):
    x = x_ref[...]
    y = x * lax.rsqrt(jnp.mean(x * x, axis=-1, keepdims=True) + EPS)
    o_ref[...] = (y * g_ref[...]).astype(o_ref.dtype)


def _rms_norm(x, g, out_dtype, tm=256):
    m, d = x.shape
    tm = min(tm, m)
    io_bytes = 2 * tm * d * (4 + jnp.dtype(out_dtype).itemsize)
    return pl.pallas_call(
        _rms_kernel,
        out_shape=jax.ShapeDtypeStruct((m, d), out_dtype),
        grid=(m // tm,),
        in_specs=[pl.BlockSpec((tm, d), lambda i: (i, 0)),
                  pl.BlockSpec((1, d), lambda i: (0, 0))],
        out_specs=pl.BlockSpec((tm, d), lambda i: (i, 0)),
        compiler_params=_cparams(("parallel",), io_bytes + 3 * tm * d * 4),
        name="rms_norm",
    )(x, g.reshape(1, d))


def _epilogue(acc, mode, res_ref, o_ref):
    if mode == "relu2":
        acc = jnp.square(jnp.maximum(acc, 0.0))
    elif mode == "residual":
        acc = res_ref[...] + acc
    o_ref[...] = acc.astype(o_ref.dtype)


def _split_dot(x_refs, k_sizes, w_rows):
    acc, off = None, 0
    for x_ref, ks in zip(x_refs, k_sizes):
        part = jnp.dot(x_ref[...], w_rows(off, ks), preferred_element_type=F32)
        acc = part if acc is None else acc + part
        off += ks
    return acc


def _mm_kernel_single(*refs, mode, k_sizes):
    n_x = len(k_sizes)
    x_refs, w_ref = refs[:n_x], refs[n_x]
    if mode == "residual":
        res_ref, o_ref = refs[n_x + 1:]
    else:
        (o_ref,), res_ref = refs[n_x + 1:], None
    acc = _split_dot(x_refs, k_sizes, lambda off, ks: w_ref[off:off + ks, :].astype(BF16))
    _epilogue(acc, mode, res_ref, o_ref)


def _mm_kernel_multi(*refs, mode):
    if mode == "residual":
        x_ref, w_ref, res_ref, o_ref, acc_ref = refs
    else:
        (x_ref, w_ref, o_ref, acc_ref), res_ref = refs, None
    k = pl.program_id(2)

    @pl.when(k == 0)
    def _():
        acc_ref[...] = jnp.zeros_like(acc_ref)

    acc_ref[...] += jnp.dot(x_ref[...], w_ref[...].astype(BF16), preferred_element_type=F32)

    @pl.when(k == pl.num_programs(2) - 1)
    def _():
        _epilogue(acc_ref[...], mode, res_ref, o_ref)


def _weight_spec(layer, block, index_map):
    return pl.BlockSpec((None,) + block, lambda *g: (layer,) + tuple(index_map(*g)))


def _matmul(x, w, layer, *, mode="none", res=None, out_dtype=F32, tm=1024, tn=1024, tk=4096):
    xs = x if isinstance(x, (tuple, list)) else (x,)
    m = xs[0].shape[0]
    k_sizes = tuple(xi.shape[1] for xi in xs)
    _, kdim, n = w.shape
    assert sum(k_sizes) == kdim
    tm, tn, tk = min(tm, m), min(tn, n), min(tk, kdim)
    nk = kdim // tk
    assert m % tm == 0 and n % tn == 0 and kdim % tk == 0
    out_bytes = jnp.dtype(out_dtype).itemsize
    w_bytes = jnp.dtype(w.dtype).itemsize
    vmem = 2 * (tm * tk * 2 + tk * tn * w_bytes + tm * tn * out_bytes) + tm * tn * 4 + (4 << 20)
    if w_bytes != 2:
        vmem += tk * tn * 2
    if nk == 1:
        grid = (m // tm, n // tn)
        in_specs = [pl.BlockSpec((tm, ks), lambda i, j: (i, 0)) for ks in k_sizes]
        in_specs.append(_weight_spec(layer, (tk, tn), lambda i, j: (0, j)))
        res_spec = pl.BlockSpec((tm, tn), lambda i, j: (i, j))
        out_spec = pl.BlockSpec((tm, tn), lambda i, j: (i, j))
        body = functools.partial(_mm_kernel_single, mode=mode, k_sizes=k_sizes)
        scratch, sem = [], ("parallel", "parallel")
    else:
        assert len(xs) == 1
        grid = (m // tm, n // tn, nk)
        in_specs = [pl.BlockSpec((tm, tk), lambda i, j, k: (i, k)),
                    _weight_spec(layer, (tk, tn), lambda i, j, k: (k, j))]
        res_spec = pl.BlockSpec((tm, tn), lambda i, j, k: (i, j))
        out_spec = pl.BlockSpec((tm, tn), lambda i, j, k: (i, j))
        body = functools.partial(_mm_kernel_multi, mode=mode)
        scratch, sem = [pltpu.VMEM((tm, tn), F32)], ("parallel", "parallel", "arbitrary")
        vmem += tm * tn * 4
    args = list(xs) + [w]
    if mode == "residual":
        in_specs.append(res_spec)
        args.append(res)
        vmem += 2 * tm * tn * 4
    return pl.pallas_call(
        body,
        out_shape=jax.ShapeDtypeStruct((m, n), out_dtype),
        grid=grid,
        in_specs=in_specs,
        out_specs=out_spec,
        scratch_shapes=scratch,
        compiler_params=_cparams(sem, vmem),
        name="proj_" + mode,
    )(*args)


def _mm_kernel_stationary(*refs, mode, k_sizes, chunk_rows):
    n_x = len(k_sizes)
    x_refs, wchunk_ref = refs[:n_x], refs[n_x]
    if mode == "residual":
        res_ref, o_ref, wbf_ref = refs[n_x + 1:]
    else:
        (o_ref, wbf_ref), res_ref = refs[n_x + 1:], None
    j, i = pl.program_id(0), pl.program_id(1)

    @pl.when(j < pl.num_programs(0) - 1)
    def _():
        r0 = pl.multiple_of(i * chunk_rows, chunk_rows)
        wbf_ref[j % 2, pl.ds(r0, chunk_rows), :] = wchunk_ref[...].astype(BF16)

    @pl.when(j > 0)
    def _():
        w_tile = wbf_ref.at[(j + 1) % 2]
        acc = _split_dot(x_refs, k_sizes, lambda off, ks: w_tile[off:off + ks, :])
        _epilogue(acc, mode, res_ref, o_ref)


def _matmul_stationary(x, w, layer, *, mode="none", res=None, out_dtype=F32, tm=1024, tn=1024):
    xs = x if isinstance(x, (tuple, list)) else (x,)
    m = xs[0].shape[0]
    k_sizes = tuple(xi.shape[1] for xi in xs)
    _, kdim, n = w.shape
    assert sum(k_sizes) == kdim and m % tm == 0 and n % tn == 0
    n_i, n_j = m // tm, n // tn
    assert kdim % n_i == 0
    chunk_rows = kdim // n_i
    assert chunk_rows % 16 == 0
    out_bytes = jnp.dtype(out_dtype).itemsize

    def row_tile(j, i):
        return jnp.where(j == 0, 0, i)

    in_specs = [pl.BlockSpec((tm, ks), lambda j, i: (row_tile(j, i), 0)) for ks in k_sizes]
    in_specs.append(_weight_spec(layer, (chunk_rows, tn), lambda j, i: (i, jnp.minimum(j, n_j - 1))))
    io_spec = pl.BlockSpec((tm, tn), lambda j, i: (row_tile(j, i), jnp.maximum(j - 1, 0)))
    args = list(xs) + [w]
    vmem = (2 * (tm * kdim * 2 + chunk_rows * tn * 4 + tm * tn * out_bytes) + 2 * kdim * tn * 2
            + tm * tn * 4 + (4 << 20))
    if mode == "residual":
        in_specs.append(io_spec)
        args.append(res)
        vmem += 2 * tm * tn * 4
    return pl.pallas_call(
        functools.partial(_mm_kernel_stationary, mode=mode, k_sizes=k_sizes, chunk_rows=chunk_rows),
        out_shape=jax.ShapeDtypeStruct((m, n), out_dtype),
        grid=(n_j + 1, n_i),
        in_specs=in_specs,
        out_specs=io_spec,
        scratch_shapes=[pltpu.VMEM((2, kdim, tn), BF16)],
        compiler_params=_cparams(("arbitrary", "arbitrary"), vmem),
        name="projws_" + mode,
    )(*args)


def _chunk_pos(shape, chunk):
    return lax.broadcasted_iota(jnp.int32, shape, 0) % chunk


def _cumsum_in_chunks(x, chunk):
    pos = _chunk_pos(x.shape, chunk)
    shift = 1
    while shift < chunk:
        x = x + jnp.where(pos >= shift, pltpu.roll(x, shift, 0), 0.0)
        shift *= 2
    return x


def _causal_mask(chunk):
    r = lax.broadcasted_iota(jnp.int32, (chunk, chunk), 0)
    c = lax.broadcasted_iota(jnp.int32, (chunk, chunk), 1)
    return r >= c


_NT = (((1,), (1,)), ((), ()))
_TN = (((0,), (0,)), ((), ()))


def _lower_bound(logits, layer):
    rows = [logits[i:i + 1] for i in range(logits.shape[0])]
    mx = functools.reduce(jnp.maximum, rows)
    es = [jnp.exp(r - mx) for r in rows]
    den = functools.reduce(lambda a, b: a + b, es)
    lb = es[0] / den
    for i in range(1, layer + 1):
        lb = lb + es[i] / den
    return lb


def _hgrn_kernel(*refs, chunk, n_chunks, layer, has_state, heads):
    if has_state:
        q_ref, f_ref, v_ref, g_ref, lbl_ref, gn_ref, s0_ref, o_ref, sout_ref, st_ref = refs
    else:
        (q_ref, f_ref, v_ref, g_ref, lbl_ref, gn_ref, o_ref, sout_ref, st_ref), s0_ref = refs, None
    t = pl.program_id(2)
    last = t == pl.num_programs(2) - 1
    mask = _causal_mask(chunk)

    @pl.when(t == 0)
    def _():
        for hh in range(heads):
            if has_state:
                st_ref[hh] = s0_ref[0, hh].T
            else:
                st_ref[hh] = jnp.zeros(st_ref.shape[1:], F32)

    finals = []
    for hh in range(heads):
        ls = slice(hh * HEAD_LANES, (hh + 1) * HEAD_LANES)
        lb = _lower_bound(lbl_ref[:, ls], layer)
        f = lb + (1.0 - lb) * jax.nn.sigmoid(f_ref[:, ls])
        q = jax.nn.silu(q_ref[:, ls])
        k = 1.0 - f
        b = _cumsum_in_chunks(jnp.log(f), chunk)
        q_in = (q * jnp.exp(b)).astype(BF16)
        k_in = (k * jnp.exp(-b)).astype(BF16)
        v = v_ref[:, ls].astype(BF16)
        st = st_ref[hh]
        outs = []
        for c in range(n_chunks):
            sl = slice(c * chunk, (c + 1) * chunk)
            b_c = b[sl]
            b_last = b_c[chunk - 1:chunk]
            k_out = (k[sl] * jnp.exp(b_last - b_c)).astype(BF16)
            attn = lax.dot_general(q_in[sl], k_in[sl], _NT, preferred_element_type=F32)
            attn = jnp.where(mask, attn, 0.0).astype(BF16)
            o_intra = jnp.dot(attn, v[sl], preferred_element_type=F32)
            o_inter = lax.dot_general(q_in[sl], st.astype(BF16), _NT, preferred_element_type=F32)
            kv_t = lax.dot_general(v[sl], k_out, _TN, preferred_element_type=F32)
            st = jnp.exp(b_last) * st + kv_t
            outs.append(o_intra + o_inter)
        st_ref[hh] = st
        o = outs[0] if n_chunks == 1 else jnp.concatenate(outs, axis=0)
        o = o * lax.rsqrt(jnp.mean(o * o, axis=-1, keepdims=True) + EPS)
        o = o * gn_ref[:, ls] * jax.nn.silu(g_ref[:, ls])
        o_ref[:, ls] = o.astype(o_ref.dtype)
        finals.append(st)

    @pl.when(last)
    def _():
        for hh in range(heads):
            sout_ref[0, hh] = finals[hh].T


def _ret_kernel(*refs, chunk, n_chunks, has_state, heads, dv):
    if has_state:
        q_ref, k_ref, v_ref, g_ref, cos_ref, sin_ref, lg_ref, gn_ref, s0_ref, o_ref, sout_ref, st_ref = refs
    else:
        (q_ref, k_ref, v_ref, g_ref, cos_ref, sin_ref, lg_ref, gn_ref, o_ref, sout_ref, st_ref), s0_ref = refs, None
    t = pl.program_id(2)
    last = t == pl.num_programs(2) - 1
    cosf, sinf = cos_ref[...], sin_ref[...]
    half = HEAD_LANES // 2
    mask = _causal_mask(chunk)
    pos1 = (_chunk_pos(cosf.shape, chunk) + 1).astype(F32)

    @pl.when(t == 0)
    def _():
        for hh in range(heads):
            if has_state:
                st_ref[hh] = s0_ref[0, hh]
            else:
                st_ref[hh] = jnp.zeros(st_ref.shape[1:], F32)

    finals = []
    for hh in range(heads):
        ls = slice(hh * HEAD_LANES, (hh + 1) * HEAD_LANES)
        ws = slice(hh * dv, (hh + 1) * dv)
        q, k = q_ref[:, ls], k_ref[:, ls]
        qr = q * cosf + pltpu.roll(q, half, 1) * sinf
        kr = (k * cosf + pltpu.roll(k, half, 1) * sinf) * (HEAD_LANES ** -0.5)
        lg_wide = lg_ref[hh]
        lg = lg_wide[:, :HEAD_LANES]
        b = pos1 * lg
        b_last = float(chunk) * lg
        q_in = (qr * jnp.exp(b)).astype(BF16)
        k_in = (kr * jnp.exp(-b)).astype(BF16)
        k_out = (kr * jnp.exp(b_last - b)).astype(BF16)
        decay = jnp.exp(float(chunk) * lg_wide)
        v = v_ref[:, ws].astype(BF16)
        st = st_ref[hh]
        outs = []
        for c in range(n_chunks):
            sl = slice(c * chunk, (c + 1) * chunk)
            attn = lax.dot_general(q_in[sl], k_in[sl], _NT, preferred_element_type=F32)
            attn = jnp.where(mask, attn, 0.0).astype(BF16)
            o_intra = jnp.dot(attn, v[sl], preferred_element_type=F32)
            o_inter = jnp.dot(q_in[sl], st.astype(BF16), preferred_element_type=F32)
            kv = lax.dot_general(k_out[sl], v[sl], _TN, preferred_element_type=F32)
            st = decay * st + kv
            outs.append(o_intra + o_inter)
        st_ref[hh] = st
        o = outs[0] if n_chunks == 1 else jnp.concatenate(outs, axis=0)
        c0 = o - jnp.mean(o, axis=-1, keepdims=True)
        o = c0 * lax.rsqrt(jnp.mean(c0 * c0, axis=-1, keepdims=True) + EPS)
        o = o * gn_ref[:, ws] * jax.nn.silu(g_ref[:, ws])
        o_ref[:, ws] = o.astype(o_ref.dtype)
        finals.append(st)

    @pl.when(last)
    def _():
        for hh in range(heads):
            sout_ref[0, hh] = finals[hh]


def _rotary_tables(pos):
    half = HEAD_LANES // 2
    inv = 1.0 / (ROPE_BASE ** jnp.linspace(0.0, 1.0, half, dtype=F32))
    ang = pos.astype(F32)[:, None] * inv[None, :]
    cos, sin = jnp.cos(ang), jnp.sin(ang)
    return jnp.concatenate([cos, cos], axis=-1), jnp.concatenate([-sin, sin], axis=-1)


def _even_mixer(proj, batch, seq, pos, layer, s_a, s_b, a_shape, b_shape, lb_logits, g_a, g_b,
                rows_per_step=512, heads_per_step=4):
    m = proj.shape[0]
    a_heads, dk, dv_a = a_shape
    b_heads, dk_b, dv_b = b_shape
    assert dk == HEAD_LANES and dv_a == HEAD_LANES and dk_b == HEAD_LANES
    chunk = min(CHUNK, seq)
    rows = min(rows_per_step, seq)
    assert seq % rows == 0 and rows % chunk == 0
    nt = seq // rows
    has_state = s_a is not None
    hp_a = a_heads if nt == 1 else heads_per_step
    hp_b = b_heads if nt == 1 else heads_per_step
    assert a_heads % hp_a == 0 and b_heads % hp_b == 0
    a_w, a_v = a_heads * dk, a_heads * dv_a
    b_qk, b_v = b_heads * HEAD_LANES, b_heads * dv_b
    off_b = 2 * a_w + 2 * a_v
    grid_sem = ("parallel", "parallel", "arbitrary")

    def cols(offset, width, hp):
        blk = hp * width
        assert offset % blk == 0
        return pl.BlockSpec((rows, blk), lambda b, h, t: (b * nt + t, offset // blk + h))

    n_layers = lb_logits.shape[0]
    wa = hp_a * HEAD_LANES
    in_specs = [cols(0, dk, hp_a), cols(a_w, dk, hp_a), cols(2 * a_w, dv_a, hp_a), cols(2 * a_w + a_v, dv_a, hp_a),
                pl.BlockSpec((n_layers, wa), lambda b, h, t: (0, h)),
                pl.BlockSpec((1, wa), lambda b, h, t: (0, h))]
    args = [proj, proj, proj, proj, lb_logits, g_a.reshape(1, a_v)]
    if has_state:
        in_specs.append(pl.BlockSpec((1, hp_a, dk, dv_a), lambda b, h, t: (b, h, 0, 0)))
        args.append(s_a)
    o_a, new_a = pl.pallas_call(
        functools.partial(_hgrn_kernel, chunk=chunk, n_chunks=rows // chunk, layer=layer, has_state=has_state,
                          heads=hp_a),
        out_shape=(jax.ShapeDtypeStruct((m, a_v), BF16),
                   jax.ShapeDtypeStruct((batch, a_heads, dk, dv_a), F32)),
        grid=(batch, a_heads // hp_a, nt),
        in_specs=in_specs,
        out_specs=(pl.BlockSpec((rows, wa), lambda b, h, t: (b * nt + t, h)),
                   pl.BlockSpec((1, hp_a, dk, dv_a), lambda b, h, t: (b, h, 0, 0))),
        scratch_shapes=[pltpu.VMEM((hp_a, dv_a, dk), F32)],
        compiler_params=_cparams(grid_sem, 32 << 20),
        name="hgrn2",
    )(*args)

    cosf, sinf = _rotary_tables(pos)
    log_gamma = jnp.log(1.0 - jnp.exp2(-5.0 - jnp.arange(b_heads, dtype=F32)))
    lg = jnp.broadcast_to(log_gamma[:, None, None], (b_heads, 1, dv_b))
    wv = hp_b * dv_b
    in_specs = [cols(off_b, HEAD_LANES, hp_b), cols(off_b + b_qk, HEAD_LANES, hp_b),
                cols(off_b + 2 * b_qk, dv_b, hp_b), cols(off_b + 2 * b_qk + b_v, dv_b, hp_b),
                pl.BlockSpec((rows, HEAD_LANES), lambda b, h, t: (t, 0)),
                pl.BlockSpec((rows, HEAD_LANES), lambda b, h, t: (t, 0)),
                pl.BlockSpec((hp_b, 1, dv_b), lambda b, h, t: (h, 0, 0)),
                pl.BlockSpec((1, wv), lambda b, h, t: (0, h))]
    args = [proj, proj, proj, proj, cosf, sinf, lg, g_b.reshape(1, b_v)]
    if has_state:
        in_specs.append(pl.BlockSpec((1, hp_b, HEAD_LANES, dv_b), lambda b, h, t: (b, h, 0, 0)))
        args.append(s_b)
    o_b, new_b = pl.pallas_call(
        functools.partial(_ret_kernel, chunk=chunk, n_chunks=rows // chunk, has_state=has_state, heads=hp_b,
                          dv=dv_b),
        out_shape=(jax.ShapeDtypeStruct((m, b_v), BF16),
                   jax.ShapeDtypeStruct((batch, b_heads, HEAD_LANES, dv_b), F32)),
        grid=(batch, b_heads // hp_b, nt),
        in_specs=in_specs,
        out_specs=(pl.BlockSpec((rows, wv), lambda b, h, t: (b * nt + t, h)),
                   pl.BlockSpec((1, hp_b, HEAD_LANES, dv_b), lambda b, h, t: (b, h, 0, 0))),
        scratch_shapes=[pltpu.VMEM((hp_b, HEAD_LANES, dv_b), F32)],
        compiler_params=_cparams(grid_sem, 32 << 20),
        name="retention",
    )(*args)
    return o_a, o_b, new_a, new_b


SUBLANES = 8


def _griffin_kernel(*refs, rows, first_pos_zero, has_state):
    if has_state:
        (xb_ref, yb_ref, cw_ref, cb_ref, wa_ref, wx_ref, gab_ref, gxb_ref, lam_ref, sconv_ref, h0_ref,
         o_ref, hlast_ref, nconv_ref, xpad_ref, h_ref, a_ref, b_ref) = refs
    else:
        (xb_ref, yb_ref, cw_ref, cb_ref, wa_ref, wx_ref, gab_ref, gxb_ref, lam_ref,
         o_ref, hlast_ref, nconv_ref, xpad_ref, h_ref, a_ref, b_ref) = refs
        sconv_ref = h0_ref = None
    t = pl.program_id(2)
    width = cw_ref.shape[0]
    lead = SUBLANES - (width - 1)

    @pl.when(t == 0)
    def _():
        if has_state:
            xpad_ref[lead:SUBLANES, :] = sconv_ref[0]
            h_ref[...] = h0_ref[0]
        else:
            xpad_ref[0:SUBLANES, :] = jnp.zeros((SUBLANES, xpad_ref.shape[1]), F32)
            h_ref[...] = jnp.zeros_like(h_ref)

    xpad_ref[SUBLANES:SUBLANES + rows, :] = xb_ref[...]
    cw = cw_ref[...]
    acc = xpad_ref[lead:lead + rows, :] * cw[0:1]
    for j in range(1, width):
        acc = acc + xpad_ref[lead + j:lead + j + rows, :] * cw[j:j + 1]
    conv = cb_ref[...] + acc
    tail = xpad_ref[rows + lead:rows + SUBLANES, :]
    xpad_ref[lead:SUBLANES, :] = tail

    conv_bf = conv.astype(BF16)
    r = jax.nn.sigmoid(jnp.dot(conv_bf, wa_ref[0], preferred_element_type=F32) + gab_ref[...])
    i = jax.nn.sigmoid(jnp.dot(conv_bf, wx_ref[0], preferred_element_type=F32) + gxb_ref[...])
    log_a = -RG_C * r * jax.nn.softplus(-lam_ref[...])
    a = jnp.exp(log_a)
    mult = jnp.sqrt(1.0 - a * a)
    if first_pos_zero:
        row = lax.broadcasted_iota(jnp.int32, a.shape, 0)
        mult = jnp.where(row + t * rows == 0, 1.0, mult)
    bb = mult * (i * conv)

    grouped = (rows // SUBLANES, SUBLANES, a.shape[1])
    a3, b3 = a.reshape(grouped), bb.reshape(grouped)
    sub = lax.broadcasted_iota(jnp.int32, grouped, 1)
    shift = 1
    while shift < SUBLANES:
        take = sub >= shift
        b3 = jnp.where(take, a3 * pltpu.roll(b3, shift, 1) + b3, b3)
        a3 = jnp.where(take, a3 * pltpu.roll(a3, shift, 1), a3)
        shift *= 2
    a_ref[...] = a3.reshape(a.shape)
    b_ref[...] = b3.reshape(a.shape)

    def group(gidx, carry):
        r0 = pl.multiple_of(gidx * SUBLANES, SUBLANES)
        hg = a_ref[pl.ds(r0, SUBLANES), :] * carry + b_ref[pl.ds(r0, SUBLANES), :]
        b_ref[pl.ds(r0, SUBLANES), :] = hg
        return hg[SUBLANES - 1:SUBLANES, :]

    h_ref[...] = lax.fori_loop(0, rows // SUBLANES, group, h_ref[...], unroll=True)
    hs = b_ref[...]
    o_ref[...] = (jax.nn.gelu(yb_ref[...], approximate=True) * hs).astype(o_ref.dtype)

    @pl.when(t == pl.num_programs(2) - 1)
    def _():
        hlast_ref[0] = h_ref[...]
        nconv_ref[0] = tail


def _pair_blockdiag(w, pair):
    n, c, _ = w.shape
    w = w.reshape(n // pair, pair, c, c)
    rows = []
    for p in range(pair):
        blocks = [w[:, p] if q == p else jnp.zeros_like(w[:, p]) for q in range(pair)]
        rows.append(jnp.concatenate(blocks, axis=-1))
    return jnp.concatenate(rows, axis=-2)


def _odd_mixer(xy, batch, seq, pos0, h0, s_conv, conv_w, conv_b, wa, wx, ga_b, gx_b, lam, rows_per_step=256):
    m = xy.shape[0]
    d_rnn = conv_w.shape[1]
    width = conv_w.shape[0]
    blk = wa.shape[1]
    nblk = d_rnn // blk
    rows = min(rows_per_step, seq)
    assert seq % rows == 0 and rows % SUBLANES == 0 and rows >= width - 1
    nt = seq // rows
    has_state = h0 is not None

    def vec(a):
        return a.reshape(1, d_rnn)

    def vspec():
        return pl.BlockSpec((1, blk), lambda b, j, t: (0, j))

    in_specs = [pl.BlockSpec((rows, blk), lambda b, j, t: (b * nt + t, j)),
                pl.BlockSpec((rows, blk), lambda b, j, t: (b * nt + t, nblk + j)),
                pl.BlockSpec((width, blk), lambda b, j, t: (0, j)),
                vspec(),
                pl.BlockSpec((1, blk, blk), lambda b, j, t: (j, 0, 0)),
                pl.BlockSpec((1, blk, blk), lambda b, j, t: (j, 0, 0)),
                vspec(), vspec(), vspec()]
    args = [xy, xy, conv_w, vec(conv_b), wa, wx, vec(ga_b), vec(gx_b), vec(lam)]
    if has_state:
        in_specs += [pl.BlockSpec((1, width - 1, blk), lambda b, j, t: (b, 0, j)),
                     pl.BlockSpec((1, 1, blk), lambda b, j, t: (b, 0, j))]
        args += [s_conv, h0.reshape(batch, 1, d_rnn)]
    o, h_last, new_conv = pl.pallas_call(
        functools.partial(_griffin_kernel, rows=rows, first_pos_zero=(pos0 == 0), has_state=has_state),
        out_shape=(jax.ShapeDtypeStruct((m, d_rnn), BF16),
                   jax.ShapeDtypeStruct((batch, 1, d_rnn), F32),
                   jax.ShapeDtypeStruct((batch, width - 1, d_rnn), F32)),
        grid=(batch, nblk, nt),
        in_specs=in_specs,
        out_specs=(pl.BlockSpec((rows, blk), lambda b, j, t: (b * nt + t, j)),
                   pl.BlockSpec((1, 1, blk), lambda b, j, t: (b, 0, j)),
                   pl.BlockSpec((1, width - 1, blk), lambda b, j, t: (b, 0, j))),
        scratch_shapes=[pltpu.VMEM((rows + SUBLANES, blk), F32),
                        pltpu.VMEM((1, blk), F32),
                        pltpu.VMEM((rows, blk), F32),
                        pltpu.VMEM((rows, blk), F32)],
        compiler_params=_cparams(("parallel", "parallel", "arbitrary"), 32 << 20),
        name="griffin",
    )(*args)
    return o, h_last.reshape(batch, d_rnn), new_conv


def _trunk(x, pos0, states, p):
    batch, seq, d = x.shape
    depth = p["norm_mix"].shape[0]
    pos = pos0 + jnp.arange(seq, dtype=jnp.int32)
    h = x.reshape(batch * seq, d)
    many_rows = h.shape[0] >= 8 * 1024

    def project(xin, w, layer, **kw):
        if many_rows:
            return _matmul_stationary(xin, w, layer, tn=(512 if w.shape[1] > 4096 else 1024), **kw)
        return _matmul(xin, w, layer, tn=512, tk=w.shape[1], **kw)

    n_hgrn, n_ret, n_rg, n_conv = [], [], [], []
    hn = _rms_norm(h, p["norm_mix"][0], BF16)
    for l in range(depth):
        if l % 2 == 0:
            e = l // 2
            proj = project(hn, p["w_in_even"], e)
            s_a, s_b = (None, None) if states is None else (states[0][e], states[1][e])
            o_a, o_b, sa, sb = _even_mixer(proj, batch, seq, pos, l, s_a, s_b, p["hgrn_shape"], p["ret_shape"],
                                           p["hgrn_lb_logits"], p["hgrn_norm"][e], p["ret_norm"][e])
            n_hgrn.append(sa)
            n_ret.append(sb)
            h = project((o_a, o_b), p["w_out_even"], e, mode="residual", res=h)
        else:
            od = l // 2
            xy = project(hn, p["w_in_odd"], od)
            h0, sc = (None, None) if states is None else (states[2][od], states[3][od])
            o, sh, scn = _odd_mixer(xy, batch, seq, pos0, h0, sc, p["conv_w"][od], p["conv_b"][od],
                                    p["gate_a_w"][od], p["gate_x_w"][od], p["gate_a_b"][od], p["gate_x_b"][od],
                                    p["rglru_lambda"][od])
            n_rg.append(sh)
            n_conv.append(scn)
            h = project(o, p["w_out_odd"], od, mode="residual", res=h)
        hm = _rms_norm(h, p["norm_mlp"][l], BF16)
        up = project(hm, p["w_up"], l, mode="relu2", out_dtype=BF16)
        h = _matmul(up, p["w_down_bf16"], l, mode="residual", res=h, tk=2048)
        if l + 1 < depth:
            hn = _rms_norm(h, p["norm_mix"][l + 1], BF16)
    out = _rms_norm(h, p["norm_final"], x.dtype).reshape(batch, seq, d)
    return out, jnp.stack(n_hgrn), jnp.stack(n_ret), jnp.stack(n_rg), jnp.stack(n_conv)


def kernel(x_prompt, x_sample, state_hgrn, state_ret, state_rglru, state_conv, norm_mix, norm_mlp, norm_final,
           w_in_even, hgrn_lb_logits, hgrn_norm, ret_norm, w_out_even, w_in_odd, conv_w, conv_b, gate_a_w,
           gate_a_b, gate_x_w, gate_x_b, rglru_lambda, w_out_odd, w_up, w_down):
    past_len = 4096

    p = dict(
        norm_mix=norm_mix, norm_mlp=norm_mlp, norm_final=norm_final,
        w_in_even=w_in_even, w_out_even=w_out_even, w_in_odd=w_in_odd, w_out_odd=w_out_odd, w_up=w_up,
        w_down_bf16=w_down.astype(BF16),
        hgrn_lb_logits=hgrn_lb_logits, hgrn_norm=hgrn_norm, ret_norm=ret_norm,
        conv_w=conv_w, conv_b=conv_b,
        gate_a_w=jnp.stack([_pair_blockdiag(w, RG_PAIR) for w in gate_a_w.astype(BF16)]),
        gate_x_w=jnp.stack([_pair_blockdiag(w, RG_PAIR) for w in gate_x_w.astype(BF16)]),
        gate_a_b=gate_a_b, gate_x_b=gate_x_b, rglru_lambda=rglru_lambda,
        hgrn_shape=state_hgrn.shape[2:], ret_shape=state_ret.shape[2:],
    )
    y_p, hg_p, rt_p, rg_p, cv_p = _trunk(x_prompt, 0, None, p)
    y_s, hg_s, rt_s, rg_s, cv_s = _trunk(x_sample, past_len, (state_hgrn, state_ret, state_rglru, state_conv), p)
    return (y_p, y_s, hg_p, rt_p, rg_p, cv_p, hg_s, rt_s, rg_s, cv_s)
```

```python
import functools

import jax
import jax.numpy as jnp
from jax import lax
from jax.experimental import pallas as pl
from jax.experimental.pallas import tpu as pltpu

EPS = 1e-6
CHUNK = 64
ROPE_BASE = 10000.0
RG_C = 8.0
HEAD_LANES = 128
RG_PAIR = 2

V7X_VMEM_BYTES = 64 * 1024 * 1024
VMEM_CAP_BYTES = V7X_VMEM_BYTES - 6 * 1024 * 1024

F32 = jnp.float32
BF16 = jnp.bfloat16


def _cparams(semantics, vmem_bytes):
    return pltpu.CompilerParams(dimension_semantics=semantics,
                                vmem_limit_bytes=int(min(max(vmem_bytes, 32 * 1024 * 1024), VMEM_CAP_BYTES)))


def _rms_kernel(x_ref, g_ref, o_ref):
    x = x_ref[...]
    y = x * lax.rsqrt(jnp.mean(x * x, axis=-1, keepdims=True) + EPS)
    o_ref[...] = (y * g_ref[...]).astype(o_ref.dtype)


def _rms_norm(x, g, out_dtype, tm=256):
    m, d = x.shape
    tm = min(tm, m)
    io_bytes = 2 * tm * d * (4 + jnp.dtype(out_dtype).itemsize)
    return pl.pallas_call(
        _rms_kernel,
        out_shape=jax.ShapeDtypeStruct((m, d), out_dtype),
        grid=(m // tm,),
        in_specs=[pl.BlockSpec((tm, d), lambda i: (i, 0)),
                  pl.BlockSpec((1, d), lambda i: (0, 0))],
        out_specs=pl.BlockSpec((tm, d), lambda i: (i, 0)),
        compiler_params=_cparams(("parallel",), io_bytes + 3 * tm * d * 4),
        name="rms_norm",
    )(x, g.reshape(1, d))


def _epilogue(acc, mode, res_ref, o_ref):
    if mode == "relu2":
        acc = jnp.square(jnp.maximum(acc, 0.0))
    elif mode == "residual":
        acc = res_ref[...] + acc
    o_ref[...] = acc.astype(o_ref.dtype)


def _split_dot(x_refs, k_sizes, w_rows):
    acc, off = None, 0
    for x_ref, ks in zip(x_refs, k_sizes):
        part = jnp.dot(x_ref[...], w_rows(off, ks), preferred_element_type=F32)
        acc = part if acc is None else acc + part
        off += ks
    return acc


def _mm_kernel_single(*refs, mode, k_sizes):
    n_x = len(k_sizes)
    x_refs, w_ref = refs[:n_x], refs[n_x]
    if mode == "residual":
        res_ref, o_ref = refs[n_x + 1:]
    else:
        (o_ref,), res_ref = refs[n_x + 1:], None
    acc = _split_dot(x_refs, k_sizes, lambda off, ks: w_ref[off:off + ks, :].astype(BF16))
    _epilogue(acc, mode, res_ref, o_ref)


def _mm_kernel_multi(*refs, mode):
    if mode == "residual":
        x_ref, w_ref, res_ref, o_ref = refs
    else:
        x_ref, w_ref, o_ref, acc_ref = refs
    k = pl.program_id(2)
    part = jnp.dot(x_ref[...], w_ref[...].astype(BF16), preferred_element_type=F32)

    if mode == "residual":
        @pl.when(k == 0)
        def _():
            o_ref[...] = res_ref[...] + part

        @pl.when(k > 0)
        def _():
            o_ref[...] += part
    else:
        @pl.when(k == 0)
        def _():
            acc_ref[...] = part

        @pl.when(k > 0)
        def _():
            acc_ref[...] += part

        @pl.when(k == pl.num_programs(2) - 1)
        def _():
            _epilogue(acc_ref[...], mode, None, o_ref)


def _weight_spec(layer, block, index_map):
    return pl.BlockSpec((None,) + block, lambda *g: (layer,) + tuple(index_map(*g)))


def _matmul(x, w, layer, *, mode="none", res=None, out_dtype=F32, tm=1024, tn=1024, tk=4096):
    xs = x if isinstance(x, (tuple, list)) else (x,)
    m = xs[0].shape[0]
    k_sizes = tuple(xi.shape[1] for xi in xs)
    _, kdim, n = w.shape
    assert sum(k_sizes) == kdim
    tm, tn, tk = min(tm, m), min(tn, n), min(tk, kdim)
    nk = kdim // tk
    assert m % tm == 0 and n % tn == 0 and kdim % tk == 0
    out_bytes = jnp.dtype(out_dtype).itemsize
    w_bytes = jnp.dtype(w.dtype).itemsize
    vmem = 2 * (tm * tk * 2 + tk * tn * w_bytes + tm * tn * out_bytes) + tm * tn * 4 + (4 << 20)
    if w_bytes != 2:
        vmem += tk * tn * 2
    if nk == 1:
        grid = (m // tm, n // tn)
        in_specs = [pl.BlockSpec((tm, ks), lambda i, j: (i, 0)) for ks in k_sizes]
        in_specs.append(_weight_spec(layer, (tk, tn), lambda i, j: (0, j)))
        res_spec = pl.BlockSpec((tm, tn), lambda i, j: (i, j))
        out_spec = pl.BlockSpec((tm, tn), lambda i, j: (i, j))
        body = functools.partial(_mm_kernel_single, mode=mode, k_sizes=k_sizes)
        scratch, sem = [], ("parallel", "parallel")
    else:
        assert len(xs) == 1
        grid = (m // tm, n // tn, nk)
        in_specs = [pl.BlockSpec((tm, tk), lambda i, j, k: (i, k)),
                    _weight_spec(layer, (tk, tn), lambda i, j, k: (k, j))]
        res_spec = pl.BlockSpec((tm, tn), lambda i, j, k: (i, j))
        out_spec = pl.BlockSpec((tm, tn), lambda i, j, k: (i, j))
        body = functools.partial(_mm_kernel_multi, mode=mode)
        sem = ("parallel", "parallel", "arbitrary")
        if mode == "residual":
            assert out_dtype == F32
            scratch = []
            vmem += tm * tn * 4
        else:
            scratch = [pltpu.VMEM((tm, tn), F32)]
            vmem += tm * tn * 4
    args = list(xs) + [w]
    if mode == "residual":
        in_specs.append(res_spec)
        args.append(res)
        vmem += 2 * tm * tn * 4
    return pl.pallas_call(
        body,
        out_shape=jax.ShapeDtypeStruct((m, n), out_dtype),
        grid=grid,
        in_specs=in_specs,
        out_specs=out_spec,
        scratch_shapes=scratch,
        compiler_params=_cparams(sem, vmem),
        name="proj_" + mode,
    )(*args)


def _mm_kernel_stationary(*refs, mode, k_sizes, chunk_rows):
    n_x = len(k_sizes)
    x_refs, wchunk_ref = refs[:n_x], refs[n_x]
    if mode == "residual":
        res_ref, o_ref, wbf_ref = refs[n_x + 1:]
    else:
        (o_ref, wbf_ref), res_ref = refs[n_x + 1:], None
    j, i = pl.program_id(0), pl.program_id(1)

    @pl.when(j < pl.num_programs(0) - 1)
    def _():
        r0 = pl.multiple_of(i * chunk_rows, chunk_rows)
        wbf_ref[j % 2, pl.ds(r0, chunk_rows), :] = wchunk_ref[...].astype(BF16)

    @pl.when(j > 0)
    def _():
        w_tile = wbf_ref.at[(j + 1) % 2]
        acc = _split_dot(x_refs, k_sizes, lambda off, ks: w_tile[off:off + ks, :])
        _epilogue(acc, mode, res_ref, o_ref)


def _matmul_stationary(x, w, layer, *, mode="none", res=None, out_dtype=F32, tm=1024, tn=1024):
    xs = x if isinstance(x, (tuple, list)) else (x,)
    m = xs[0].shape[0]
    k_sizes = tuple(xi.shape[1] for xi in xs)
    _, kdim, n = w.shape
    assert sum(k_sizes) == kdim and m % tm == 0 and n % tn == 0
    n_i, n_j = m // tm, n // tn
    assert kdim % n_i == 0
    chunk_rows = kdim // n_i
    assert chunk_rows % 16 == 0
    out_bytes = jnp.dtype(out_dtype).itemsize

    def row_tile(j, i):
        return jnp.where(j == 0, 0, i)

    in_specs = [pl.BlockSpec((tm, ks), lambda j, i: (row_tile(j, i), 0)) for ks in k_sizes]
    in_specs.append(_weight_spec(layer, (chunk_rows, tn), lambda j, i: (i, jnp.minimum(j, n_j - 1))))
    io_spec = pl.BlockSpec((tm, tn), lambda j, i: (row_tile(j, i), jnp.maximum(j - 1, 0)))
    args = list(xs) + [w]
    vmem = (2 * (tm * kdim * 2 + chunk_rows * tn * 4 + tm * tn * out_bytes) + 2 * kdim * tn * 2
            + tm * tn * 4 + (4 << 20))
    if mode == "residual":
        in_specs.append(io_spec)
        args.append(res)
        vmem += 2 * tm * tn * 4
    return pl.pallas_call(
        functools.partial(_mm_kernel_stationary, mode=mode, k_sizes=k_sizes, chunk_rows=chunk_rows),
        out_shape=jax.ShapeDtypeStruct((m, n), out_dtype),
        grid=(n_j + 1, n_i),
        in_specs=in_specs,
        out_specs=io_spec,
        scratch_shapes=[pltpu.VMEM((2, kdim, tn), BF16)],
        compiler_params=_cparams(("arbitrary", "arbitrary"), vmem),
        name="projws_" + mode,
    )(*args)


def _chunk_pos(shape, chunk):
    return lax.broadcasted_iota(jnp.int32, shape, 0) % chunk


def _cumsum_in_chunks(x, chunk):
    pos = _chunk_pos(x.shape, chunk)
    shift = 1
    while shift < chunk:
        x = x + jnp.where(pos >= shift, pltpu.roll(x, shift, 0), 0.0)
        shift *= 2
    return x


def _causal_mask(chunk):
    r = lax.broadcasted_iota(jnp.int32, (chunk, chunk), 0)
    c = lax.broadcasted_iota(jnp.int32, (chunk, chunk), 1)
    return r >= c


_NT = (((1,), (1,)), ((), ()))
_TN = (((0,), (0,)), ((), ()))


def _lower_bound(logits, layer):
    rows = [logits[i:i + 1] for i in range(logits.shape[0])]
    mx = functools.reduce(jnp.maximum, rows)
    es = [jnp.exp(r - mx) for r in rows]
    den = functools.reduce(lambda a, b: a + b, es)
    lb = es[0] / den
    for i in range(1, layer + 1):
        lb = lb + es[i] / den
    return lb


def _hgrn_kernel(*refs, chunk, n_chunks, layer, has_state, heads):
    if has_state:
        q_ref, f_ref, v_ref, g_ref, lbl_ref, gn_ref, s0_ref, o_ref, sout_ref, st_ref = refs
    else:
        (q_ref, f_ref, v_ref, g_ref, lbl_ref, gn_ref, o_ref, sout_ref, st_ref), s0_ref = refs, None
    t = pl.program_id(2)
    last = t == pl.num_programs(2) - 1
    mask = _causal_mask(chunk)

    @pl.when(t == 0)
    def _():
        for hh in range(heads):
            if has_state:
                st_ref[hh] = s0_ref[0, hh].T
            else:
                st_ref[hh] = jnp.zeros(st_ref.shape[1:], F32)

    finals = []
    for hh in range(heads):
        ls = slice(hh * HEAD_LANES, (hh + 1) * HEAD_LANES)
        lb = _lower_bound(lbl_ref[:, ls], layer)
        f = lb + (1.0 - lb) * jax.nn.sigmoid(f_ref[:, ls])
        q = jax.nn.silu(q_ref[:, ls])
        k = 1.0 - f
        b = _cumsum_in_chunks(jnp.log(f), chunk)
        q_in = (q * jnp.exp(b)).astype(BF16)
        k_in = (k * jnp.exp(-b)).astype(BF16)
        v = v_ref[:, ls].astype(BF16)
        st = st_ref[hh]
        outs = []
        for c in range(n_chunks):
            sl = slice(c * chunk, (c + 1) * chunk)
            b_c = b[sl]
            b_last = b_c[chunk - 1:chunk]
            k_out = (k[sl] * jnp.exp(b_last - b_c)).astype(BF16)
            attn = lax.dot_general(q_in[sl], k_in[sl], _NT, preferred_element_type=F32)
            attn = jnp.where(mask, attn, 0.0).astype(BF16)
            o_intra = jnp.dot(attn, v[sl], preferred_element_type=F32)
            o_inter = lax.dot_general(q_in[sl], st.astype(BF16), _NT, preferred_element_type=F32)
            kv_t = lax.dot_general(v[sl], k_out, _TN, preferred_element_type=F32)
            st = jnp.exp(b_last) * st + kv_t
            outs.append(o_intra + o_inter)
        st_ref[hh] = st
        o = outs[0] if n_chunks == 1 else jnp.concatenate(outs, axis=0)
        o = o * lax.rsqrt(jnp.mean(o * o, axis=-1, keepdims=True) + EPS)
        o = o * gn_ref[:, ls] * jax.nn.silu(g_ref[:, ls])
        o_ref[:, ls] = o.astype(o_ref.dtype)
        finals.append(st)

    @pl.when(last)
    def _():
        for hh in range(heads):
            sout_ref[0, hh] = finals[hh].T


def _ret_kernel(*refs, chunk, n_chunks, has_state, heads, dv):
    if has_state:
        q_ref, k_ref, v_ref, g_ref, cos_ref, sin_ref, lg_ref, gn_ref, s0_ref, o_ref, sout_ref, st_ref = refs
    else:
        (q_ref, k_ref, v_ref, g_ref, cos_ref, sin_ref, lg_ref, gn_ref, o_ref, sout_ref, st_ref), s0_ref = refs, None
    t = pl.program_id(2)
    last = t == pl.num_programs(2) - 1
    cosf, sinf = cos_ref[...], sin_ref[...]
    half = HEAD_LANES // 2
    mask = _causal_mask(chunk)
    pos1 = (_chunk_pos(cosf.shape, chunk) + 1).astype(F32)

    @pl.when(t == 0)
    def _():
        for hh in range(heads):
            if has_state:
                st_ref[hh] = s0_ref[0, hh]
            else:
                st_ref[hh] = jnp.zeros(st_ref.shape[1:], F32)

    finals = []
    for hh in range(heads):
        ls = slice(hh * HEAD_LANES, (hh + 1) * HEAD_LANES)
        ws = slice(hh * dv, (hh + 1) * dv)
        q, k = q_ref[:, ls], k_ref[:, ls]
        qr = q * cosf + pltpu.roll(q, half, 1) * sinf
        kr = (k * cosf + pltpu.roll(k, half, 1) * sinf) * (HEAD_LANES ** -0.5)
        lg_wide = lg_ref[hh]
        lg = lg_wide[:, :HEAD_LANES]
        b = pos1 * lg
        b_last = float(chunk) * lg
        q_in = (qr * jnp.exp(b)).astype(BF16)
        k_in = (kr * jnp.exp(-b)).astype(BF16)
        k_out = (kr * jnp.exp(b_last - b)).astype(BF16)
        decay = jnp.exp(float(chunk) * lg_wide)
        v = v_ref[:, ws].astype(BF16)
        st = st_ref[hh]
        outs = []
        for c in range(n_chunks):
            sl = slice(c * chunk, (c + 1) * chunk)
            attn = lax.dot_general(q_in[sl], k_in[sl], _NT, preferred_element_type=F32)
            attn = jnp.where(mask, attn, 0.0).astype(BF16)
            o_intra = jnp.dot(attn, v[sl], preferred_element_type=F32)
            o_inter = jnp.dot(q_in[sl], st.astype(BF16), preferred_element_type=F32)
            kv = lax.dot_general(k_out[sl], v[sl], _TN, preferred_element_type=F32)
            st = decay * st + kv
            outs.append(o_intra + o_inter)
        st_ref[hh] = st
        o = outs[0] if n_chunks == 1 else jnp.concatenate(outs, axis=0)
        c0 = o - jnp.mean(o, axis=-1, keepdims=True)
        o = c0 * lax.rsqrt(jnp.mean(c0 * c0, axis=-1, keepdims=True) + EPS)
        o = o * gn_ref[:, ws] * jax.nn.silu(g_ref[:, ws])
        o_ref[:, ws] = o.astype(o_ref.dtype)
        finals.append(st)

    @pl.when(last)
    def _():
        for hh in range(heads):
            sout_ref[0, hh] = finals[hh]


def _rotary_tables(pos):
    half = HEAD_LANES // 2
    inv = 1.0 / (ROPE_BASE ** jnp.linspace(0.0, 1.0, half, dtype=F32))
    ang = pos.astype(F32)[:, None] * inv[None, :]
    cos, sin = jnp.cos(ang), jnp.sin(ang)
    return jnp.concatenate([cos, cos], axis=-1), jnp.concatenate([-sin, sin], axis=-1)


def _even_mixer(proj, batch, seq, pos, layer, s_a, s_b, a_shape, b_shape, lb_logits, g_a, g_b,
                rows_per_step=1024, heads_per_step=4):
    m = proj.shape[0]
    a_heads, dk, dv_a = a_shape
    b_heads, dk_b, dv_b = b_shape
    assert dk == HEAD_LANES and dv_a == HEAD_LANES and dk_b == HEAD_LANES
    chunk = min(CHUNK, seq)
    rows = min(rows_per_step, seq)
    assert seq % rows == 0 and rows % chunk == 0
    nt = seq // rows
    has_state = s_a is not None
    hp_a = a_heads if nt == 1 else heads_per_step
    hp_b = b_heads if nt == 1 else heads_per_step
    assert a_heads % hp_a == 0 and b_heads % hp_b == 0
    a_w, a_v = a_heads * dk, a_heads * dv_a
    b_qk, b_v = b_heads * HEAD_LANES, b_heads * dv_b
    off_b = 2 * a_w + 2 * a_v
    grid_sem = ("parallel", "parallel", "arbitrary")

    def cols(offset, width, hp):
        blk = hp * width
        assert offset % blk == 0
        return pl.BlockSpec((rows, blk), lambda b, h, t: (b * nt + t, offset // blk + h))

    n_layers = lb_logits.shape[0]
    wa = hp_a * HEAD_LANES
    in_specs = [cols(0, dk, hp_a), cols(a_w, dk, hp_a), cols(2 * a_w, dv_a, hp_a), cols(2 * a_w + a_v, dv_a, hp_a),
                pl.BlockSpec((n_layers, wa), lambda b, h, t: (0, h)),
                pl.BlockSpec((1, wa), lambda b, h, t: (0, h))]
    args = [proj, proj, proj, proj, lb_logits, g_a.reshape(1, a_v)]
    if has_state:
        in_specs.append(pl.BlockSpec((1, hp_a, dk, dv_a), lambda b, h, t: (b, h, 0, 0)))
        args.append(s_a)
    o_a, new_a = pl.pallas_call(
        functools.partial(_hgrn_kernel, chunk=chunk, n_chunks=rows // chunk, layer=layer, has_state=has_state,
                          heads=hp_a),
        out_shape=(jax.ShapeDtypeStruct((m, a_v), BF16),
                   jax.ShapeDtypeStruct((batch, a_heads, dk, dv_a), F32)),
        grid=(batch, a_heads // hp_a, nt),
        in_specs=in_specs,
        out_specs=(pl.BlockSpec((rows, wa), lambda b, h, t: (b * nt + t, h)),
                   pl.BlockSpec((1, hp_a, dk, dv_a), lambda b, h, t: (b, h, 0, 0))),
        scratch_shapes=[pltpu.VMEM((hp_a, dv_a, dk), F32)],
        compiler_params=_cparams(grid_sem, 48 << 20),
        name="hgrn2",
    )(*args)

    cosf, sinf = _rotary_tables(pos)
    log_gamma = jnp.log(1.0 - jnp.exp2(-5.0 - jnp.arange(b_heads, dtype=F32)))
    lg = jnp.broadcast_to(log_gamma[:, None, None], (b_heads, 1, dv_b))
    wv = hp_b * dv_b
    in_specs = [cols(off_b, HEAD_LANES, hp_b), cols(off_b + b_qk, HEAD_LANES, hp_b),
                cols(off_b + 2 * b_qk, dv_b, hp_b), cols(off_b + 2 * b_qk + b_v, dv_b, hp_b),
                pl.BlockSpec((rows, HEAD_LANES), lambda b, h, t: (t, 0)),
                pl.BlockSpec((rows, HEAD_LANES), lambda b, h, t: (t, 0)),
                pl.BlockSpec((hp_b, 1, dv_b), lambda b, h, t: (h, 0, 0)),
                pl.BlockSpec((1, wv), lambda b, h, t: (0, h))]
    args = [proj, proj, proj, proj, cosf, sinf, lg, g_b.reshape(1, b_v)]
    if has_state:
        in_specs.append(pl.BlockSpec((1, hp_b, HEAD_LANES, dv_b), lambda b, h, t: (b, h, 0, 0)))
        args.append(s_b)
    o_b, new_b = pl.pallas_call(
        functools.partial(_ret_kernel, chunk=chunk, n_chunks=rows // chunk, has_state=has_state, heads=hp_b,
                          dv=dv_b),
        out_shape=(jax.ShapeDtypeStruct((m, b_v), BF16),
                   jax.ShapeDtypeStruct((batch, b_heads, HEAD_LANES, dv_b), F32)),
        grid=(batch, b_heads // hp_b, nt),
        in_specs=in_specs,
        out_specs=(pl.BlockSpec((rows, wv), lambda b, h, t: (b * nt + t, h)),
                   pl.BlockSpec((1, hp_b, HEAD_LANES, dv_b), lambda b, h, t: (b, h, 0, 0))),
        scratch_shapes=[pltpu.VMEM((hp_b, HEAD_LANES, dv_b), F32)],
        compiler_params=_cparams(grid_sem, 48 << 20),
        name="retention",
    )(*args)
    return o_a, o_b, new_a, new_b


SUBLANES = 8


def _griffin_kernel(*refs, rows, first_pos_zero, has_state):
    if has_state:
        (xb_ref, yb_ref, cw_ref, cb_ref, wa_ref, wx_ref, gab_ref, gxb_ref, lam_ref, sconv_ref, h0_ref,
         o_ref, hlast_ref, nconv_ref, xpad_ref, h_ref, a_ref, b_ref) = refs
    else:
        (xb_ref, yb_ref, cw_ref, cb_ref, wa_ref, wx_ref, gab_ref, gxb_ref, lam_ref,
         o_ref, hlast_ref, nconv_ref, xpad_ref, h_ref, a_ref, b_ref) = refs
        sconv_ref = h0_ref = None
    t = pl.program_id(2)
    width = cw_ref.shape[0]
    lead = SUBLANES - (width - 1)

    @pl.when(t == 0)
    def _():
        if has_state:
            xpad_ref[lead:SUBLANES, :] = sconv_ref[0]
            h_ref[...] = h0_ref[0]
        else:
            xpad_ref[0:SUBLANES, :] = jnp.zeros((SUBLANES, xpad_ref.shape[1]), F32)
            h_ref[...] = jnp.zeros_like(h_ref)

    xpad_ref[SUBLANES:SUBLANES + rows, :] = xb_ref[...]
    cw = cw_ref[...]
    acc = xpad_ref[lead:lead + rows, :] * cw[0:1]
    for j in range(1, width):
        acc = acc + xpad_ref[lead + j:lead + j + rows, :] * cw[j:j + 1]
    conv = cb_ref[...] + acc
    tail = xpad_ref[rows + lead:rows + SUBLANES, :]
    xpad_ref[lead:SUBLANES, :] = tail

    conv_bf = conv.astype(BF16)
    r = jax.nn.sigmoid(jnp.dot(conv_bf, wa_ref[0], preferred_element_type=F32) + gab_ref[...])
    i = jax.nn.sigmoid(jnp.dot(conv_bf, wx_ref[0], preferred_element_type=F32) + gxb_ref[...])
    log_a = -RG_C * r * jax.nn.softplus(-lam_ref[...])
    a = jnp.exp(log_a)
    mult = jnp.sqrt(1.0 - a * a)
    if first_pos_zero:
        row = lax.broadcasted_iota(jnp.int32, a.shape, 0)
        mult = jnp.where(row + t * rows == 0, 1.0, mult)
    bb = mult * (i * conv)

    grouped = (rows // SUBLANES, SUBLANES, a.shape[1])
    a3, b3 = a.reshape(grouped), bb.reshape(grouped)
    sub = lax.broadcasted_iota(jnp.int32, grouped, 1)
    shift = 1
    while shift < SUBLANES:
        take = sub >= shift
        b3 = jnp.where(take, a3 * pltpu.roll(b3, shift, 1) + b3, b3)
        a3 = jnp.where(take, a3 * pltpu.roll(a3, shift, 1), a3)
        shift *= 2
    a_ref[...] = a3.reshape(a.shape)
    b_ref[...] = b3.reshape(a.shape)

    def group(gidx, carry):
        r0 = pl.multiple_of(gidx * SUBLANES, SUBLANES)
        hg = a_ref[pl.ds(r0, SUBLANES), :] * carry + b_ref[pl.ds(r0, SUBLANES), :]
        b_ref[pl.ds(r0, SUBLANES), :] = hg
        return hg[SUBLANES - 1:SUBLANES, :]

    h_ref[...] = lax.fori_loop(0, rows // SUBLANES, group, h_ref[...], unroll=True)
    hs = b_ref[...]
    o_ref[...] = (jax.nn.gelu(yb_ref[...], approximate=True) * hs).astype(o_ref.dtype)

    @pl.when(t == pl.num_programs(2) - 1)
    def _():
        hlast_ref[0] = h_ref[...]
        nconv_ref[0] = tail


def _pair_blockdiag(w, pair):
    n, c, _ = w.shape
    w = w.reshape(n // pair, pair, c, c)
    rows = []
    for p in range(pair):
        blocks = [w[:, p] if q == p else jnp.zeros_like(w[:, p]) for q in range(pair)]
        rows.append(jnp.concatenate(blocks, axis=-1))
    return jnp.concatenate(rows, axis=-2)


def _odd_mixer(xy, batch, seq, pos0, h0, s_conv, conv_w, conv_b, wa, wx, ga_b, gx_b, lam, rows_per_step=512):
    m = xy.shape[0]
    d_rnn = conv_w.shape[1]
    width = conv_w.shape[0]
    blk = wa.shape[1]
    nblk = d_rnn // blk
    rows = min(rows_per_step, seq)
    assert seq % rows == 0 and rows % SUBLANES == 0 and rows >= width - 1
    nt = seq // rows
    has_state = h0 is not None

    def vec(a):
        return a.reshape(1, d_rnn)

    def vspec():
        return pl.BlockSpec((1, blk), lambda b, j, t: (0, j))

    in_specs = [pl.BlockSpec((rows, blk), lambda b, j, t: (b * nt + t, j)),
                pl.BlockSpec((rows, blk), lambda b, j, t: (b * nt + t, nblk + j)),
                pl.BlockSpec((width, blk), lambda b, j, t: (0, j)),
                vspec(),
                pl.BlockSpec((1, blk, blk), lambda b, j, t: (j, 0, 0)),
                pl.BlockSpec((1, blk, blk), lambda b, j, t: (j, 0, 0)),
                vspec(), vspec(), vspec()]
    args = [xy, xy, conv_w, vec(conv_b), wa, wx, vec(ga_b), vec(gx_b), vec(lam)]
    if has_state:
        in_specs += [pl.BlockSpec((1, width - 1, blk), lambda b, j, t: (b, 0, j)),
                     pl.BlockSpec((1, 1, blk), lambda b, j, t: (b, 0, j))]
        args += [s_conv, h0.reshape(batch, 1, d_rnn)]
    o, h_last, new_conv = pl.pallas_call(
        functools.partial(_griffin_kernel, rows=rows, first_pos_zero=(pos0 == 0), has_state=has_state),
        out_shape=(jax.ShapeDtypeStruct((m, d_rnn), BF16),
                   jax.ShapeDtypeStruct((batch, 1, d_rnn), F32),
                   jax.ShapeDtypeStruct((batch, width - 1, d_rnn), F32)),
        grid=(batch, nblk, nt),
        in_specs=in_specs,
        out_specs=(pl.BlockSpec((rows, blk), lambda b, j, t: (b * nt + t, j)),
                   pl.BlockSpec((1, 1, blk), lambda b, j, t: (b, 0, j)),
                   pl.BlockSpec((1, width - 1, blk), lambda b, j, t: (b, 0, j))),
        scratch_shapes=[pltpu.VMEM((rows + SUBLANES, blk), F32),
                        pltpu.VMEM((1, blk), F32),
                        pltpu.VMEM((rows, blk), F32),
                        pltpu.VMEM((rows, blk), F32)],
        compiler_params=_cparams(("parallel", "parallel", "arbitrary"), 32 << 20),
        name="griffin",
    )(*args)
    return o, h_last.reshape(batch, d_rnn), new_conv


def _trunk(x, pos0, states, p):
    batch, seq, d = x.shape
    depth = p["norm_mix"].shape[0]
    pos = pos0 + jnp.arange(seq, dtype=jnp.int32)
    h = x.reshape(batch * seq, d)
    many_rows = h.shape[0] >= 8 * 1024

    def project(xin, w, layer, **kw):
        if many_rows:
            return _matmul_stationary(xin, w, layer, tn=(512 if w.shape[1] > 4096 else 1024), **kw)
        return _matmul(xin, w, layer, tn=512, tk=w.shape[1], **kw)

    n_hgrn, n_ret, n_rg, n_conv = [], [], [], []
    hn = _rms_norm(h, p["norm_mix"][0], BF16)
    for l in range(depth):
        if l % 2 == 0:
            e = l // 2
            proj = project(hn, p["w_in_even"], e)
            s_a, s_b = (None, None) if states is None else (states[0][e], states[1][e])
            o_a, o_b, sa, sb = _even_mixer(proj, batch, seq, pos, l, s_a, s_b, p["hgrn_shape"], p["ret_shape"],
                                           p["hgrn_lb_logits"], p["hgrn_norm"][e], p["ret_norm"][e])
            n_hgrn.append(sa)
            n_ret.append(sb)
            h = project((o_a, o_b), p["w_out_even"], e, mode="residual", res=h)
        else:
            od = l // 2
            xy = project(hn, p["w_in_odd"], od)
            h0, sc = (None, None) if states is None else (states[2][od], states[3][od])
            o, sh, scn = _odd_mixer(xy, batch, seq, pos0, h0, sc, p["conv_w"][od], p["conv_b"][od],
                                    p["gate_a_w"][od], p["gate_x_w"][od], p["gate_a_b"][od], p["gate_x_b"][od],
                                    p["rglru_lambda"][od])
            n_rg.append(sh)
            n_conv.append(scn)
            h = project(o, p["w_out_odd"], od, mode="residual", res=h)
        hm = _rms_norm(h, p["norm_mlp"][l], BF16)
        up = project(hm, p["w_up"], l, mode="relu2", out_dtype=BF16)
        h = _matmul(up, p["w_down_bf16"], l, mode="residual", res=h, tk=4096)
        if l + 1 < depth:
            hn = _rms_norm(h, p["norm_mix"][l + 1], BF16)
    out = _rms_norm(h, p["norm_final"], x.dtype).reshape(batch, seq, d)
    return out, jnp.stack(n_hgrn), jnp.stack(n_ret), jnp.stack(n_rg), jnp.stack(n_conv)


def kernel(x_prompt, x_sample, state_hgrn, state_ret, state_rglru, state_conv, norm_mix, norm_mlp, norm_final,
           w_in_even, hgrn_lb_logits, hgrn_norm, ret_norm, w_out_even, w_in_odd, conv_w, conv_b, gate_a_w,
           gate_a_b, gate_x_w, gate_x_b, rglru_lambda, w_out_odd, w_up, w_down):
    past_len = 4096

    p = dict(
        norm_mix=norm_mix, norm_mlp=norm_mlp, norm_final=norm_final,
        w_in_even=w_in_even, w_out_even=w_out_even, w_in_odd=w_in_odd, w_out_odd=w_out_odd, w_up=w_up,
        w_down_bf16=w_down.astype(BF16),
        hgrn_lb_logits=hgrn_lb_logits, hgrn_norm=hgrn_norm, ret_norm=ret_norm,
        conv_w=conv_w, conv_b=conv_b,
        gate_a_w=jnp.stack([_pair_blockdiag(w, RG_PAIR) for w in gate_a_w.astype(BF16)]),
        gate_x_w=jnp.stack([_pair_blockdiag(w, RG_PAIR) for w in gate_x_w.astype(BF16)]),
        gate_a_b=gate_a_b, gate_x_b=gate_x_b, rglru_lambda=rglru_lambda,
        hgrn_shape=state_hgrn.shape[2:], ret_shape=state_ret.shape[2:],
    )
    y_p, hg_p, rt_p, rg_p, cv_p = _trunk(x_prompt, 0, None, p)
    y_s, hg_s, rt_s, rg_s, cv_s = _trunk(x_sample, past_len, (state_hgrn, state_ret, state_rglru, state_conv), p)
    return (y_p, y_s, hg_p, rt_p, rg_p, cv_p, hg_s, rt_s, rg_s, cv_s)
```

```python
import functools

import jax
import jax.numpy as jnp
from jax import lax
from jax.experimental import pallas as pl
from jax.experimental.pallas import tpu as pltpu

EPS = 1e-6
CHUNK = 64
ROPE_BASE = 10000.0
RG_C = 8.0
HEAD_LANES = 128
RG_PAIR = 2

V7X_VMEM_BYTES = 64 * 1024 * 1024
VMEM_CAP_BYTES = V7X_VMEM_BYTES - 6 * 1024 * 1024

F32 = jnp.float32
BF16 = jnp.bfloat16


def _cparams(semantics, vmem_bytes):
    return pltpu.CompilerParams(dimension_semantics=semantics,
                                vmem_limit_bytes=int(min(max(vmem_bytes, 32 * 1024 * 1024), VMEM_CAP_BYTES)))


def _rms_kernel(x_ref, g_ref, o_ref):
    x = x_ref[...]
    y = x * lax.rsqrt(jnp.mean(x * x, axis=-1, keepdims=True) + EPS)
    o_ref[...] = (y * g_ref[...]).astype(o_ref.dtype)


def _rms_norm(x, g, out_dtype, tm=512):
    m, d = x.shape
    tm = min(tm, m)
    io_bytes = 2 * tm * d * (4 + jnp.dtype(out_dtype).itemsize)
    return pl.pallas_call(
        _rms_kernel,
        out_shape=jax.ShapeDtypeStruct((m, d), out_dtype),
        grid=(m // tm,),
        in_specs=[pl.BlockSpec((tm, d), lambda i: (i, 0)),
                  pl.BlockSpec((1, d), lambda i: (0, 0))],
        out_specs=pl.BlockSpec((tm, d), lambda i: (i, 0)),
        compiler_params=_cparams(("parallel",), io_bytes + 3 * tm * d * 4),
        name="rms_norm",
    )(x, g.reshape(1, d))


def _unpack(refs, n_x, scaled, mode, norm_out):
    it = iter(refs)
    x_refs = [next(it) for _ in range(n_x)]
    w_ref = next(it)
    ssq_ref = next(it) if scaled else None
    res_ref = next(it) if mode == "residual" else None
    gain_ref = next(it) if norm_out else None
    o_ref = next(it)
    side = (gain_ref, next(it), next(it)) if norm_out else None
    return x_refs, w_ref, ssq_ref, res_ref, o_ref, side, list(it)


def _row_scale(acc, ssq_ref, norm_dim):
    if ssq_ref is None:
        return acc
    return acc * lax.rsqrt(jnp.sum(ssq_ref[...], axis=-1, keepdims=True) * (1.0 / norm_dim) + EPS)


def _epilogue(acc, mode, res_ref, o_ref, side):
    if mode == "relu2":
        acc = jnp.square(jnp.maximum(acc, 0.0))
    elif mode == "residual":
        acc = res_ref[...] + acc
    o_ref[...] = acc.astype(o_ref.dtype)
    if side is not None:
        gain_ref, hg_ref, ssq_out_ref = side
        hg_ref[...] = (acc * gain_ref[...]).astype(hg_ref.dtype)
        sq = acc * acc
        lanes = ssq_out_ref.shape[1]
        tot = sq[:, 0:lanes]
        for c in range(1, sq.shape[1] // lanes):
            tot = tot + sq[:, c * lanes:(c + 1) * lanes]
        ssq_out_ref[...] = tot


def _split_dot(x_refs, k_sizes, w_rows):
    acc, off = None, 0
    for x_ref, ks in zip(x_refs, k_sizes):
        part = jnp.dot(x_ref[...], w_rows(off, ks), preferred_element_type=F32)
        acc = part if acc is None else acc + part
        off += ks
    return acc


def _mm_kernel_single(*refs, mode, k_sizes, scaled, norm_out, norm_dim):
    x_refs, w_ref, ssq_ref, res_ref, o_ref, side, _ = _unpack(refs, len(k_sizes), scaled, mode, norm_out)
    acc = _split_dot(x_refs, k_sizes, lambda off, ks: w_ref[off:off + ks, :].astype(BF16))
    _epilogue(_row_scale(acc, ssq_ref, norm_dim), mode, res_ref, o_ref, side)


def _mm_kernel_multi(*refs, mode, norm_out):
    (x_ref,), w_ref, _, res_ref, o_ref, side, scratch = _unpack(refs, 1, False, mode, norm_out)
    k = pl.program_id(2)
    last = pl.num_programs(2) - 1
    part = jnp.dot(x_ref[...], w_ref[...].astype(BF16), preferred_element_type=F32)

    if mode == "residual":
        @pl.when(k == 0)
        def _():
            o_ref[...] = res_ref[...] + part

        @pl.when(jnp.logical_and(k > 0, k < last))
        def _():
            o_ref[...] += part

        @pl.when(jnp.logical_and(k > 0, k == last))
        def _():
            _epilogue(o_ref[...] + part, "none", None, o_ref, side)
    else:
        (acc_ref,) = scratch

        @pl.when(k == 0)
        def _():
            acc_ref[...] = part

        @pl.when(k > 0)
        def _():
            acc_ref[...] += part

        @pl.when(k == last)
        def _():
            _epilogue(acc_ref[...], mode, None, o_ref, side)


def _weight_spec(layer, block, index_map):
    return pl.BlockSpec((None,) + block, lambda *g: (layer,) + tuple(index_map(*g)))


NORM_LANES = 128


def _norm_side(m, n, tn, norm_gain, io_index):
    n_j = n // tn
    gain_spec = pl.BlockSpec((1, tn), lambda *g: (0, io_index(*g)[1]))
    out_shapes = (jax.ShapeDtypeStruct((m, n), BF16), jax.ShapeDtypeStruct((m, n_j * NORM_LANES), F32))
    return gain_spec, norm_gain.reshape(1, n), out_shapes


def _matmul(x, w, layer, *, mode="none", res=None, out_dtype=F32, tm=1024, tn=1024, tk=4096,
            row_ssq=None, norm_gain=None):
    xs = x if isinstance(x, (tuple, list)) else (x,)
    m = xs[0].shape[0]
    k_sizes = tuple(xi.shape[1] for xi in xs)
    _, kdim, n = w.shape
    assert sum(k_sizes) == kdim
    tm, tn, tk = min(tm, m), min(tn, n), min(tk, kdim)
    nk = kdim // tk
    assert m % tm == 0 and n % tn == 0 and kdim % tk == 0
    scaled, norm_out = row_ssq is not None, norm_gain is not None
    out_bytes = jnp.dtype(out_dtype).itemsize
    w_bytes = jnp.dtype(w.dtype).itemsize
    vmem = 2 * (tm * tk * 2 + tk * tn * w_bytes + tm * tn * out_bytes) + tm * tn * 4 + (4 << 20)
    if w_bytes != 2:
        vmem += tk * tn * 2
    if nk == 1:
        grid = (m // tm, n // tn)
        in_specs = [pl.BlockSpec((tm, ks), lambda i, j: (i, 0)) for ks in k_sizes]
        in_specs.append(_weight_spec(layer, (tk, tn), lambda i, j: (0, j)))
        row_index = lambda i, j: (i, 0)
        io_index = lambda i, j: (i, j)
        body = functools.partial(_mm_kernel_single, mode=mode, k_sizes=k_sizes, scaled=scaled, norm_out=norm_out,
                                 norm_dim=kdim)
        scratch, sem = [], ("parallel", "parallel")
    else:
        assert len(xs) == 1 and not scaled
        grid = (m // tm, n // tn, nk)
        in_specs = [pl.BlockSpec((tm, tk), lambda i, j, k: (i, k)),
                    _weight_spec(layer, (tk, tn), lambda i, j, k: (k, j))]
        row_index = lambda i, j, k: (i, 0)
        io_index = lambda i, j, k: (i, j)
        body = functools.partial(_mm_kernel_multi, mode=mode, norm_out=norm_out)
        sem = ("parallel", "parallel", "arbitrary")
        if mode == "residual":
            assert out_dtype == F32
            scratch = []
            vmem += tm * tn * 4
        else:
            scratch = [pltpu.VMEM((tm, tn), F32)]
            vmem += tm * tn * 4
    args = list(xs) + [w]
    if scaled:
        in_specs.append(pl.BlockSpec((tm, row_ssq.shape[1]), row_index))
        args.append(row_ssq)
        vmem += 2 * tm * row_ssq.shape[1] * 4 + tm * tn * 4
    if mode == "residual":
        in_specs.append(pl.BlockSpec((tm, tn), io_index))
        args.append(res)
        vmem += 2 * tm * tn * 4
    out_shape = jax.ShapeDtypeStruct((m, n), out_dtype)
    out_specs = pl.BlockSpec((tm, tn), io_index)
    if norm_out:
        gain_spec, gain, side_shapes = _norm_side(m, n, tn, norm_gain, io_index)
        in_specs.append(gain_spec)
        args.append(gain)
        out_shape = (out_shape,) + side_shapes
        out_specs = (out_specs, pl.BlockSpec((tm, tn), io_index), pl.BlockSpec((tm, NORM_LANES), io_index))
        vmem += 2 * tm * (tn * 2 + NORM_LANES * 4) + tm * tn * 4
    return pl.pallas_call(
        body,
        out_shape=out_shape,
        grid=grid,
        in_specs=in_specs,
        out_specs=out_specs,
        scratch_shapes=scratch,
        compiler_params=_cparams(sem, vmem),
        name="proj_" + mode,
    )(*args)


def _mm_kernel_stationary(*refs, mode, k_sizes, chunk_rows, scaled, norm_out, norm_dim):
    x_refs, wchunk_ref, ssq_ref, res_ref, o_ref, side, (wbf_ref,) = _unpack(refs, len(k_sizes), scaled, mode,
                                                                            norm_out)
    j, i = pl.program_id(0), pl.program_id(1)

    @pl.when(j < pl.num_programs(0) - 1)
    def _():
        r0 = pl.multiple_of(i * chunk_rows, chunk_rows)
        wbf_ref[j % 2, pl.ds(r0, chunk_rows), :] = wchunk_ref[...].astype(BF16)

    @pl.when(j > 0)
    def _():
        w_tile = wbf_ref.at[(j + 1) % 2]
        acc = _split_dot(x_refs, k_sizes, lambda off, ks: w_tile[off:off + ks, :])
        _epilogue(_row_scale(acc, ssq_ref, norm_dim), mode, res_ref, o_ref, side)


def _matmul_stationary(x, w, layer, *, mode="none", res=None, out_dtype=F32, tm=1024, tn=1024,
                       row_ssq=None, norm_gain=None):
    xs = x if isinstance(x, (tuple, list)) else (x,)
    m = xs[0].shape[0]
    k_sizes = tuple(xi.shape[1] for xi in xs)
    _, kdim, n = w.shape
    assert sum(k_sizes) == kdim and m % tm == 0 and n % tn == 0
    n_i, n_j = m // tm, n // tn
    assert kdim % n_i == 0
    chunk_rows = kdim // n_i
    assert chunk_rows % 16 == 0
    scaled, norm_out = row_ssq is not None, norm_gain is not None
    out_bytes = jnp.dtype(out_dtype).itemsize

    def row_tile(j, i):
        return jnp.where(j == 0, 0, i)

    row_index = lambda j, i: (row_tile(j, i), 0)
    io_index = lambda j, i: (row_tile(j, i), jnp.maximum(j - 1, 0))
    in_specs = [pl.BlockSpec((tm, ks), row_index) for ks in k_sizes]
    in_specs.append(_weight_spec(layer, (chunk_rows, tn), lambda j, i: (i, jnp.minimum(j, n_j - 1))))
    args = list(xs) + [w]
    vmem = (2 * (tm * kdim * 2 + chunk_rows * tn * 4 + tm * tn * out_bytes) + 2 * kdim * tn * 2
            + tm * tn * 4 + (4 << 20))
    if scaled:
        in_specs.append(pl.BlockSpec((tm, row_ssq.shape[1]), row_index))
        args.append(row_ssq)
        vmem += 2 * tm * row_ssq.shape[1] * 4 + tm * tn * 4
    if mode == "residual":
        in_specs.append(pl.BlockSpec((tm, tn), io_index))
        args.append(res)
        vmem += 2 * tm * tn * 4
    out_shape = jax.ShapeDtypeStruct((m, n), out_dtype)
    out_specs = pl.BlockSpec((tm, tn), io_index)
    if norm_out:
        gain_spec, gain, side_shapes = _norm_side(m, n, tn, norm_gain, io_index)
        in_specs.append(gain_spec)
        args.append(gain)
        out_shape = (out_shape,) + side_shapes
        out_specs = (out_specs, pl.BlockSpec((tm, tn), io_index), pl.BlockSpec((tm, NORM_LANES), io_index))
        vmem += 2 * tm * (tn * 2 + NORM_LANES * 4) + tm * tn * 4
    return pl.pallas_call(
        functools.partial(_mm_kernel_stationary, mode=mode, k_sizes=k_sizes, chunk_rows=chunk_rows, scaled=scaled,
                          norm_out=norm_out, norm_dim=kdim),
        out_shape=out_shape,
        grid=(n_j + 1, n_i),
        in_specs=in_specs,
        out_specs=out_specs,
        scratch_shapes=[pltpu.VMEM((2, kdim, tn), BF16)],
        compiler_params=_cparams(("arbitrary", "arbitrary"), vmem),
        name="projws_" + mode,
    )(*args)


def _chunk_pos(shape, chunk):
    return lax.broadcasted_iota(jnp.int32, shape, 0) % chunk


def _cumsum_in_chunks(x, chunk):
    pos = _chunk_pos(x.shape, chunk)
    shift = 1
    while shift < chunk:
        x = x + jnp.where(pos >= shift, pltpu.roll(x, shift, 0), 0.0)
        shift *= 2
    return x


def _causal_mask(chunk):
    r = lax.broadcasted_iota(jnp.int32, (chunk, chunk), 0)
    c = lax.broadcasted_iota(jnp.int32, (chunk, chunk), 1)
    return r >= c


_NT = (((1,), (1,)), ((), ()))
_TN = (((0,), (0,)), ((), ()))


def _lower_bound(logits, layer):
    rows = [logits[i:i + 1] for i in range(logits.shape[0])]
    mx = functools.reduce(jnp.maximum, rows)
    es = [jnp.exp(r - mx) for r in rows]
    den = functools.reduce(lambda a, b: a + b, es)
    lb = es[0] / den
    for i in range(1, layer + 1):
        lb = lb + es[i] / den
    return lb


def _hgrn_kernel(*refs, chunk, n_chunks, layer, has_state, heads):
    if has_state:
        q_ref, f_ref, v_ref, g_ref, lbl_ref, gn_ref, s0_ref, o_ref, sout_ref, st_ref = refs
    else:
        (q_ref, f_ref, v_ref, g_ref, lbl_ref, gn_ref, o_ref, sout_ref, st_ref), s0_ref = refs, None
    t = pl.program_id(2)
    last = t == pl.num_programs(2) - 1
    mask = _causal_mask(chunk)

    @pl.when(t == 0)
    def _():
        for hh in range(heads):
            if has_state:
                st_ref[hh] = s0_ref[0, hh].T
            else:
                st_ref[hh] = jnp.zeros(st_ref.shape[1:], F32)

    finals = []
    for hh in range(heads):
        ls = slice(hh * HEAD_LANES, (hh + 1) * HEAD_LANES)
        lb = _lower_bound(lbl_ref[:, ls], layer)
        f = lb + (1.0 - lb) * jax.nn.sigmoid(f_ref[:, ls])
        q = jax.nn.silu(q_ref[:, ls])
        k = 1.0 - f
        b = _cumsum_in_chunks(jnp.log(f), chunk)
        q_in = (q * jnp.exp(b)).astype(BF16)
        k_in = (k * jnp.exp(-b)).astype(BF16)
        v = v_ref[:, ls].astype(BF16)
        st = st_ref[hh]
        outs = []
        for c in range(n_chunks):
            sl = slice(c * chunk, (c + 1) * chunk)
            b_c = b[sl]
            b_last = b_c[chunk - 1:chunk]
            k_out = (k[sl] * jnp.exp(b_last - b_c)).astype(BF16)
            attn = lax.dot_general(q_in[sl], k_in[sl], _NT, preferred_element_type=F32)
            attn = jnp.where(mask, attn, 0.0).astype(BF16)
            o_intra = jnp.dot(attn, v[sl], preferred_element_type=F32)
            o_inter = lax.dot_general(q_in[sl], st.astype(BF16), _NT, preferred_element_type=F32)
            kv_t = lax.dot_general(v[sl], k_out, _TN, preferred_element_type=F32)
            st = jnp.exp(b_last) * st + kv_t
            outs.append(o_intra + o_inter)
        st_ref[hh] = st
        o = outs[0] if n_chunks == 1 else jnp.concatenate(outs, axis=0)
        o = o * lax.rsqrt(jnp.mean(o * o, axis=-1, keepdims=True) + EPS)
        o = o * gn_ref[:, ls] * jax.nn.silu(g_ref[:, ls])
        o_ref[:, ls] = o.astype(o_ref.dtype)
        finals.append(st)

    @pl.when(last)
    def _():
        for hh in range(heads):
            sout_ref[0, hh] = finals[hh].T


def _ret_kernel(*refs, chunk, n_chunks, has_state, heads, dv):
    if has_state:
        q_ref, k_ref, v_ref, g_ref, cos_ref, sin_ref, lg_ref, gn_ref, s0_ref, o_ref, sout_ref, st_ref = refs
    else:
        (q_ref, k_ref, v_ref, g_ref, cos_ref, sin_ref, lg_ref, gn_ref, o_ref, sout_ref, st_ref), s0_ref = refs, None
    t = pl.program_id(2)
    last = t == pl.num_programs(2) - 1
    cosf, sinf = cos_ref[...], sin_ref[...]
    half = HEAD_LANES // 2
    mask = _causal_mask(chunk)
    pos1 = (_chunk_pos(cosf.shape, chunk) + 1).astype(F32)

    @pl.when(t == 0)
    def _():
        for hh in range(heads):
            if has_state:
                st_ref[hh] = s0_ref[0, hh]
            else:
                st_ref[hh] = jnp.zeros(st_ref.shape[1:], F32)

    finals = []
    for hh in range(heads):
        ls = slice(hh * HEAD_LANES, (hh + 1) * HEAD_LANES)
        ws = slice(hh * dv, (hh + 1) * dv)
        q, k = q_ref[:, ls], k_ref[:, ls]
        qr = q * cosf + pltpu.roll(q, half, 1) * sinf
        kr = (k * cosf + pltpu.roll(k, half, 1) * sinf) * (HEAD_LANES ** -0.5)
        lg_wide = lg_ref[hh]
        lg = lg_wide[:, :HEAD_LANES]
        b = pos1 * lg
        b_last = float(chunk) * lg
        q_in = (qr * jnp.exp(b)).astype(BF16)
        k_in = (kr * jnp.exp(-b)).astype(BF16)
        k_out = (kr * jnp.exp(b_last - b)).astype(BF16)
        decay = jnp.exp(float(chunk) * lg_wide)
        v = v_ref[:, ws].astype(BF16)
        st = st_ref[hh]
        outs = []
        for c in range(n_chunks):
            sl = slice(c * chunk, (c + 1) * chunk)
            attn = lax.dot_general(q_in[sl], k_in[sl], _NT, preferred_element_type=F32)
            attn = jnp.where(mask, attn, 0.0).astype(BF16)
            o_intra = jnp.dot(attn, v[sl], preferred_element_type=F32)
            o_inter = jnp.dot(q_in[sl], st.astype(BF16), preferred_element_type=F32)
            kv = lax.dot_general(k_out[sl], v[sl], _TN, preferred_element_type=F32)
            st = decay * st + kv
            outs.append(o_intra + o_inter)
        st_ref[hh] = st
        o = outs[0] if n_chunks == 1 else jnp.concatenate(outs, axis=0)
        c0 = o - jnp.mean(o, axis=-1, keepdims=True)
        o = c0 * lax.rsqrt(jnp.mean(c0 * c0, axis=-1, keepdims=True) + EPS)
        o = o * gn_ref[:, ws] * jax.nn.silu(g_ref[:, ws])
        o_ref[:, ws] = o.astype(o_ref.dtype)
        finals.append(st)

    @pl.when(last)
    def _():
        for hh in range(heads):
            sout_ref[0, hh] = finals[hh]


def _rotary_tables(pos):
    half = HEAD_LANES // 2
    inv = 1.0 / (ROPE_BASE ** jnp.linspace(0.0, 1.0, half, dtype=F32))
    ang = pos.astype(F32)[:, None] * inv[None, :]
    cos, sin = jnp.cos(ang), jnp.sin(ang)
    return jnp.concatenate([cos, cos], axis=-1), jnp.concatenate([-sin, sin], axis=-1)


def _even_mixer(proj, batch, seq, pos, layer, s_a, s_b, a_shape, b_shape, lb_logits, g_a, g_b,
                rows_per_step=1024, heads_per_step=4):
    m = proj.shape[0]
    a_heads, dk, dv_a = a_shape
    b_heads, dk_b, dv_b = b_shape
    assert dk == HEAD_LANES and dv_a == HEAD_LANES and dk_b == HEAD_LANES
    chunk = min(CHUNK, seq)
    rows = min(rows_per_step, seq)
    assert seq % rows == 0 and rows % chunk == 0
    nt = seq // rows
    has_state = s_a is not None
    hp_a = a_heads if nt == 1 else heads_per_step
    hp_b = b_heads if nt == 1 else heads_per_step
    assert a_heads % hp_a == 0 and b_heads % hp_b == 0
    a_w, a_v = a_heads * dk, a_heads * dv_a
    b_qk, b_v = b_heads * HEAD_LANES, b_heads * dv_b
    off_b = 2 * a_w + 2 * a_v
    grid_sem = ("parallel", "parallel", "arbitrary")

    def cols(offset, width, hp):
        blk = hp * width
        assert offset % blk == 0
        return pl.BlockSpec((rows, blk), lambda b, h, t: (b * nt + t, offset // blk + h))

    n_layers = lb_logits.shape[0]
    wa = hp_a * HEAD_LANES
    in_specs = [cols(0, dk, hp_a), cols(a_w, dk, hp_a), cols(2 * a_w, dv_a, hp_a), cols(2 * a_w + a_v, dv_a, hp_a),
                pl.BlockSpec((n_layers, wa), lambda b, h, t: (0, h)),
                pl.BlockSpec((1, wa), lambda b, h, t: (0, h))]
    args = [proj, proj, proj, proj, lb_logits, g_a.reshape(1, a_v)]
    if has_state:
        in_specs.append(pl.BlockSpec((1, hp_a, dk, dv_a), lambda b, h, t: (b, h, 0, 0)))
        args.append(s_a)
    o_a, new_a = pl.pallas_call(
        functools.partial(_hgrn_kernel, chunk=chunk, n_chunks=rows // chunk, layer=layer, has_state=has_state,
                          heads=hp_a),
        out_shape=(jax.ShapeDtypeStruct((m, a_v), BF16),
                   jax.ShapeDtypeStruct((batch, a_heads, dk, dv_a), F32)),
        grid=(batch, a_heads // hp_a, nt),
        in_specs=in_specs,
        out_specs=(pl.BlockSpec((rows, wa), lambda b, h, t: (b * nt + t, h)),
                   pl.BlockSpec((1, hp_a, dk, dv_a), lambda b, h, t: (b, h, 0, 0))),
        scratch_shapes=[pltpu.VMEM((hp_a, dv_a, dk), F32)],
        compiler_params=_cparams(grid_sem, 48 << 20),
        name="hgrn2",
    )(*args)

    cosf, sinf = _rotary_tables(pos)
    log_gamma = jnp.log(1.0 - jnp.exp2(-5.0 - jnp.arange(b_heads, dtype=F32)))
    lg = jnp.broadcast_to(log_gamma[:, None, None], (b_heads, 1, dv_b))
    wv = hp_b * dv_b
    in_specs = [cols(off_b, HEAD_LANES, hp_b), cols(off_b + b_qk, HEAD_LANES, hp_b),
                cols(off_b + 2 * b_qk, dv_b, hp_b), cols(off_b + 2 * b_qk + b_v, dv_b, hp_b),
                pl.BlockSpec((rows, HEAD_LANES), lambda b, h, t: (t, 0)),
                pl.BlockSpec((rows, HEAD_LANES), lambda b, h, t: (t, 0)),
                pl.BlockSpec((hp_b, 1, dv_b), lambda b, h, t: (h, 0, 0)),
                pl.BlockSpec((1, wv), lambda b, h, t: (0, h))]
    args = [proj, proj, proj, proj, cosf, sinf, lg, g_b.reshape(1, b_v)]
    if has_state:
        in_specs.append(pl.BlockSpec((1, hp_b, HEAD_LANES, dv_b), lambda b, h, t: (b, h, 0, 0)))
        args.append(s_b)
    o_b, new_b = pl.pallas_call(
        functools.partial(_ret_kernel, chunk=chunk, n_chunks=rows // chunk, has_state=has_state, heads=hp_b,
                          dv=dv_b),
        out_shape=(jax.ShapeDtypeStruct((m, b_v), BF16),
                   jax.ShapeDtypeStruct((batch, b_heads, HEAD_LANES, dv_b), F32)),
        grid=(batch, b_heads // hp_b, nt),
        in_specs=in_specs,
        out_specs=(pl.BlockSpec((rows, wv), lambda b, h, t: (b * nt + t, h)),
                   pl.BlockSpec((1, hp_b, HEAD_LANES, dv_b), lambda b, h, t: (b, h, 0, 0))),
        scratch_shapes=[pltpu.VMEM((hp_b, HEAD_LANES, dv_b), F32)],
        compiler_params=_cparams(grid_sem, 48 << 20),
        name="retention",
    )(*args)
    return o_a, o_b, new_a, new_b


SUBLANES = 8


def _rglru_block(xb, yb, pos_base, rows, first_pos_zero, cw_ref, cb_ref, wa_ref, wx_ref, gab_ref, gxb_ref, lam_ref,
                 xpad_ref, h_ref, a_ref, b_ref):
    width = cw_ref.shape[0]
    lead = SUBLANES - (width - 1)
    xpad_ref[SUBLANES:SUBLANES + rows, :] = xb
    cw = cw_ref[...]
    acc = xpad_ref[lead:lead + rows, :] * cw[0:1]
    for j in range(1, width):
        acc = acc + xpad_ref[lead + j:lead + j + rows, :] * cw[j:j + 1]
    conv = cb_ref[...] + acc
    tail = xpad_ref[rows + lead:rows + SUBLANES, :]
    xpad_ref[lead:SUBLANES, :] = tail

    conv_bf = conv.astype(BF16)
    r = jax.nn.sigmoid(jnp.dot(conv_bf, wa_ref[0], preferred_element_type=F32) + gab_ref[...])
    i = jax.nn.sigmoid(jnp.dot(conv_bf, wx_ref[0], preferred_element_type=F32) + gxb_ref[...])
    log_a = -RG_C * r * jax.nn.softplus(-lam_ref[...])
    a = jnp.exp(log_a)
    mult = jnp.sqrt(1.0 - a * a)
    if first_pos_zero:
        row = lax.broadcasted_iota(jnp.int32, a.shape, 0)
        mult = jnp.where(row + pos_base == 0, 1.0, mult)
    bb = mult * (i * conv)

    grouped = (rows // SUBLANES, SUBLANES, a.shape[1])
    a3, b3 = a.reshape(grouped), bb.reshape(grouped)
    sub = lax.broadcasted_iota(jnp.int32, grouped, 1)
    shift = 1
    while shift < SUBLANES:
        take = sub >= shift
        b3 = jnp.where(take, a3 * pltpu.roll(b3, shift, 1) + b3, b3)
        a3 = jnp.where(take, a3 * pltpu.roll(a3, shift, 1), a3)
        shift *= 2
    a_ref[...] = a3.reshape(a.shape)
    b_ref[...] = b3.reshape(a.shape)

    def group(gidx, carry):
        r0 = pl.multiple_of(gidx * SUBLANES, SUBLANES)
        hg = a_ref[pl.ds(r0, SUBLANES), :] * carry + b_ref[pl.ds(r0, SUBLANES), :]
        b_ref[pl.ds(r0, SUBLANES), :] = hg
        return hg[SUBLANES - 1:SUBLANES, :]

    h_ref[...] = lax.fori_loop(0, rows // SUBLANES, group, h_ref[...], unroll=True)
    return jax.nn.gelu(yb, approximate=True) * b_ref[...], tail


def _griffin_kernel(*refs, rows, streams, first_pos_zero, has_state):
    if has_state:
        (xb_ref, yb_ref, cw_ref, cb_ref, wa_ref, wx_ref, gab_ref, gxb_ref, lam_ref, sconv_ref, h0_ref,
         o_ref, hlast_ref, nconv_ref, xpad_ref, h_ref, a_ref, b_ref) = refs
    else:
        (xb_ref, yb_ref, cw_ref, cb_ref, wa_ref, wx_ref, gab_ref, gxb_ref, lam_ref,
         o_ref, hlast_ref, nconv_ref, xpad_ref, h_ref, a_ref, b_ref) = refs
        sconv_ref = h0_ref = None
    t = pl.program_id(2)

    @pl.when(t == 0)
    def _():
        for q in range(streams):
            if has_state:
                xpad_ref[q, SUBLANES - sconv_ref.shape[1]:SUBLANES, :] = sconv_ref[q]
                h_ref[q] = h0_ref[q]
            else:
                xpad_ref[q, 0:SUBLANES, :] = jnp.zeros((SUBLANES, xpad_ref.shape[2]), F32)
                h_ref[q] = jnp.zeros(h_ref.shape[1:], F32)

    tails = []
    for q in range(streams):
        rs = slice(q * rows, (q + 1) * rows)
        o, tail = _rglru_block(xb_ref[rs, :], yb_ref[rs, :], t * rows, rows, first_pos_zero, cw_ref, cb_ref, wa_ref,
                               wx_ref, gab_ref, gxb_ref, lam_ref, xpad_ref.at[q], h_ref.at[q], a_ref.at[q],
                               b_ref.at[q])
        o_ref[rs, :] = o.astype(o_ref.dtype)
        tails.append(tail)

    @pl.when(t == pl.num_programs(2) - 1)
    def _():
        for q in range(streams):
            hlast_ref[q] = h_ref[q]
            nconv_ref[q] = tails[q]


def _pair_blockdiag(w, pair):
    n, c, _ = w.shape
    w = w.reshape(n // pair, pair, c, c)
    rows = []
    for p in range(pair):
        blocks = [w[:, p] if q == p else jnp.zeros_like(w[:, p]) for q in range(pair)]
        rows.append(jnp.concatenate(blocks, axis=-1))
    return jnp.concatenate(rows, axis=-2)


def _odd_mixer(xy, batch, seq, pos0, h0, s_conv, conv_w, conv_b, wa, wx, ga_b, gx_b, lam,
               rows_per_step=512, short_streams_per_step=8):
    m = xy.shape[0]
    d_rnn = conv_w.shape[1]
    width = conv_w.shape[0]
    blk = wa.shape[1]
    nblk = d_rnn // blk
    rows = min(rows_per_step, seq)
    assert seq % rows == 0 and rows % SUBLANES == 0 and rows >= width - 1
    nt = seq // rows
    streams = short_streams_per_step if (nt == 1 and batch % short_streams_per_step == 0) else 1
    has_state = h0 is not None

    def vec(a):
        return a.reshape(1, d_rnn)

    def vspec():
        return pl.BlockSpec((1, blk), lambda b, j, t: (0, j))

    def rows_spec(col0):
        return pl.BlockSpec((streams * rows, blk), lambda b, j, t: (b * nt + t, col0 + j))

    def state_spec(n):
        return pl.BlockSpec((streams, n, blk), lambda b, j, t: (b, 0, j))

    in_specs = [rows_spec(0), rows_spec(nblk),
                pl.BlockSpec((width, blk), lambda b, j, t: (0, j)),
                vspec(),
                pl.BlockSpec((1, blk, blk), lambda b, j, t: (j, 0, 0)),
                pl.BlockSpec((1, blk, blk), lambda b, j, t: (j, 0, 0)),
                vspec(), vspec(), vspec()]
    args = [xy, xy, conv_w, vec(conv_b), wa, wx, vec(ga_b), vec(gx_b), vec(lam)]
    if has_state:
        in_specs += [state_spec(width - 1), state_spec(1)]
        args += [s_conv, h0.reshape(batch, 1, d_rnn)]
    o, h_last, new_conv = pl.pallas_call(
        functools.partial(_griffin_kernel, rows=rows, streams=streams, first_pos_zero=(pos0 == 0),
                          has_state=has_state),
        out_shape=(jax.ShapeDtypeStruct((m, d_rnn), BF16),
                   jax.ShapeDtypeStruct((batch, 1, d_rnn), F32),
                   jax.ShapeDtypeStruct((batch, width - 1, d_rnn), F32)),
        grid=(batch // streams, nblk, nt),
        in_specs=in_specs,
        out_specs=(rows_spec(0), state_spec(1), state_spec(width - 1)),
        scratch_shapes=[pltpu.VMEM((streams, rows + SUBLANES, blk), F32),
                        pltpu.VMEM((streams, 1, blk), F32),
                        pltpu.VMEM((streams, rows, blk), F32),
                        pltpu.VMEM((streams, rows, blk), F32)],
        compiler_params=_cparams(("parallel", "parallel", "arbitrary"), 32 << 20),
        name="griffin",
    )(*args)
    return o, h_last.reshape(batch, d_rnn), new_conv


def _trunk(x, pos0, states, p):
    batch, seq, d = x.shape
    depth = p["norm_mix"].shape[0]
    pos = pos0 + jnp.arange(seq, dtype=jnp.int32)
    h = x.reshape(batch * seq, d)
    many_rows = h.shape[0] >= 8 * 1024

    def project(xin, w, layer, **kw):
        if many_rows:
            narrow = w.shape[1] > 4096 or "norm_gain" in kw
            return _matmul_stationary(xin, w, layer, tn=(512 if narrow else 1024), **kw)
        return _matmul(xin, w, layer, tn=512, tk=w.shape[1], **kw)

    n_hgrn, n_ret, n_rg, n_conv = [], [], [], []
    hn, ssq = _rms_norm(h, p["norm_mix"][0], BF16), None
    for l in range(depth):
        if l % 2 == 0:
            e = l // 2
            proj = project(hn, p["w_in_even"], e, row_ssq=ssq)
            s_a, s_b = (None, None) if states is None else (states[0][e], states[1][e])
            o_a, o_b, sa, sb = _even_mixer(proj, batch, seq, pos, l, s_a, s_b, p["hgrn_shape"], p["ret_shape"],
                                           p["hgrn_lb_logits"], p["hgrn_norm"][e], p["ret_norm"][e])
            n_hgrn.append(sa)
            n_ret.append(sb)
            h, hm, ssq_m = project((o_a, o_b), p["w_out_even"], e, mode="residual", res=h,
                                   norm_gain=p["norm_mlp"][l])
        else:
            od = l // 2
            xy = project(hn, p["w_in_odd"], od, row_ssq=ssq)
            h0, sc = (None, None) if states is None else (states[2][od], states[3][od])
            o, sh, scn = _odd_mixer(xy, batch, seq, pos0, h0, sc, p["conv_w"][od], p["conv_b"][od],
                                    p["gate_a_w"][od], p["gate_x_w"][od], p["gate_a_b"][od], p["gate_x_b"][od],
                                    p["rglru_lambda"][od])
            n_rg.append(sh)
            n_conv.append(scn)
            h, hm, ssq_m = project(o, p["w_out_odd"], od, mode="residual", res=h, norm_gain=p["norm_mlp"][l])
        up = project(hm, p["w_up"], l, mode="relu2", out_dtype=BF16, row_ssq=ssq_m)
        if l + 1 < depth:
            h, hn, ssq = _matmul(up, p["w_down_bf16"], l, mode="residual", res=h, tk=2048,
                                 norm_gain=p["norm_mix"][l + 1])
        else:
            h = _matmul(up, p["w_down_bf16"], l, mode="residual", res=h, tk=4096)
    out = _rms_norm(h, p["norm_final"], x.dtype).reshape(batch, seq, d)
    return out, jnp.stack(n_hgrn), jnp.stack(n_ret), jnp.stack(n_rg), jnp.stack(n_conv)


def kernel(x_prompt, x_sample, state_hgrn, state_ret, state_rglru, state_conv, norm_mix, norm_mlp, norm_final,
           w_in_even, hgrn_lb_logits, hgrn_norm, ret_norm, w_out_even, w_in_odd, conv_w, conv_b, gate_a_w,
           gate_a_b, gate_x_w, gate_x_b, rglru_lambda, w_out_odd, w_up, w_down):
    past_len = 4096

    p = dict(
        norm_mix=norm_mix, norm_mlp=norm_mlp, norm_final=norm_final,
        w_in_even=w_in_even, w_out_even=w_out_even, w_in_odd=w_in_odd, w_out_odd=w_out_odd, w_up=w_up,
        w_down_bf16=w_down.astype(BF16),
        hgrn_lb_logits=hgrn_lb_logits, hgrn_norm=hgrn_norm, ret_norm=ret_norm,
        conv_w=conv_w, conv_b=conv_b,
        gate_a_w=jnp.stack([_pair_blockdiag(w, RG_PAIR) for w in gate_a_w.astype(BF16)]),
        gate_x_w=jnp.stack([_pair_blockdiag(w, RG_PAIR) for w in gate_x_w.astype(BF16)]),
        gate_a_b=gate_a_b, gate_x_b=gate_x_b, rglru_lambda=rglru_lambda,
        hgrn_shape=state_hgrn.shape[2:], ret_shape=state_ret.shape[2:],
    )
    y_p, hg_p, rt_p, rg_p, cv_p = _trunk(x_prompt, 0, None, p)
    y_s, hg_s, rt_s, rg_s, cv_s = _trunk(x_sample, past_len, (state_hgrn, state_ret, state_rglru, state_conv), p)
    return (y_p, y_s, hg_p, rt_p, rg_p, cv_p, hg_s, rt_s, rg_s, cv_s)
```

```python
import functools

import jax
import jax.numpy as jnp
from jax import lax
from jax.experimental import pallas as pl
from jax.experimental.pallas import tpu as pltpu

EPS = 1e-6
CHUNK = 64
ROPE_BASE = 10000.0
RG_C = 8.0
HEAD_LANES = 128
RG_PAIR = 2

V7X_VMEM_BYTES = 64 * 1024 * 1024
VMEM_CAP_BYTES = V7X_VMEM_BYTES - 6 * 1024 * 1024

F32 = jnp.float32
BF16 = jnp.bfloat16


def _cparams(semantics, vmem_bytes):
    return pltpu.CompilerParams(dimension_semantics=semantics,
                                vmem_limit_bytes=int(min(max(vmem_bytes, 32 * 1024 * 1024), VMEM_CAP_BYTES)))


def _rms_kernel(x_ref, g_ref, o_ref):
    x = x_ref[...]
    y = x * lax.rsqrt(jnp.mean(x * x, axis=-1, keepdims=True) + EPS)
    o_ref[...] = (y * g_ref[...]).astype(o_ref.dtype)


def _rms_norm(x, g, out_dtype, tm=512):
    m, d = x.shape
    tm = min(tm, m)
    io_bytes = 2 * tm * d * (4 + jnp.dtype(out_dtype).itemsize)
    return pl.pallas_call(
        _rms_kernel,
        out_shape=jax.ShapeDtypeStruct((m, d), out_dtype),
        grid=(m // tm,),
        in_specs=[pl.BlockSpec((tm, d), lambda i: (i, 0)),
                  pl.BlockSpec((1, d), lambda i: (0, 0))],
        out_specs=pl.BlockSpec((tm, d), lambda i: (i, 0)),
        compiler_params=_cparams(("parallel",), io_bytes + 3 * tm * d * 4),
        name="rms_norm",
    )(x, g.reshape(1, d))


def _epilogue(acc, mode, res_ref, o_ref):
    if mode == "relu2":
        acc = jnp.square(jnp.maximum(acc, 0.0))
    elif mode == "residual":
        acc = res_ref[...] + acc
    o_ref[...] = acc.astype(o_ref.dtype)


def _split_dot(x_refs, k_sizes, w_rows):
    acc, off = None, 0
    for x_ref, ks in zip(x_refs, k_sizes):
        part = jnp.dot(x_ref[...], w_rows(off, ks), preferred_element_type=F32)
        acc = part if acc is None else acc + part
        off += ks
    return acc


def _mm_kernel_single(*refs, mode, k_sizes):
    n_x = len(k_sizes)
    x_refs, w_ref = refs[:n_x], refs[n_x]
    if mode == "residual":
        res_ref, o_ref = refs[n_x + 1:]
    else:
        (o_ref,), res_ref = refs[n_x + 1:], None
    acc = _split_dot(x_refs, k_sizes, lambda off, ks: w_ref[off:off + ks, :].astype(BF16))
    _epilogue(acc, mode, res_ref, o_ref)


def _mm_kernel_multi(*refs, mode):
    if mode == "residual":
        x_ref, w_ref, res_ref, o_ref = refs
    else:
        x_ref, w_ref, o_ref, acc_ref = refs
    k = pl.program_id(2)
    part = jnp.dot(x_ref[...], w_ref[...].astype(BF16), preferred_element_type=F32)

    if mode == "residual":
        @pl.when(k == 0)
        def _():
            o_ref[...] = res_ref[...] + part

        @pl.when(k > 0)
        def _():
            o_ref[...] += part
    else:
        @pl.when(k == 0)
        def _():
            acc_ref[...] = part

        @pl.when(k > 0)
        def _():
            acc_ref[...] += part

        @pl.when(k == pl.num_programs(2) - 1)
        def _():
            _epilogue(acc_ref[...], mode, None, o_ref)


def _weight_spec(layer, block, index_map):
    return pl.BlockSpec((None,) + block, lambda *g: (layer,) + tuple(index_map(*g)))


def _matmul(x, w, layer, *, mode="none", res=None, out_dtype=F32, tm=1024, tn=1024, tk=4096):
    xs = x if isinstance(x, (tuple, list)) else (x,)
    m = xs[0].shape[0]
    k_sizes = tuple(xi.shape[1] for xi in xs)
    _, kdim, n = w.shape
    assert sum(k_sizes) == kdim
    tm, tn, tk = min(tm, m), min(tn, n), min(tk, kdim)
    nk = kdim // tk
    assert m % tm == 0 and n % tn == 0 and kdim % tk == 0
    out_bytes = jnp.dtype(out_dtype).itemsize
    w_bytes = jnp.dtype(w.dtype).itemsize
    vmem = 2 * (tm * tk * 2 + tk * tn * w_bytes + tm * tn * out_bytes) + tm * tn * 4 + (4 << 20)
    if w_bytes != 2:
        vmem += tk * tn * 2
    if nk == 1:
        grid = (m // tm, n // tn)
        in_specs = [pl.BlockSpec((tm, ks), lambda i, j: (i, 0)) for ks in k_sizes]
        in_specs.append(_weight_spec(layer, (tk, tn), lambda i, j: (0, j)))
        res_spec = pl.BlockSpec((tm, tn), lambda i, j: (i, j))
        out_spec = pl.BlockSpec((tm, tn), lambda i, j: (i, j))
        body = functools.partial(_mm_kernel_single, mode=mode, k_sizes=k_sizes)
        scratch, sem = [], ("parallel", "parallel")
    else:
        assert len(xs) == 1
        grid = (m // tm, n // tn, nk)
        in_specs = [pl.BlockSpec((tm, tk), lambda i, j, k: (i, k)),
                    _weight_spec(layer, (tk, tn), lambda i, j, k: (k, j))]
        res_spec = pl.BlockSpec((tm, tn), lambda i, j, k: (i, j))
        out_spec = pl.BlockSpec((tm, tn), lambda i, j, k: (i, j))
        body = functools.partial(_mm_kernel_multi, mode=mode)
        sem = ("parallel", "parallel", "arbitrary")
        if mode == "residual":
            assert out_dtype == F32
            scratch = []
            vmem += tm * tn * 4
        else:
            scratch = [pltpu.VMEM((tm, tn), F32)]
            vmem += tm * tn * 4
    args = list(xs) + [w]
    if mode == "residual":
        in_specs.append(res_spec)
        args.append(res)
        vmem += 2 * tm * tn * 4
    return pl.pallas_call(
        body,
        out_shape=jax.ShapeDtypeStruct((m, n), out_dtype),
        grid=grid,
        in_specs=in_specs,
        out_specs=out_spec,
        scratch_shapes=scratch,
        compiler_params=_cparams(sem, vmem),
        name="proj_" + mode,
    )(*args)


def _mm_kernel_stationary(*refs, mode, k_sizes, chunk_rows):
    n_x = len(k_sizes)
    x_refs, wchunk_ref = refs[:n_x], refs[n_x]
    if mode == "residual":
        res_ref, o_ref, wbf_ref = refs[n_x + 1:]
    else:
        (o_ref, wbf_ref), res_ref = refs[n_x + 1:], None
    j, i = pl.program_id(0), pl.program_id(1)

    @pl.when(j < pl.num_programs(0) - 1)
    def _():
        r0 = pl.multiple_of(i * chunk_rows, chunk_rows)
        wbf_ref[j % 2, pl.ds(r0, chunk_rows), :] = wchunk_ref[...].astype(BF16)

    @pl.when(j > 0)
    def _():
        w_tile = wbf_ref.at[(j + 1) % 2]
        acc = _split_dot(x_refs, k_sizes, lambda off, ks: w_tile[off:off + ks, :])
        _epilogue(acc, mode, res_ref, o_ref)


def _matmul_stationary(x, w, layer, *, mode="none", res=None, out_dtype=F32, tm=1024, tn=1024, k_slab=None):
    xs = x if isinstance(x, (tuple, list)) else (x,)
    m = xs[0].shape[0]
    slab, n_slabs = (0, 1) if k_slab is None else k_slab
    assert len(xs) == 1 or n_slabs == 1
    k_sizes = tuple(xi.shape[1] // n_slabs for xi in xs)
    _, k_total, n = w.shape
    kdim = k_total // n_slabs
    assert sum(k_sizes) == kdim and m % tm == 0 and n % tn == 0
    n_i, n_j = m // tm, n // tn
    assert kdim % n_i == 0
    chunk_rows = kdim // n_i
    assert chunk_rows % 16 == 0
    out_bytes = jnp.dtype(out_dtype).itemsize

    def row_tile(j, i):
        return jnp.where(j == 0, 0, i)

    in_specs = [pl.BlockSpec((tm, ks), lambda j, i: (row_tile(j, i), slab)) for ks in k_sizes]
    in_specs.append(_weight_spec(layer, (chunk_rows, tn), lambda j, i: (slab * n_i + i, jnp.minimum(j, n_j - 1))))
    io_spec = pl.BlockSpec((tm, tn), lambda j, i: (row_tile(j, i), jnp.maximum(j - 1, 0)))
    args = list(xs) + [w]
    vmem = (2 * (tm * kdim * 2 + chunk_rows * tn * 4 + tm * tn * out_bytes) + 2 * kdim * tn * 2
            + tm * tn * 4 + (4 << 20))
    if mode == "residual":
        in_specs.append(io_spec)
        args.append(res)
        vmem += 2 * tm * tn * 4
    return pl.pallas_call(
        functools.partial(_mm_kernel_stationary, mode=mode, k_sizes=k_sizes, chunk_rows=chunk_rows),
        out_shape=jax.ShapeDtypeStruct((m, n), out_dtype),
        grid=(n_j + 1, n_i),
        in_specs=in_specs,
        out_specs=io_spec,
        scratch_shapes=[pltpu.VMEM((2, kdim, tn), BF16)],
        compiler_params=_cparams(("arbitrary", "arbitrary"), vmem),
        name="projws_" + mode,
    )(*args)


def _chunk_pos(shape, chunk):
    return lax.broadcasted_iota(jnp.int32, shape, 0) % chunk


def _cumsum_in_chunks(x, chunk):
    pos = _chunk_pos(x.shape, chunk)
    shift = 1
    while shift < chunk:
        x = x + jnp.where(pos >= shift, pltpu.roll(x, shift, 0), 0.0)
        shift *= 2
    return x


def _causal_mask(chunk):
    r = lax.broadcasted_iota(jnp.int32, (chunk, chunk), 0)
    c = lax.broadcasted_iota(jnp.int32, (chunk, chunk), 1)
    return r >= c


_NT = (((1,), (1,)), ((), ()))
_TN = (((0,), (0,)), ((), ()))


def _lower_bound(logits, layer):
    rows = [logits[i:i + 1] for i in range(logits.shape[0])]
    mx = functools.reduce(jnp.maximum, rows)
    es = [jnp.exp(r - mx) for r in rows]
    den = functools.reduce(lambda a, b: a + b, es)
    lb = es[0] / den
    for i in range(1, layer + 1):
        lb = lb + es[i] / den
    return lb


def _hgrn_kernel(*refs, chunk, n_chunks, layer, has_state, heads):
    if has_state:
        q_ref, f_ref, v_ref, g_ref, lbl_ref, gn_ref, s0_ref, o_ref, sout_ref, st_ref = refs
    else:
        (q_ref, f_ref, v_ref, g_ref, lbl_ref, gn_ref, o_ref, sout_ref, st_ref), s0_ref = refs, None
    t = pl.program_id(2)
    last = t == pl.num_programs(2) - 1
    mask = _causal_mask(chunk)

    @pl.when(t == 0)
    def _():
        for hh in range(heads):
            if has_state:
                st_ref[hh] = s0_ref[0, hh].T
            else:
                st_ref[hh] = jnp.zeros(st_ref.shape[1:], F32)

    finals = []
    for hh in range(heads):
        ls = slice(hh * HEAD_LANES, (hh + 1) * HEAD_LANES)
        lb = _lower_bound(lbl_ref[:, ls], layer)
        f = lb + (1.0 - lb) * jax.nn.sigmoid(f_ref[:, ls])
        q = jax.nn.silu(q_ref[:, ls])
        k = 1.0 - f
        b = _cumsum_in_chunks(jnp.log(f), chunk)
        q_in = (q * jnp.exp(b)).astype(BF16)
        k_in = (k * jnp.exp(-b)).astype(BF16)
        v = v_ref[:, ls].astype(BF16)
        st = st_ref[hh]
        outs = []
        for c in range(n_chunks):
            sl = slice(c * chunk, (c + 1) * chunk)
            b_c = b[sl]
            b_last = b_c[chunk - 1:chunk]
            k_out = (k[sl] * jnp.exp(b_last - b_c)).astype(BF16)
            attn = lax.dot_general(q_in[sl], k_in[sl], _NT, preferred_element_type=F32)
            attn = jnp.where(mask, attn, 0.0).astype(BF16)
            o_intra = jnp.dot(attn, v[sl], preferred_element_type=F32)
            o_inter = lax.dot_general(q_in[sl], st.astype(BF16), _NT, preferred_element_type=F32)
            kv_t = lax.dot_general(v[sl], k_out, _TN, preferred_element_type=F32)
            st = jnp.exp(b_last) * st + kv_t
            outs.append(o_intra + o_inter)
        st_ref[hh] = st
        o = outs[0] if n_chunks == 1 else jnp.concatenate(outs, axis=0)
        o = o * lax.rsqrt(jnp.mean(o * o, axis=-1, keepdims=True) + EPS)
        o = o * gn_ref[:, ls] * jax.nn.silu(g_ref[:, ls])
        o_ref[:, ls] = o.astype(o_ref.dtype)
        finals.append(st)

    @pl.when(last)
    def _():
        for hh in range(heads):
            sout_ref[0, hh] = finals[hh].T


def _ret_kernel(*refs, chunk, n_chunks, has_state, heads, dv):
    if has_state:
        q_ref, k_ref, v_ref, g_ref, cos_ref, sin_ref, lg_ref, gn_ref, s0_ref, o_ref, sout_ref, st_ref = refs
    else:
        (q_ref, k_ref, v_ref, g_ref, cos_ref, sin_ref, lg_ref, gn_ref, o_ref, sout_ref, st_ref), s0_ref = refs, None
    t = pl.program_id(2)
    last = t == pl.num_programs(2) - 1
    cosf, sinf = cos_ref[...], sin_ref[...]
    half = HEAD_LANES // 2
    mask = _causal_mask(chunk)
    pos1 = (_chunk_pos(cosf.shape, chunk) + 1).astype(F32)

    @pl.when(t == 0)
    def _():
        for hh in range(heads):
            if has_state:
                st_ref[hh] = s0_ref[0, hh]
            else:
                st_ref[hh] = jnp.zeros(st_ref.shape[1:], F32)

    finals = []
    for hh in range(heads):
        ls = slice(hh * HEAD_LANES, (hh + 1) * HEAD_LANES)
        ws = slice(hh * dv, (hh + 1) * dv)
        q, k = q_ref[:, ls], k_ref[:, ls]
        qr = q * cosf + pltpu.roll(q, half, 1) * sinf
        kr = (k * cosf + pltpu.roll(k, half, 1) * sinf) * (HEAD_LANES ** -0.5)
        lg_wide = lg_ref[hh]
        lg = lg_wide[:, :HEAD_LANES]
        b = pos1 * lg
        b_last = float(chunk) * lg
        q_in = (qr * jnp.exp(b)).astype(BF16)
        k_in = (kr * jnp.exp(-b)).astype(BF16)
        k_out = (kr * jnp.exp(b_last - b)).astype(BF16)
        decay = jnp.exp(float(chunk) * lg_wide)
        v = v_ref[:, ws].astype(BF16)
        st = st_ref[hh]
        outs = []
        for c in range(n_chunks):
            sl = slice(c * chunk, (c + 1) * chunk)
            attn = lax.dot_general(q_in[sl], k_in[sl], _NT, preferred_element_type=F32)
            attn = jnp.where(mask, attn, 0.0).astype(BF16)
            o_intra = jnp.dot(attn, v[sl], preferred_element_type=F32)
            o_inter = jnp.dot(q_in[sl], st.astype(BF16), preferred_element_type=F32)
            kv = lax.dot_general(k_out[sl], v[sl], _TN, preferred_element_type=F32)
            st = decay * st + kv
            outs.append(o_intra + o_inter)
        st_ref[hh] = st
        o = outs[0] if n_chunks == 1 else jnp.concatenate(outs, axis=0)
        c0 = o - jnp.mean(o, axis=-1, keepdims=True)
        o = c0 * lax.rsqrt(jnp.mean(c0 * c0, axis=-1, keepdims=True) + EPS)
        o = o * gn_ref[:, ws] * jax.nn.silu(g_ref[:, ws])
        o_ref[:, ws] = o.astype(o_ref.dtype)
        finals.append(st)

    @pl.when(last)
    def _():
        for hh in range(heads):
            sout_ref[0, hh] = finals[hh]


def _rotary_tables(pos):
    half = HEAD_LANES // 2
    inv = 1.0 / (ROPE_BASE ** jnp.linspace(0.0, 1.0, half, dtype=F32))
    ang = pos.astype(F32)[:, None] * inv[None, :]
    cos, sin = jnp.cos(ang), jnp.sin(ang)
    return jnp.concatenate([cos, cos], axis=-1), jnp.concatenate([-sin, sin], axis=-1)


def _even_mixer(proj, batch, seq, pos, layer, s_a, s_b, a_shape, b_shape, lb_logits, g_a, g_b,
                rows_per_step=1024, heads_per_step=4):
    m = proj.shape[0]
    a_heads, dk, dv_a = a_shape
    b_heads, dk_b, dv_b = b_shape
    assert dk == HEAD_LANES and dv_a == HEAD_LANES and dk_b == HEAD_LANES
    chunk = min(CHUNK, seq)
    rows = min(rows_per_step, seq)
    assert seq % rows == 0 and rows % chunk == 0
    nt = seq // rows
    has_state = s_a is not None
    hp_a = a_heads if nt == 1 else heads_per_step
    hp_b = b_heads if nt == 1 else heads_per_step
    assert a_heads % hp_a == 0 and b_heads % hp_b == 0
    a_w, a_v = a_heads * dk, a_heads * dv_a
    b_qk, b_v = b_heads * HEAD_LANES, b_heads * dv_b
    off_b = 2 * a_w + 2 * a_v
    grid_sem = ("parallel", "parallel", "arbitrary")

    def cols(offset, width, hp):
        blk = hp * width
        assert offset % blk == 0
        return pl.BlockSpec((rows, blk), lambda b, h, t: (b * nt + t, offset // blk + h))

    n_layers = lb_logits.shape[0]
    wa = hp_a * HEAD_LANES
    in_specs = [cols(0, dk, hp_a), cols(a_w, dk, hp_a), cols(2 * a_w, dv_a, hp_a), cols(2 * a_w + a_v, dv_a, hp_a),
                pl.BlockSpec((n_layers, wa), lambda b, h, t: (0, h)),
                pl.BlockSpec((1, wa), lambda b, h, t: (0, h))]
    args = [proj, proj, proj, proj, lb_logits, g_a.reshape(1, a_v)]
    if has_state:
        in_specs.append(pl.BlockSpec((1, hp_a, dk, dv_a), lambda b, h, t: (b, h, 0, 0)))
        args.append(s_a)
    o_a, new_a = pl.pallas_call(
        functools.partial(_hgrn_kernel, chunk=chunk, n_chunks=rows // chunk, layer=layer, has_state=has_state,
                          heads=hp_a),
        out_shape=(jax.ShapeDtypeStruct((m, a_v), BF16),
                   jax.ShapeDtypeStruct((batch, a_heads, dk, dv_a), F32)),
        grid=(batch, a_heads // hp_a, nt),
        in_specs=in_specs,
        out_specs=(pl.BlockSpec((rows, wa), lambda b, h, t: (b * nt + t, h)),
                   pl.BlockSpec((1, hp_a, dk, dv_a), lambda b, h, t: (b, h, 0, 0))),
        scratch_shapes=[pltpu.VMEM((hp_a, dv_a, dk), F32)],
        compiler_params=_cparams(grid_sem, 48 << 20),
        name="hgrn2",
    )(*args)

    cosf, sinf = _rotary_tables(pos)
    log_gamma = jnp.log(1.0 - jnp.exp2(-5.0 - jnp.arange(b_heads, dtype=F32)))
    lg = jnp.broadcast_to(log_gamma[:, None, None], (b_heads, 1, dv_b))
    wv = hp_b * dv_b
    in_specs = [cols(off_b, HEAD_LANES, hp_b), cols(off_b + b_qk, HEAD_LANES, hp_b),
                cols(off_b + 2 * b_qk, dv_b, hp_b), cols(off_b + 2 * b_qk + b_v, dv_b, hp_b),
                pl.BlockSpec((rows, HEAD_LANES), lambda b, h, t: (t, 0)),
                pl.BlockSpec((rows, HEAD_LANES), lambda b, h, t: (t, 0)),
                pl.BlockSpec((hp_b, 1, dv_b), lambda b, h, t: (h, 0, 0)),
                pl.BlockSpec((1, wv), lambda b, h, t: (0, h))]
    args = [proj, proj, proj, proj, cosf, sinf, lg, g_b.reshape(1, b_v)]
    if has_state:
        in_specs.append(pl.BlockSpec((1, hp_b, HEAD_LANES, dv_b), lambda b, h, t: (b, h, 0, 0)))
        args.append(s_b)
    o_b, new_b = pl.pallas_call(
        functools.partial(_ret_kernel, chunk=chunk, n_chunks=rows // chunk, has_state=has_state, heads=hp_b,
                          dv=dv_b),
        out_shape=(jax.ShapeDtypeStruct((m, b_v), BF16),
                   jax.ShapeDtypeStruct((batch, b_heads, HEAD_LANES, dv_b), F32)),
        grid=(batch, b_heads // hp_b, nt),
        in_specs=in_specs,
        out_specs=(pl.BlockSpec((rows, wv), lambda b, h, t: (b * nt + t, h)),
                   pl.BlockSpec((1, hp_b, HEAD_LANES, dv_b), lambda b, h, t: (b, h, 0, 0))),
        scratch_shapes=[pltpu.VMEM((hp_b, HEAD_LANES, dv_b), F32)],
        compiler_params=_cparams(grid_sem, 48 << 20),
        name="retention",
    )(*args)
    return o_a, o_b, new_a, new_b


SUBLANES = 8


def _rglru_block(xb, yb, pos_base, rows, first_pos_zero, cw_ref, cb_ref, wa_ref, wx_ref, gab_ref, gxb_ref, lam_ref,
                 xpad_ref, h_ref, a_ref, b_ref):
    width = cw_ref.shape[0]
    lead = SUBLANES - (width - 1)
    xpad_ref[SUBLANES:SUBLANES + rows, :] = xb
    cw = cw_ref[...]
    acc = xpad_ref[lead:lead + rows, :] * cw[0:1]
    for j in range(1, width):
        acc = acc + xpad_ref[lead + j:lead + j + rows, :] * cw[j:j + 1]
    conv = cb_ref[...] + acc
    tail = xpad_ref[rows + lead:rows + SUBLANES, :]
    xpad_ref[lead:SUBLANES, :] = tail

    conv_bf = conv.astype(BF16)
    r = jax.nn.sigmoid(jnp.dot(conv_bf, wa_ref[0], preferred_element_type=F32) + gab_ref[...])
    i = jax.nn.sigmoid(jnp.dot(conv_bf, wx_ref[0], preferred_element_type=F32) + gxb_ref[...])
    log_a = -RG_C * r * jax.nn.softplus(-lam_ref[...])
    a = jnp.exp(log_a)
    mult = jnp.sqrt(1.0 - a * a)
    if first_pos_zero:
        row = lax.broadcasted_iota(jnp.int32, a.shape, 0)
        mult = jnp.where(row + pos_base == 0, 1.0, mult)
    bb = mult * (i * conv)

    grouped = (rows // SUBLANES, SUBLANES, a.shape[1])
    a3, b3 = a.reshape(grouped), bb.reshape(grouped)
    sub = lax.broadcasted_iota(jnp.int32, grouped, 1)
    shift = 1
    while shift < SUBLANES:
        take = sub >= shift
        b3 = jnp.where(take, a3 * pltpu.roll(b3, shift, 1) + b3, b3)
        a3 = jnp.where(take, a3 * pltpu.roll(a3, shift, 1), a3)
        shift *= 2
    a_ref[...] = a3.reshape(a.shape)
    b_ref[...] = b3.reshape(a.shape)

    def group(gidx, carry):
        r0 = pl.multiple_of(gidx * SUBLANES, SUBLANES)
        hg = a_ref[pl.ds(r0, SUBLANES), :] * carry + b_ref[pl.ds(r0, SUBLANES), :]
        b_ref[pl.ds(r0, SUBLANES), :] = hg
        return hg[SUBLANES - 1:SUBLANES, :]

    h_ref[...] = lax.fori_loop(0, rows // SUBLANES, group, h_ref[...], unroll=True)
    return jax.nn.gelu(yb, approximate=True) * b_ref[...], tail


def _griffin_kernel(*refs, rows, streams, first_pos_zero, has_state):
    if has_state:
        (xb_ref, yb_ref, cw_ref, cb_ref, wa_ref, wx_ref, gab_ref, gxb_ref, lam_ref, sconv_ref, h0_ref,
         o_ref, hlast_ref, nconv_ref, xpad_ref, h_ref, a_ref, b_ref) = refs
    else:
        (xb_ref, yb_ref, cw_ref, cb_ref, wa_ref, wx_ref, gab_ref, gxb_ref, lam_ref,
         o_ref, hlast_ref, nconv_ref, xpad_ref, h_ref, a_ref, b_ref) = refs
        sconv_ref = h0_ref = None
    t = pl.program_id(2)

    @pl.when(t == 0)
    def _():
        for q in range(streams):
            if has_state:
                xpad_ref[q, SUBLANES - sconv_ref.shape[1]:SUBLANES, :] = sconv_ref[q]
                h_ref[q] = h0_ref[q]
            else:
                xpad_ref[q, 0:SUBLANES, :] = jnp.zeros((SUBLANES, xpad_ref.shape[2]), F32)
                h_ref[q] = jnp.zeros(h_ref.shape[1:], F32)

    tails = []
    for q in range(streams):
        rs = slice(q * rows, (q + 1) * rows)
        o, tail = _rglru_block(xb_ref[rs, :], yb_ref[rs, :], t * rows, rows, first_pos_zero, cw_ref, cb_ref, wa_ref,
                               wx_ref, gab_ref, gxb_ref, lam_ref, xpad_ref.at[q], h_ref.at[q], a_ref.at[q],
                               b_ref.at[q])
        o_ref[rs, :] = o.astype(o_ref.dtype)
        tails.append(tail)

    @pl.when(t == pl.num_programs(2) - 1)
    def _():
        for q in range(streams):
            hlast_ref[q] = h_ref[q]
            nconv_ref[q] = tails[q]


def _pair_blockdiag(w, pair):
    n, c, _ = w.shape
    w = w.reshape(n // pair, pair, c, c)
    rows = []
    for p in range(pair):
        blocks = [w[:, p] if q == p else jnp.zeros_like(w[:, p]) for q in range(pair)]
        rows.append(jnp.concatenate(blocks, axis=-1))
    return jnp.concatenate(rows, axis=-2)


def _odd_mixer(xy, batch, seq, pos0, h0, s_conv, conv_w, conv_b, wa, wx, ga_b, gx_b, lam,
               rows_per_step=512, short_streams_per_step=8):
    m = xy.shape[0]
    d_rnn = conv_w.shape[1]
    width = conv_w.shape[0]
    blk = wa.shape[1]
    nblk = d_rnn // blk
    rows = min(rows_per_step, seq)
    assert seq % rows == 0 and rows % SUBLANES == 0 and rows >= width - 1
    nt = seq // rows
    streams = short_streams_per_step if (nt == 1 and batch % short_streams_per_step == 0) else 1
    has_state = h0 is not None

    def vec(a):
        return a.reshape(1, d_rnn)

    def vspec():
        return pl.BlockSpec((1, blk), lambda b, j, t: (0, j))

    def rows_spec(col0):
        return pl.BlockSpec((streams * rows, blk), lambda b, j, t: (b * nt + t, col0 + j))

    def state_spec(n):
        return pl.BlockSpec((streams, n, blk), lambda b, j, t: (b, 0, j))

    in_specs = [rows_spec(0), rows_spec(nblk),
                pl.BlockSpec((width, blk), lambda b, j, t: (0, j)),
                vspec(),
                pl.BlockSpec((1, blk, blk), lambda b, j, t: (j, 0, 0)),
                pl.BlockSpec((1, blk, blk), lambda b, j, t: (j, 0, 0)),
                vspec(), vspec(), vspec()]
    args = [xy, xy, conv_w, vec(conv_b), wa, wx, vec(ga_b), vec(gx_b), vec(lam)]
    if has_state:
        in_specs += [state_spec(width - 1), state_spec(1)]
        args += [s_conv, h0.reshape(batch, 1, d_rnn)]
    o, h_last, new_conv = pl.pallas_call(
        functools.partial(_griffin_kernel, rows=rows, streams=streams, first_pos_zero=(pos0 == 0),
                          has_state=has_state),
        out_shape=(jax.ShapeDtypeStruct((m, d_rnn), BF16),
                   jax.ShapeDtypeStruct((batch, 1, d_rnn), F32),
                   jax.ShapeDtypeStruct((batch, width - 1, d_rnn), F32)),
        grid=(batch // streams, nblk, nt),
        in_specs=in_specs,
        out_specs=(rows_spec(0), state_spec(1), state_spec(width - 1)),
        scratch_shapes=[pltpu.VMEM((streams, rows + SUBLANES, blk), F32),
                        pltpu.VMEM((streams, 1, blk), F32),
                        pltpu.VMEM((streams, rows, blk), F32),
                        pltpu.VMEM((streams, rows, blk), F32)],
        compiler_params=_cparams(("parallel", "parallel", "arbitrary"), 32 << 20),
        name="griffin",
    )(*args)
    return o, h_last.reshape(batch, d_rnn), new_conv


def _trunk(x, pos0, states, p):
    batch, seq, d = x.shape
    depth = p["norm_mix"].shape[0]
    pos = pos0 + jnp.arange(seq, dtype=jnp.int32)
    h = x.reshape(batch * seq, d)
    many_rows = h.shape[0] >= 8 * 1024

    def project(xin, w, layer, **kw):
        if many_rows:
            return _matmul_stationary(xin, w, layer, tn=(512 if w.shape[1] > 4096 else 1024), **kw)
        return _matmul(xin, w, layer, tn=512, tk=w.shape[1], **kw)

    n_hgrn, n_ret, n_rg, n_conv = [], [], [], []
    hn = _rms_norm(h, p["norm_mix"][0], BF16)
    for l in range(depth):
        if l % 2 == 0:
            e = l // 2
            proj = project(hn, p["w_in_even"], e)
            s_a, s_b = (None, None) if states is None else (states[0][e], states[1][e])
            o_a, o_b, sa, sb = _even_mixer(proj, batch, seq, pos, l, s_a, s_b, p["hgrn_shape"], p["ret_shape"],
                                           p["hgrn_lb_logits"], p["hgrn_norm"][e], p["ret_norm"][e])
            n_hgrn.append(sa)
            n_ret.append(sb)
            h = project((o_a, o_b), p["w_out_even"], e, mode="residual", res=h)
        else:
            od = l // 2
            xy = project(hn, p["w_in_odd"], od)
            h0, sc = (None, None) if states is None else (states[2][od], states[3][od])
            o, sh, scn = _odd_mixer(xy, batch, seq, pos0, h0, sc, p["conv_w"][od], p["conv_b"][od],
                                    p["gate_a_w"][od], p["gate_x_w"][od], p["gate_a_b"][od], p["gate_x_b"][od],
                                    p["rglru_lambda"][od])
            n_rg.append(sh)
            n_conv.append(scn)
            h = project(o, p["w_out_odd"], od, mode="residual", res=h)
        hm = _rms_norm(h, p["norm_mlp"][l], BF16)
        up = project(hm, p["w_up"], l, mode="relu2", out_dtype=BF16)
        if many_rows:
            n_slabs = p["w_down"].shape[1] // d
            for s in range(n_slabs):
                h = _matmul_stationary(up, p["w_down"], l, mode="residual", res=h, k_slab=(s, n_slabs))
        else:
            h = _matmul(up, p["w_down"], l, mode="residual", res=h, tn=512, tk=2048)
        if l + 1 < depth:
            hn = _rms_norm(h, p["norm_mix"][l + 1], BF16)
    out = _rms_norm(h, p["norm_final"], x.dtype).reshape(batch, seq, d)
    return out, jnp.stack(n_hgrn), jnp.stack(n_ret), jnp.stack(n_rg), jnp.stack(n_conv)


def kernel(x_prompt, x_sample, state_hgrn, state_ret, state_rglru, state_conv, norm_mix, norm_mlp, norm_final,
           w_in_even, hgrn_lb_logits, hgrn_norm, ret_norm, w_out_even, w_in_odd, conv_w, conv_b, gate_a_w,
           gate_a_b, gate_x_w, gate_x_b, rglru_lambda, w_out_odd, w_up, w_down):
    past_len = 4096

    p = dict(
        norm_mix=norm_mix, norm_mlp=norm_mlp, norm_final=norm_final,
        w_in_even=w_in_even, w_out_even=w_out_even, w_in_odd=w_in_odd, w_out_odd=w_out_odd, w_up=w_up,
        w_down=w_down,
        hgrn_lb_logits=hgrn_lb_logits, hgrn_norm=hgrn_norm, ret_norm=ret_norm,
        conv_w=conv_w, conv_b=conv_b,
        gate_a_w=jnp.stack([_pair_blockdiag(w, RG_PAIR) for w in gate_a_w.astype(BF16)]),
        gate_x_w=jnp.stack([_pair_blockdiag(w, RG_PAIR) for w in gate_x_w.astype(BF16)]),
        gate_a_b=gate_a_b, gate_x_b=gate_x_b, rglru_lambda=rglru_lambda,
        hgrn_shape=state_hgrn.shape[2:], ret_shape=state_ret.shape[2:],
    )
    y_p, hg_p, rt_p, rg_p, cv_p = _trunk(x_prompt, 0, None, p)
    y_s, hg_s, rt_s, rg_s, cv_s = _trunk(x_sample, past_len, (state_hgrn, state_ret, state_rglru, state_conv), p)
    return (y_p, y_s, hg_p, rt_p, rg_p, cv_p, hg_s, rt_s, rg_s, cv_s)
```

```python
import functools

import jax
import jax.numpy as jnp
from jax import lax
from jax.experimental import pallas as pl
from jax.experimental.pallas import tpu as pltpu

EPS = 1e-6
CHUNK = 64
ROPE_BASE = 10000.0
RG_C = 8.0
HEAD_LANES = 128
RG_PAIR = 2

V7X_VMEM_BYTES = 64 * 1024 * 1024
VMEM_CAP_BYTES = V7X_VMEM_BYTES - 6 * 1024 * 1024

F32 = jnp.float32
BF16 = jnp.bfloat16


def _cparams(semantics, vmem_bytes):
    return pltpu.CompilerParams(dimension_semantics=semantics,
                                vmem_limit_bytes=int(min(max(vmem_bytes, 32 * 1024 * 1024), VMEM_CAP_BYTES)))


def _rms_kernel(x_ref, g_ref, o_ref):
    x = x_ref[...]
    y = x * lax.rsqrt(jnp.mean(x * x, axis=-1, keepdims=True) + EPS)
    o_ref[...] = (y * g_ref[...]).astype(o_ref.dtype)


def _rms_norm(x, g, out_dtype, tm=512):
    m, d = x.shape
    tm = min(tm, m)
    io_bytes = 2 * tm * d * (4 + jnp.dtype(out_dtype).itemsize)
    return pl.pallas_call(
        _rms_kernel,
        out_shape=jax.ShapeDtypeStruct((m, d), out_dtype),
        grid=(m // tm,),
        in_specs=[pl.BlockSpec((tm, d), lambda i: (i, 0)),
                  pl.BlockSpec((1, d), lambda i: (0, 0))],
        out_specs=pl.BlockSpec((tm, d), lambda i: (i, 0)),
        compiler_params=_cparams(("parallel",), io_bytes + 3 * tm * d * 4),
        name="rms_norm",
    )(x, g.reshape(1, d))


def _epilogue(acc, mode, res_ref, o_ref):
    if mode == "relu2":
        acc = jnp.square(jnp.maximum(acc, 0.0))
    elif mode == "residual":
        acc = res_ref[...] + acc
    o_ref[...] = acc.astype(o_ref.dtype)


def _split_dot(x_refs, k_sizes, w_rows):
    acc, off = None, 0
    for x_ref, ks in zip(x_refs, k_sizes):
        part = jnp.dot(x_ref[...], w_rows(off, ks), preferred_element_type=F32)
        acc = part if acc is None else acc + part
        off += ks
    return acc


def _mm_kernel_single(*refs, mode, k_sizes):
    n_x = len(k_sizes)
    x_refs, w_ref = refs[:n_x], refs[n_x]
    if mode == "residual":
        res_ref, o_ref = refs[n_x + 1:]
    else:
        (o_ref,), res_ref = refs[n_x + 1:], None
    acc = _split_dot(x_refs, k_sizes, lambda off, ks: w_ref[off:off + ks, :].astype(BF16))
    _epilogue(acc, mode, res_ref, o_ref)


def _mm_kernel_multi(*refs, mode):
    if mode == "residual":
        x_ref, w_ref, res_ref, o_ref = refs
    else:
        x_ref, w_ref, o_ref, acc_ref = refs
    k = pl.program_id(2)
    part = jnp.dot(x_ref[...], w_ref[...].astype(BF16), preferred_element_type=F32)

    if mode == "residual":
        @pl.when(k == 0)
        def _():
            o_ref[...] = res_ref[...] + part

        @pl.when(k > 0)
        def _():
            o_ref[...] += part
    else:
        @pl.when(k == 0)
        def _():
            acc_ref[...] = part

        @pl.when(k > 0)
        def _():
            acc_ref[...] += part

        @pl.when(k == pl.num_programs(2) - 1)
        def _():
            _epilogue(acc_ref[...], mode, None, o_ref)


def _weight_spec(layer, block, index_map):
    return pl.BlockSpec((None,) + block, lambda *g: (layer,) + tuple(index_map(*g)))


def _matmul(x, w, layer, *, mode="none", res=None, out_dtype=F32, tm=1024, tn=1024, tk=4096):
    xs = x if isinstance(x, (tuple, list)) else (x,)
    m = xs[0].shape[0]
    k_sizes = tuple(xi.shape[1] for xi in xs)
    _, kdim, n = w.shape
    assert sum(k_sizes) == kdim
    tm, tn, tk = min(tm, m), min(tn, n), min(tk, kdim)
    nk = kdim // tk
    assert m % tm == 0 and n % tn == 0 and kdim % tk == 0
    out_bytes = jnp.dtype(out_dtype).itemsize
    w_bytes = jnp.dtype(w.dtype).itemsize
    vmem = 2 * (tm * tk * 2 + tk * tn * w_bytes + tm * tn * out_bytes) + tm * tn * 4 + (4 << 20)
    if w_bytes != 2:
        vmem += tk * tn * 2
    if nk == 1:
        grid = (m // tm, n // tn)
        in_specs = [pl.BlockSpec((tm, ks), lambda i, j: (i, 0)) for ks in k_sizes]
        in_specs.append(_weight_spec(layer, (tk, tn), lambda i, j: (0, j)))
        res_spec = pl.BlockSpec((tm, tn), lambda i, j: (i, j))
        out_spec = pl.BlockSpec((tm, tn), lambda i, j: (i, j))
        body = functools.partial(_mm_kernel_single, mode=mode, k_sizes=k_sizes)
        scratch, sem = [], ("parallel", "parallel")
    else:
        assert len(xs) == 1
        grid = (m // tm, n // tn, nk)
        in_specs = [pl.BlockSpec((tm, tk), lambda i, j, k: (i, k)),
                    _weight_spec(layer, (tk, tn), lambda i, j, k: (k, j))]
        res_spec = pl.BlockSpec((tm, tn), lambda i, j, k: (i, j))
        out_spec = pl.BlockSpec((tm, tn), lambda i, j, k: (i, j))
        body = functools.partial(_mm_kernel_multi, mode=mode)
        sem = ("parallel", "parallel", "arbitrary")
        if mode == "residual":
            assert out_dtype == F32
            scratch = []
            vmem += tm * tn * 4
        else:
            scratch = [pltpu.VMEM((tm, tn), F32)]
            vmem += tm * tn * 4
    args = list(xs) + [w]
    if mode == "residual":
        in_specs.append(res_spec)
        args.append(res)
        vmem += 2 * tm * tn * 4
    return pl.pallas_call(
        body,
        out_shape=jax.ShapeDtypeStruct((m, n), out_dtype),
        grid=grid,
        in_specs=in_specs,
        out_specs=out_spec,
        scratch_shapes=scratch,
        compiler_params=_cparams(sem, vmem),
        name="proj_" + mode,
    )(*args)


def _mm_kernel_stationary(*refs, mode, k_sizes, chunk_rows):
    n_x = len(k_sizes)
    x_refs, wchunk_ref = refs[:n_x], refs[n_x]
    if mode == "residual":
        res_ref, o_ref, wbf_ref = refs[n_x + 1:]
    else:
        (o_ref, wbf_ref), res_ref = refs[n_x + 1:], None
    j, i = pl.program_id(0), pl.program_id(1)
    last = pl.num_programs(0) - 1

    def stage(slot):
        r0 = pl.multiple_of(i * chunk_rows, chunk_rows)
        wbf_ref[slot, pl.ds(r0, chunk_rows), :] = wchunk_ref[...].astype(BF16)

    def multiply(slot):
        acc = _split_dot(x_refs, k_sizes, lambda off, ks: wbf_ref[slot, off:off + ks, :])
        _epilogue(acc, mode, res_ref, o_ref)

    @pl.when(j == 0)
    def _():
        stage(0)

    for parity in range(2):
        fill, read = parity, 1 - parity
        same_parity = j % 2 == parity

        @pl.when(jnp.logical_and(same_parity, jnp.logical_and(j > 0, j < last)))
        def _():
            stage(fill)
            multiply(read)

        @pl.when(jnp.logical_and(same_parity, jnp.logical_and(j > 0, j == last)))
        def _():
            multiply(read)


def _matmul_stationary(x, w, layer, *, mode="none", res=None, out_dtype=F32, tm=1024, tn=1024, k_slab=None):
    xs = x if isinstance(x, (tuple, list)) else (x,)
    m = xs[0].shape[0]
    slab, n_slabs = (0, 1) if k_slab is None else k_slab
    assert len(xs) == 1 or n_slabs == 1
    k_sizes = tuple(xi.shape[1] // n_slabs for xi in xs)
    _, k_total, n = w.shape
    kdim = k_total // n_slabs
    assert sum(k_sizes) == kdim and m % tm == 0 and n % tn == 0
    n_i, n_j = m // tm, n // tn
    assert kdim % n_i == 0
    chunk_rows = kdim // n_i
    assert chunk_rows % 16 == 0
    out_bytes = jnp.dtype(out_dtype).itemsize

    def row_tile(j, i):
        return jnp.where(j == 0, 0, i)

    in_specs = [pl.BlockSpec((tm, ks), lambda j, i: (row_tile(j, i), slab)) for ks in k_sizes]
    in_specs.append(_weight_spec(layer, (chunk_rows, tn), lambda j, i: (slab * n_i + i, jnp.minimum(j, n_j - 1))))
    io_spec = pl.BlockSpec((tm, tn), lambda j, i: (row_tile(j, i), jnp.maximum(j - 1, 0)))
    args = list(xs) + [w]
    vmem = (2 * (tm * kdim * 2 + chunk_rows * tn * 4 + tm * tn * out_bytes) + 2 * kdim * tn * 2
            + tm * tn * 4 + (4 << 20))
    if mode == "residual":
        in_specs.append(io_spec)
        args.append(res)
        vmem += 2 * tm * tn * 4
    return pl.pallas_call(
        functools.partial(_mm_kernel_stationary, mode=mode, k_sizes=k_sizes, chunk_rows=chunk_rows),
        out_shape=jax.ShapeDtypeStruct((m, n), out_dtype),
        grid=(n_j + 1, n_i),
        in_specs=in_specs,
        out_specs=io_spec,
        scratch_shapes=[pltpu.VMEM((2, kdim, tn), BF16)],
        compiler_params=_cparams(("arbitrary", "arbitrary"), vmem),
        name="projws_" + mode,
    )(*args)


def _chunk_pos(shape, chunk):
    return lax.broadcasted_iota(jnp.int32, shape, 0) % chunk


def _cumsum_in_chunks(x, chunk):
    pos = _chunk_pos(x.shape, chunk)
    shift = 1
    while shift < chunk:
        x = x + jnp.where(pos >= shift, pltpu.roll(x, shift, 0), 0.0)
        shift *= 2
    return x


def _causal_mask(chunk):
    r = lax.broadcasted_iota(jnp.int32, (chunk, chunk), 0)
    c = lax.broadcasted_iota(jnp.int32, (chunk, chunk), 1)
    return r >= c


_NT = (((1,), (1,)), ((), ()))
_TN = (((0,), (0,)), ((), ()))


def _lower_bound(logits, layer):
    rows = [logits[i:i + 1] for i in range(logits.shape[0])]
    mx = functools.reduce(jnp.maximum, rows)
    es = [jnp.exp(r - mx) for r in rows]
    den = functools.reduce(lambda a, b: a + b, es)
    lb = es[0] / den
    for i in range(1, layer + 1):
        lb = lb + es[i] / den
    return lb


def _hgrn_kernel(*refs, chunk, n_chunks, layer, has_state, heads):
    if has_state:
        q_ref, f_ref, v_ref, g_ref, lbl_ref, gn_ref, s0_ref, o_ref, sout_ref, st_ref = refs
    else:
        (q_ref, f_ref, v_ref, g_ref, lbl_ref, gn_ref, o_ref, sout_ref, st_ref), s0_ref = refs, None
    t = pl.program_id(2)
    last = t == pl.num_programs(2) - 1
    mask = _causal_mask(chunk)

    @pl.when(t == 0)
    def _():
        for hh in range(heads):
            if has_state:
                st_ref[hh] = s0_ref[0, hh].T
            else:
                st_ref[hh] = jnp.zeros(st_ref.shape[1:], F32)

    finals = []
    for hh in range(heads):
        ls = slice(hh * HEAD_LANES, (hh + 1) * HEAD_LANES)
        lb = _lower_bound(lbl_ref[:, ls], layer)
        f = lb + (1.0 - lb) * jax.nn.sigmoid(f_ref[:, ls])
        q = jax.nn.silu(q_ref[:, ls])
        k = 1.0 - f
        b = _cumsum_in_chunks(jnp.log(f), chunk)
        q_in = (q * jnp.exp(b)).astype(BF16)
        k_in = (k * jnp.exp(-b)).astype(BF16)
        v = v_ref[:, ls].astype(BF16)
        st = st_ref[hh]
        outs = []
        for c in range(n_chunks):
            sl = slice(c * chunk, (c + 1) * chunk)
            b_c = b[sl]
            b_last = b_c[chunk - 1:chunk]
            k_out = (k[sl] * jnp.exp(b_last - b_c)).astype(BF16)
            attn = lax.dot_general(q_in[sl], k_in[sl], _NT, preferred_element_type=F32)
            attn = jnp.where(mask, attn, 0.0).astype(BF16)
            o_intra = jnp.dot(attn, v[sl], preferred_element_type=F32)
            o_inter = lax.dot_general(q_in[sl], st.astype(BF16), _NT, preferred_element_type=F32)
            kv_t = lax.dot_general(v[sl], k_out, _TN, preferred_element_type=F32)
            st = jnp.exp(b_last) * st + kv_t
            outs.append(o_intra + o_inter)
        st_ref[hh] = st
        o = outs[0] if n_chunks == 1 else jnp.concatenate(outs, axis=0)
        o = o * lax.rsqrt(jnp.mean(o * o, axis=-1, keepdims=True) + EPS)
        o = o * gn_ref[:, ls] * jax.nn.silu(g_ref[:, ls])
        o_ref[:, ls] = o.astype(o_ref.dtype)
        finals.append(st)

    @pl.when(last)
    def _():
        for hh in range(heads):
            sout_ref[0, hh] = finals[hh].T


def _ret_kernel(*refs, chunk, n_chunks, has_state, heads, dv):
    if has_state:
        q_ref, k_ref, v_ref, g_ref, cos_ref, sin_ref, lg_ref, gn_ref, s0_ref, o_ref, sout_ref, st_ref = refs
    else:
        (q_ref, k_ref, v_ref, g_ref, cos_ref, sin_ref, lg_ref, gn_ref, o_ref, sout_ref, st_ref), s0_ref = refs, None
    t = pl.program_id(2)
    last = t == pl.num_programs(2) - 1
    cosf, sinf = cos_ref[...], sin_ref[...]
    half = HEAD_LANES // 2
    mask = _causal_mask(chunk)
    pos1 = (_chunk_pos(cosf.shape, chunk) + 1).astype(F32)

    @pl.when(t == 0)
    def _():
        for hh in range(heads):
            if has_state:
                st_ref[hh] = s0_ref[0, hh]
            else:
                st_ref[hh] = jnp.zeros(st_ref.shape[1:], F32)

    finals = []
    for hh in range(heads):
        ls = slice(hh * HEAD_LANES, (hh + 1) * HEAD_LANES)
        ws = slice(hh * dv, (hh + 1) * dv)
        q, k = q_ref[:, ls], k_ref[:, ls]
        qr = q * cosf + pltpu.roll(q, half, 1) * sinf
        kr = (k * cosf + pltpu.roll(k, half, 1) * sinf) * (HEAD_LANES ** -0.5)
        lg_wide = lg_ref[hh]
        lg = lg_wide[:, :HEAD_LANES]
        b = pos1 * lg
        b_last = float(chunk) * lg
        q_in = (qr * jnp.exp(b)).astype(BF16)
        k_in = (kr * jnp.exp(-b)).astype(BF16)
        k_out = (kr * jnp.exp(b_last - b)).astype(BF16)
        decay = jnp.exp(float(chunk) * lg_wide)
        v = v_ref[:, ws].astype(BF16)
        st = st_ref[hh]
        outs = []
        for c in range(n_chunks):
            sl = slice(c * chunk, (c + 1) * chunk)
            attn = lax.dot_general(q_in[sl], k_in[sl], _NT, preferred_element_type=F32)
            attn = jnp.where(mask, attn, 0.0).astype(BF16)
            o_intra = jnp.dot(attn, v[sl], preferred_element_type=F32)
            o_inter = jnp.dot(q_in[sl], st.astype(BF16), preferred_element_type=F32)
            kv = lax.dot_general(k_out[sl], v[sl], _TN, preferred_element_type=F32)
            st = decay * st + kv
            outs.append(o_intra + o_inter)
        st_ref[hh] = st
        o = outs[0] if n_chunks == 1 else jnp.concatenate(outs, axis=0)
        c0 = o - jnp.mean(o, axis=-1, keepdims=True)
        o = c0 * lax.rsqrt(jnp.mean(c0 * c0, axis=-1, keepdims=True) + EPS)
        o = o * gn_ref[:, ws] * jax.nn.silu(g_ref[:, ws])
        o_ref[:, ws] = o.astype(o_ref.dtype)
        finals.append(st)

    @pl.when(last)
    def _():
        for hh in range(heads):
            sout_ref[0, hh] = finals[hh]


def _rotary_tables(pos):
    half = HEAD_LANES // 2
    inv = 1.0 / (ROPE_BASE ** jnp.linspace(0.0, 1.0, half, dtype=F32))
    ang = pos.astype(F32)[:, None] * inv[None, :]
    cos, sin = jnp.cos(ang), jnp.sin(ang)
    return jnp.concatenate([cos, cos], axis=-1), jnp.concatenate([-sin, sin], axis=-1)


def _even_mixer(proj, batch, seq, pos, layer, s_a, s_b, a_shape, b_shape, lb_logits, g_a, g_b,
                rows_per_step=1024, heads_per_step=4):
    m = proj.shape[0]
    a_heads, dk, dv_a = a_shape
    b_heads, dk_b, dv_b = b_shape
    assert dk == HEAD_LANES and dv_a == HEAD_LANES and dk_b == HEAD_LANES
    chunk = min(CHUNK, seq)
    rows = min(rows_per_step, seq)
    assert seq % rows == 0 and rows % chunk == 0
    nt = seq // rows
    has_state = s_a is not None
    hp_a = a_heads if nt == 1 else heads_per_step
    hp_b = b_heads if nt == 1 else heads_per_step
    assert a_heads % hp_a == 0 and b_heads % hp_b == 0
    a_w, a_v = a_heads * dk, a_heads * dv_a
    b_qk, b_v = b_heads * HEAD_LANES, b_heads * dv_b
    off_b = 2 * a_w + 2 * a_v
    grid_sem = ("parallel", "parallel", "arbitrary")

    def cols(offset, width, hp):
        blk = hp * width
        assert offset % blk == 0
        return pl.BlockSpec((rows, blk), lambda b, h, t: (b * nt + t, offset // blk + h))

    n_layers = lb_logits.shape[0]
    wa = hp_a * HEAD_LANES
    in_specs = [cols(0, dk, hp_a), cols(a_w, dk, hp_a), cols(2 * a_w, dv_a, hp_a), cols(2 * a_w + a_v, dv_a, hp_a),
                pl.BlockSpec((n_layers, wa), lambda b, h, t: (0, h)),
                pl.BlockSpec((1, wa), lambda b, h, t: (0, h))]
    args = [proj, proj, proj, proj, lb_logits, g_a.reshape(1, a_v)]
    if has_state:
        in_specs.append(pl.BlockSpec((1, hp_a, dk, dv_a), lambda b, h, t: (b, h, 0, 0)))
        args.append(s_a)
    o_a, new_a = pl.pallas_call(
        functools.partial(_hgrn_kernel, chunk=chunk, n_chunks=rows // chunk, layer=layer, has_state=has_state,
                          heads=hp_a),
        out_shape=(jax.ShapeDtypeStruct((m, a_v), BF16),
                   jax.ShapeDtypeStruct((batch, a_heads, dk, dv_a), F32)),
        grid=(batch, a_heads // hp_a, nt),
        in_specs=in_specs,
        out_specs=(pl.BlockSpec((rows, wa), lambda b, h, t: (b * nt + t, h)),
                   pl.BlockSpec((1, hp_a, dk, dv_a), lambda b, h, t: (b, h, 0, 0))),
        scratch_shapes=[pltpu.VMEM((hp_a, dv_a, dk), F32)],
        compiler_params=_cparams(grid_sem, 48 << 20),
        name="hgrn2",
    )(*args)

    cosf, sinf = _rotary_tables(pos)
    log_gamma = jnp.log(1.0 - jnp.exp2(-5.0 - jnp.arange(b_heads, dtype=F32)))
    lg = jnp.broadcast_to(log_gamma[:, None, None], (b_heads, 1, dv_b))
    wv = hp_b * dv_b
    in_specs = [cols(off_b, HEAD_LANES, hp_b), cols(off_b + b_qk, HEAD_LANES, hp_b),
                cols(off_b + 2 * b_qk, dv_b, hp_b), cols(off_b + 2 * b_qk + b_v, dv_b, hp_b),
                pl.BlockSpec((rows, HEAD_LANES), lambda b, h, t: (t, 0)),
                pl.BlockSpec((rows, HEAD_LANES), lambda b, h, t: (t, 0)),
                pl.BlockSpec((hp_b, 1, dv_b), lambda b, h, t: (h, 0, 0)),
                pl.BlockSpec((1, wv), lambda b, h, t: (0, h))]
    args = [proj, proj, proj, proj, cosf, sinf, lg, g_b.reshape(1, b_v)]
    if has_state:
        in_specs.append(pl.BlockSpec((1, hp_b, HEAD_LANES, dv_b), lambda b, h, t: (b, h, 0, 0)))
        args.append(s_b)
    o_b, new_b = pl.pallas_call(
        functools.partial(_ret_kernel, chunk=chunk, n_chunks=rows // chunk, has_state=has_state, heads=hp_b,
                          dv=dv_b),
        out_shape=(jax.ShapeDtypeStruct((m, b_v), BF16),
                   jax.ShapeDtypeStruct((batch, b_heads, HEAD_LANES, dv_b), F32)),
        grid=(batch, b_heads // hp_b, nt),
        in_specs=in_specs,
        out_specs=(pl.BlockSpec((rows, wv), lambda b, h, t: (b * nt + t, h)),
                   pl.BlockSpec((1, hp_b, HEAD_LANES, dv_b), lambda b, h, t: (b, h, 0, 0))),
        scratch_shapes=[pltpu.VMEM((hp_b, HEAD_LANES, dv_b), F32)],
        compiler_params=_cparams(grid_sem, 48 << 20),
        name="retention",
    )(*args)
    return o_a, o_b, new_a, new_b


SUBLANES = 8


def _rglru_block(xb, yb, pos_base, rows, first_pos_zero, cw_ref, cb_ref, wa_ref, wx_ref, gab_ref, gxb_ref, lam_ref,
                 xpad_ref, h_ref, a_ref, b_ref):
    width = cw_ref.shape[0]
    lead = SUBLANES - (width - 1)
    xpad_ref[SUBLANES:SUBLANES + rows, :] = xb
    cw = cw_ref[...]
    acc = xpad_ref[lead:lead + rows, :] * cw[0:1]
    for j in range(1, width):
        acc = acc + xpad_ref[lead + j:lead + j + rows, :] * cw[j:j + 1]
    conv = cb_ref[...] + acc
    tail = xpad_ref[rows + lead:rows + SUBLANES, :]
    xpad_ref[lead:SUBLANES, :] = tail

    conv_bf = conv.astype(BF16)
    r = jax.nn.sigmoid(jnp.dot(conv_bf, wa_ref[0], preferred_element_type=F32) + gab_ref[...])
    i = jax.nn.sigmoid(jnp.dot(conv_bf, wx_ref[0], preferred_element_type=F32) + gxb_ref[...])
    log_a = -RG_C * r * jax.nn.softplus(-lam_ref[...])
    a = jnp.exp(log_a)
    mult = jnp.sqrt(1.0 - a * a)
    if first_pos_zero:
        row = lax.broadcasted_iota(jnp.int32, a.shape, 0)
        mult = jnp.where(row + pos_base == 0, 1.0, mult)
    bb = mult * (i * conv)

    grouped = (rows // SUBLANES, SUBLANES, a.shape[1])
    a3, b3 = a.reshape(grouped), bb.reshape(grouped)
    sub = lax.broadcasted_iota(jnp.int32, grouped, 1)
    shift = 1
    while shift < SUBLANES:
        take = sub >= shift
        b3 = jnp.where(take, a3 * pltpu.roll(b3, shift, 1) + b3, b3)
        a3 = jnp.where(take, a3 * pltpu.roll(a3, shift, 1), a3)
        shift *= 2
    a_ref[...] = a3.reshape(a.shape)
    b_ref[...] = b3.reshape(a.shape)

    def group(gidx, carry):
        r0 = pl.multiple_of(gidx * SUBLANES, SUBLANES)
        hg = a_ref[pl.ds(r0, SUBLANES), :] * carry + b_ref[pl.ds(r0, SUBLANES), :]
        b_ref[pl.ds(r0, SUBLANES), :] = hg
        return hg[SUBLANES - 1:SUBLANES, :]

    h_ref[...] = lax.fori_loop(0, rows // SUBLANES, group, h_ref[...], unroll=True)
    return jax.nn.gelu(yb, approximate=True) * b_ref[...], tail


def _griffin_kernel(*refs, rows, streams, first_pos_zero, has_state):
    if has_state:
        (xb_ref, yb_ref, cw_ref, cb_ref, wa_ref, wx_ref, gab_ref, gxb_ref, lam_ref, sconv_ref, h0_ref,
         o_ref, hlast_ref, nconv_ref, xpad_ref, h_ref, a_ref, b_ref) = refs
    else:
        (xb_ref, yb_ref, cw_ref, cb_ref, wa_ref, wx_ref, gab_ref, gxb_ref, lam_ref,
         o_ref, hlast_ref, nconv_ref, xpad_ref, h_ref, a_ref, b_ref) = refs
        sconv_ref = h0_ref = None
    t = pl.program_id(2)

    @pl.when(t == 0)
    def _():
        for q in range(streams):
            if has_state:
                xpad_ref[q, SUBLANES - sconv_ref.shape[1]:SUBLANES, :] = sconv_ref[q]
                h_ref[q] = h0_ref[q]
            else:
                xpad_ref[q, 0:SUBLANES, :] = jnp.zeros((SUBLANES, xpad_ref.shape[2]), F32)
                h_ref[q] = jnp.zeros(h_ref.shape[1:], F32)

    tails = []
    for q in range(streams):
        rs = slice(q * rows, (q + 1) * rows)
        o, tail = _rglru_block(xb_ref[rs, :], yb_ref[rs, :], t * rows, rows, first_pos_zero, cw_ref, cb_ref, wa_ref,
                               wx_ref, gab_ref, gxb_ref, lam_ref, xpad_ref.at[q], h_ref.at[q], a_ref.at[q],
                               b_ref.at[q])
        o_ref[rs, :] = o.astype(o_ref.dtype)
        tails.append(tail)

    @pl.when(t == pl.num_programs(2) - 1)
    def _():
        for q in range(streams):
            hlast_ref[q] = h_ref[q]
            nconv_ref[q] = tails[q]


def _pair_blockdiag(w, pair):
    n, c, _ = w.shape
    w = w.reshape(n // pair, pair, c, c)
    rows = []
    for p in range(pair):
        blocks = [w[:, p] if q == p else jnp.zeros_like(w[:, p]) for q in range(pair)]
        rows.append(jnp.concatenate(blocks, axis=-1))
    return jnp.concatenate(rows, axis=-2)


def _odd_mixer(xy, batch, seq, pos0, h0, s_conv, conv_w, conv_b, wa, wx, ga_b, gx_b, lam,
               rows_per_step=512, short_streams_per_step=8):
    m = xy.shape[0]
    d_rnn = conv_w.shape[1]
    width = conv_w.shape[0]
    blk = wa.shape[1]
    nblk = d_rnn // blk
    rows = min(rows_per_step, seq)
    assert seq % rows == 0 and rows % SUBLANES == 0 and rows >= width - 1
    nt = seq // rows
    streams = short_streams_per_step if (nt == 1 and batch % short_streams_per_step == 0) else 1
    has_state = h0 is not None

    def vec(a):
        return a.reshape(1, d_rnn)

    def vspec():
        return pl.BlockSpec((1, blk), lambda b, j, t: (0, j))

    def rows_spec(col0):
        return pl.BlockSpec((streams * rows, blk), lambda b, j, t: (b * nt + t, col0 + j))

    def state_spec(n):
        return pl.BlockSpec((streams, n, blk), lambda b, j, t: (b, 0, j))

    in_specs = [rows_spec(0), rows_spec(nblk),
                pl.BlockSpec((width, blk), lambda b, j, t: (0, j)),
                vspec(),
                pl.BlockSpec((1, blk, blk), lambda b, j, t: (j, 0, 0)),
                pl.BlockSpec((1, blk, blk), lambda b, j, t: (j, 0, 0)),
                vspec(), vspec(), vspec()]
    args = [xy, xy, conv_w, vec(conv_b), wa, wx, vec(ga_b), vec(gx_b), vec(lam)]
    if has_state:
        in_specs += [state_spec(width - 1), state_spec(1)]
        args += [s_conv, h0.reshape(batch, 1, d_rnn)]
    o, h_last, new_conv = pl.pallas_call(
        functools.partial(_griffin_kernel, rows=rows, streams=streams, first_pos_zero=(pos0 == 0),
                          has_state=has_state),
        out_shape=(jax.ShapeDtypeStruct((m, d_rnn), BF16),
                   jax.ShapeDtypeStruct((batch, 1, d_rnn), F32),
                   jax.ShapeDtypeStruct((batch, width - 1, d_rnn), F32)),
        grid=(batch // streams, nblk, nt),
        in_specs=in_specs,
        out_specs=(rows_spec(0), state_spec(1), state_spec(width - 1)),
        scratch_shapes=[pltpu.VMEM((streams, rows + SUBLANES, blk), F32),
                        pltpu.VMEM((streams, 1, blk), F32),
                        pltpu.VMEM((streams, rows, blk), F32),
                        pltpu.VMEM((streams, rows, blk), F32)],
        compiler_params=_cparams(("parallel", "parallel", "arbitrary"), 32 << 20),
        name="griffin",
    )(*args)
    return o, h_last.reshape(batch, d_rnn), new_conv


def _trunk(x, pos0, states, p):
    batch, seq, d = x.shape
    depth = p["norm_mix"].shape[0]
    pos = pos0 + jnp.arange(seq, dtype=jnp.int32)
    h = x.reshape(batch * seq, d)
    many_rows = h.shape[0] >= 8 * 1024

    def project(xin, w, layer, **kw):
        if many_rows:
            return _matmul_stationary(xin, w, layer, tn=(512 if w.shape[1] > 4096 else 1024), **kw)
        return _matmul(xin, w, layer, tn=512, tk=w.shape[1], **kw)

    n_hgrn, n_ret, n_rg, n_conv = [], [], [], []
    hn = _rms_norm(h, p["norm_mix"][0], BF16)
    for l in range(depth):
        if l % 2 == 0:
            e = l // 2
            proj = project(hn, p["w_in_even"], e)
            s_a, s_b = (None, None) if states is None else (states[0][e], states[1][e])
            o_a, o_b, sa, sb = _even_mixer(proj, batch, seq, pos, l, s_a, s_b, p["hgrn_shape"], p["ret_shape"],
                                           p["hgrn_lb_logits"], p["hgrn_norm"][e], p["ret_norm"][e])
            n_hgrn.append(sa)
            n_ret.append(sb)
            h = project((o_a, o_b), p["w_out_even"], e, mode="residual", res=h)
        else:
            od = l // 2
            xy = project(hn, p["w_in_odd"], od)
            h0, sc = (None, None) if states is None else (states[2][od], states[3][od])
            o, sh, scn = _odd_mixer(xy, batch, seq, pos0, h0, sc, p["conv_w"][od], p["conv_b"][od],
                                    p["gate_a_w"][od], p["gate_x_w"][od], p["gate_a_b"][od], p["gate_x_b"][od],
                                    p["rglru_lambda"][od])
            n_rg.append(sh)
            n_conv.append(scn)
            h = project(o, p["w_out_odd"], od, mode="residual", res=h)
        hm = _rms_norm(h, p["norm_mlp"][l], BF16)
        up = project(hm, p["w_up"], l, mode="relu2", out_dtype=BF16)
        if many_rows:
            n_slabs = p["w_down"].shape[1] // d
            for s in range(n_slabs):
                h = _matmul_stationary(up, p["w_down"], l, mode="residual", res=h, k_slab=(s, n_slabs))
        else:
            h = _matmul(up, p["w_down"], l, mode="residual", res=h, tn=512, tk=4096)
        if l + 1 < depth:
            hn = _rms_norm(h, p["norm_mix"][l + 1], BF16)
    out = _rms_norm(h, p["norm_final"], x.dtype).reshape(batch, seq, d)
    return out, jnp.stack(n_hgrn), jnp.stack(n_ret), jnp.stack(n_rg), jnp.stack(n_conv)


def kernel(x_prompt, x_sample, state_hgrn, state_ret, state_rglru, state_conv, norm_mix, norm_mlp, norm_final,
           w_in_even, hgrn_lb_logits, hgrn_norm, ret_norm, w_out_even, w_in_odd, conv_w, conv_b, gate_a_w,
           gate_a_b, gate_x_w, gate_x_b, rglru_lambda, w_out_odd, w_up, w_down):
    past_len = 4096

    p = dict(
        norm_mix=norm_mix, norm_mlp=norm_mlp, norm_final=norm_final,
        w_in_even=w_in_even, w_out_even=w_out_even, w_in_odd=w_in_odd, w_out_odd=w_out_odd, w_up=w_up,
        w_down=w_down,
        hgrn_lb_logits=hgrn_lb_logits, hgrn_norm=hgrn_norm, ret_norm=ret_norm,
        conv_w=conv_w, conv_b=conv_b,
        gate_a_w=jnp.stack([_pair_blockdiag(w, RG_PAIR) for w in gate_a_w.astype(BF16)]),
        gate_x_w=jnp.stack([_pair_blockdiag(w, RG_PAIR) for w in gate_x_w.astype(BF16)]),
        gate_a_b=gate_a_b, gate_x_b=gate_x_b, rglru_lambda=rglru_lambda,
        hgrn_shape=state_hgrn.shape[2:], ret_shape=state_ret.shape[2:],
    )
    y_p, hg_p, rt_p, rg_p, cv_p = _trunk(x_prompt, 0, None, p)
    y_s, hg_s, rt_s, rg_s, cv_s = _trunk(x_sample, past_len, (state_hgrn, state_ret, state_rglru, state_conv), p)
    return (y_p, y_s, hg_p, rt_p, rg_p, cv_p, hg_s, rt_s, rg_s, cv_s)
```

```python
import functools

import jax
import jax.numpy as jnp
from jax import lax
from jax.experimental import pallas as pl
from jax.experimental.pallas import tpu as pltpu

EPS = 1e-6
CHUNK = 64
ROPE_BASE = 10000.0
RG_C = 8.0
HEAD_LANES = 128
RG_PAIR = 2

V7X_VMEM_BYTES = 64 * 1024 * 1024
VMEM_CAP_BYTES = V7X_VMEM_BYTES - 6 * 1024 * 1024

F32 = jnp.float32
BF16 = jnp.bfloat16


def _cparams(semantics, vmem_bytes):
    return pltpu.CompilerParams(dimension_semantics=semantics,
                                vmem_limit_bytes=int(min(max(vmem_bytes, 32 * 1024 * 1024), VMEM_CAP_BYTES)))


def _rms_kernel(x_ref, g_ref, o_ref):
    x = x_ref[...]
    y = x * lax.rsqrt(jnp.mean(x * x, axis=-1, keepdims=True) + EPS)
    o_ref[...] = (y * g_ref[...]).astype(o_ref.dtype)


def _rms_norm(x, g, out_dtype, tm=512):
    m, d = x.shape
    tm = min(tm, m)
    io_bytes = 2 * tm * d * (4 + jnp.dtype(out_dtype).itemsize)
    return pl.pallas_call(
        _rms_kernel,
        out_shape=jax.ShapeDtypeStruct((m, d), out_dtype),
        grid=(m // tm,),
        in_specs=[pl.BlockSpec((tm, d), lambda i: (i, 0)),
                  pl.BlockSpec((1, d), lambda i: (0, 0))],
        out_specs=pl.BlockSpec((tm, d), lambda i: (i, 0)),
        compiler_params=_cparams(("parallel",), io_bytes + 3 * tm * d * 4),
        name="rms_norm",
    )(x, g.reshape(1, d))


def _epilogue(acc, mode, res_ref, o_ref):
    if mode == "relu2":
        acc = jnp.square(jnp.maximum(acc, 0.0))
    elif mode == "residual":
        acc = res_ref[...] + acc
    o_ref[...] = acc.astype(o_ref.dtype)


def _split_dot(x_refs, k_sizes, w_rows):
    acc, off = None, 0
    for x_ref, ks in zip(x_refs, k_sizes):
        part = jnp.dot(x_ref[...], w_rows(off, ks), preferred_element_type=F32)
        acc = part if acc is None else acc + part
        off += ks
    return acc


def _mm_kernel_single(*refs, mode, k_sizes):
    n_x = len(k_sizes)
    x_refs, w_ref = refs[:n_x], refs[n_x]
    if mode == "residual":
        res_ref, o_ref = refs[n_x + 1:]
    else:
        (o_ref,), res_ref = refs[n_x + 1:], None
    acc = _split_dot(x_refs, k_sizes, lambda off, ks: w_ref[off:off + ks, :].astype(BF16))
    _epilogue(acc, mode, res_ref, o_ref)


def _mm_kernel_multi(*refs, mode):
    if mode == "residual":
        x_ref, w_ref, res_ref, o_ref = refs
    else:
        x_ref, w_ref, o_ref, acc_ref = refs
    k = pl.program_id(2)
    part = jnp.dot(x_ref[...], w_ref[...].astype(BF16), preferred_element_type=F32)

    if mode == "residual":
        @pl.when(k == 0)
        def _():
            o_ref[...] = res_ref[...] + part

        @pl.when(k > 0)
        def _():
            o_ref[...] += part
    else:
        @pl.when(k == 0)
        def _():
            acc_ref[...] = part

        @pl.when(k > 0)
        def _():
            acc_ref[...] += part

        @pl.when(k == pl.num_programs(2) - 1)
        def _():
            _epilogue(acc_ref[...], mode, None, o_ref)


def _weight_spec(layer, block, index_map):
    return pl.BlockSpec((None,) + block, lambda *g: (layer,) + tuple(index_map(*g)))


def _matmul(x, w, layer, *, mode="none", res=None, out_dtype=F32, tm=1024, tn=1024, tk=4096):
    xs = x if isinstance(x, (tuple, list)) else (x,)
    m = xs[0].shape[0]
    k_sizes = tuple(xi.shape[1] for xi in xs)
    _, kdim, n = w.shape
    assert sum(k_sizes) == kdim
    tm, tn, tk = min(tm, m), min(tn, n), min(tk, kdim)
    nk = kdim // tk
    assert m % tm == 0 and n % tn == 0 and kdim % tk == 0
    out_bytes = jnp.dtype(out_dtype).itemsize
    w_bytes = jnp.dtype(w.dtype).itemsize
    vmem = 2 * (tm * tk * 2 + tk * tn * w_bytes + tm * tn * out_bytes) + tm * tn * 4 + (4 << 20)
    if w_bytes != 2:
        vmem += tk * tn * 2
    if nk == 1:
        grid = (m // tm, n // tn)
        in_specs = [pl.BlockSpec((tm, ks), lambda i, j: (i, 0)) for ks in k_sizes]
        in_specs.append(_weight_spec(layer, (tk, tn), lambda i, j: (0, j)))
        res_spec = pl.BlockSpec((tm, tn), lambda i, j: (i, j))
        out_spec = pl.BlockSpec((tm, tn), lambda i, j: (i, j))
        body = functools.partial(_mm_kernel_single, mode=mode, k_sizes=k_sizes)
        scratch, sem = [], ("parallel", "parallel")
    else:
        assert len(xs) == 1
        grid = (m // tm, n // tn, nk)
        in_specs = [pl.BlockSpec((tm, tk), lambda i, j, k: (i, k)),
                    _weight_spec(layer, (tk, tn), lambda i, j, k: (k, j))]
        res_spec = pl.BlockSpec((tm, tn), lambda i, j, k: (i, j))
        out_spec = pl.BlockSpec((tm, tn), lambda i, j, k: (i, j))
        body = functools.partial(_mm_kernel_multi, mode=mode)
        sem = ("parallel", "parallel", "arbitrary")
        if mode == "residual":
            assert out_dtype == F32
            scratch = []
            vmem += tm * tn * 4
        else:
            scratch = [pltpu.VMEM((tm, tn), F32)]
            vmem += tm * tn * 4
    args = list(xs) + [w]
    if mode == "residual":
        in_specs.append(res_spec)
        args.append(res)
        vmem += 2 * tm * tn * 4
    return pl.pallas_call(
        body,
        out_shape=jax.ShapeDtypeStruct((m, n), out_dtype),
        grid=grid,
        in_specs=in_specs,
        out_specs=out_spec,
        scratch_shapes=scratch,
        compiler_params=_cparams(sem, vmem),
        name="proj_" + mode,
    )(*args)


def _mm_kernel_stationary(*refs, mode, k_sizes, chunk_rows):
    n_x = len(k_sizes)
    x_refs, wchunk_ref = refs[:n_x], refs[n_x]
    if mode == "residual":
        res_ref, o_ref, wbf_ref = refs[n_x + 1:]
    else:
        (o_ref, wbf_ref), res_ref = refs[n_x + 1:], None
    j, i = pl.program_id(0), pl.program_id(1)

    @pl.when(j < pl.num_programs(0) - 1)
    def _():
        r0 = pl.multiple_of(i * chunk_rows, chunk_rows)
        wbf_ref[j % 2, pl.ds(r0, chunk_rows), :] = wchunk_ref[...].astype(BF16)

    @pl.when(j > 0)
    def _():
        w_tile = wbf_ref.at[(j + 1) % 2]
        acc = _split_dot(x_refs, k_sizes, lambda off, ks: w_tile[off:off + ks, :])
        _epilogue(acc, mode, res_ref, o_ref)


def _matmul_stationary(x, w, layer, *, mode="none", res=None, out_dtype=F32, tm=1024, tn=1024, k_slab=None):
    xs = x if isinstance(x, (tuple, list)) else (x,)
    m = xs[0].shape[0]
    slab, n_slabs = (0, 1) if k_slab is None else k_slab
    assert len(xs) == 1 or n_slabs == 1
    k_sizes = tuple(xi.shape[1] // n_slabs for xi in xs)
    _, k_total, n = w.shape
    kdim = k_total // n_slabs
    assert sum(k_sizes) == kdim and m % tm == 0 and n % tn == 0
    n_i, n_j = m // tm, n // tn
    assert kdim % n_i == 0
    chunk_rows = kdim // n_i
    assert chunk_rows % 16 == 0
    out_bytes = jnp.dtype(out_dtype).itemsize

    def row_tile(j, i):
        return jnp.where(j == 0, 0, i)

    in_specs = [pl.BlockSpec((tm, ks), lambda j, i: (row_tile(j, i), slab)) for ks in k_sizes]
    in_specs.append(_weight_spec(layer, (chunk_rows, tn), lambda j, i: (slab * n_i + i, jnp.minimum(j, n_j - 1))))
    io_spec = pl.BlockSpec((tm, tn), lambda j, i: (row_tile(j, i), jnp.maximum(j - 1, 0)))
    args = list(xs) + [w]
    vmem = (2 * (tm * kdim * 2 + chunk_rows * tn * 4 + tm * tn * out_bytes) + 2 * kdim * tn * 2
            + tm * tn * 4 + (4 << 20))
    if mode == "residual":
        in_specs.append(io_spec)
        args.append(res)
        vmem += 2 * tm * tn * 4
    return pl.pallas_call(
        functools.partial(_mm_kernel_stationary, mode=mode, k_sizes=k_sizes, chunk_rows=chunk_rows),
        out_shape=jax.ShapeDtypeStruct((m, n), out_dtype),
        grid=(n_j + 1, n_i),
        in_specs=in_specs,
        out_specs=io_spec,
        scratch_shapes=[pltpu.VMEM((2, kdim, tn), BF16)],
        compiler_params=_cparams(("arbitrary", "arbitrary"), vmem),
        name="projws_" + mode,
    )(*args)


def _chunk_pos(shape, chunk):
    return lax.broadcasted_iota(jnp.int32, shape, 0) % chunk


def _cumsum_in_chunks(x, chunk):
    pos = _chunk_pos(x.shape, chunk)
    shift = 1
    while shift < chunk:
        x = x + jnp.where(pos >= shift, pltpu.roll(x, shift, 0), 0.0)
        shift *= 2
    return x


def _causal_mask(chunk):
    r = lax.broadcasted_iota(jnp.int32, (chunk, chunk), 0)
    c = lax.broadcasted_iota(jnp.int32, (chunk, chunk), 1)
    return r >= c


_NT = (((1,), (1,)), ((), ()))
_TN = (((0,), (0,)), ((), ()))


def _lower_bound(logits, layer):
    rows = [logits[i:i + 1] for i in range(logits.shape[0])]
    mx = functools.reduce(jnp.maximum, rows)
    es = [jnp.exp(r - mx) for r in rows]
    den = functools.reduce(lambda a, b: a + b, es)
    lb = es[0] / den
    for i in range(1, layer + 1):
        lb = lb + es[i] / den
    return lb


def _hgrn_kernel(*refs, chunk, n_chunks, layer, has_state, heads):
    if has_state:
        q_ref, f_ref, v_ref, g_ref, lbl_ref, gn_ref, s0_ref, o_ref, sout_ref, st_ref = refs
    else:
        (q_ref, f_ref, v_ref, g_ref, lbl_ref, gn_ref, o_ref, sout_ref, st_ref), s0_ref = refs, None
    t = pl.program_id(2)
    last = t == pl.num_programs(2) - 1
    mask = _causal_mask(chunk)

    @pl.when(t == 0)
    def _():
        for hh in range(heads):
            if has_state:
                st_ref[hh] = s0_ref[0, hh].T
            else:
                st_ref[hh] = jnp.zeros(st_ref.shape[1:], F32)

    finals = []
    for hh in range(heads):
        ls = slice(hh * HEAD_LANES, (hh + 1) * HEAD_LANES)
        lb = _lower_bound(lbl_ref[:, ls], layer)
        f = lb + (1.0 - lb) * jax.nn.sigmoid(f_ref[:, ls])
        q = jax.nn.silu(q_ref[:, ls])
        k = 1.0 - f
        b = _cumsum_in_chunks(jnp.log(f), chunk)
        q_in = (q * jnp.exp(b)).astype(BF16)
        k_in = (k * jnp.exp(-b)).astype(BF16)
        v = v_ref[:, ls].astype(BF16)
        st = st_ref[hh]
        outs = []
        for c in range(n_chunks):
            sl = slice(c * chunk, (c + 1) * chunk)
            b_c = b[sl]
            b_last = b_c[chunk - 1:chunk]
            k_out = (k[sl] * jnp.exp(b_last - b_c)).astype(BF16)
            attn = lax.dot_general(q_in[sl], k_in[sl], _NT, preferred_element_type=F32)
            attn = jnp.where(mask, attn, 0.0).astype(BF16)
            o_intra = jnp.dot(attn, v[sl], preferred_element_type=F32)
            o_inter = lax.dot_general(q_in[sl], st.astype(BF16), _NT, preferred_element_type=F32)
            kv_t = lax.dot_general(v[sl], k_out, _TN, preferred_element_type=F32)
            st = jnp.exp(b_last) * st + kv_t
            outs.append(o_intra + o_inter)
        st_ref[hh] = st
        o = outs[0] if n_chunks == 1 else jnp.concatenate(outs, axis=0)
        o = o * lax.rsqrt(jnp.mean(o * o, axis=-1, keepdims=True) + EPS)
        o = o * gn_ref[:, ls] * jax.nn.silu(g_ref[:, ls])
        o_ref[:, ls] = o.astype(o_ref.dtype)
        finals.append(st)

    @pl.when(last)
    def _():
        for hh in range(heads):
            sout_ref[0, hh] = finals[hh].T


def _ret_kernel(*refs, chunk, n_chunks, has_state, heads, dv):
    if has_state:
        q_ref, k_ref, v_ref, g_ref, cos_ref, sin_ref, lg_ref, gn_ref, s0_ref, o_ref, sout_ref, st_ref = refs
    else:
        (q_ref, k_ref, v_ref, g_ref, cos_ref, sin_ref, lg_ref, gn_ref, o_ref, sout_ref, st_ref), s0_ref = refs, None
    t = pl.program_id(2)
    last = t == pl.num_programs(2) - 1
    cosf, sinf = cos_ref[...], sin_ref[...]
    half = HEAD_LANES // 2
    mask = _causal_mask(chunk)
    pos1 = (_chunk_pos(cosf.shape, chunk) + 1).astype(F32)

    @pl.when(t == 0)
    def _():
        for hh in range(heads):
            if has_state:
                st_ref[hh] = s0_ref[0, hh]
            else:
                st_ref[hh] = jnp.zeros(st_ref.shape[1:], F32)

    finals = []
    for hh in range(heads):
        ls = slice(hh * HEAD_LANES, (hh + 1) * HEAD_LANES)
        ws = slice(hh * dv, (hh + 1) * dv)
        q, k = q_ref[:, ls], k_ref[:, ls]
        qr = q * cosf + pltpu.roll(q, half, 1) * sinf
        kr = (k * cosf + pltpu.roll(k, half, 1) * sinf) * (HEAD_LANES ** -0.5)
        lg_wide = lg_ref[hh]
        lg = lg_wide[:, :HEAD_LANES]
        b = pos1 * lg
        b_last = float(chunk) * lg
        q_in = (qr * jnp.exp(b)).astype(BF16)
        k_in = (kr * jnp.exp(-b)).astype(BF16)
        k_out = (kr * jnp.exp(b_last - b)).astype(BF16)
        decay = jnp.exp(float(chunk) * lg_wide)
        v = v_ref[:, ws].astype(BF16)
        st = st_ref[hh]
        outs = []
        for c in range(n_chunks):
            sl = slice(c * chunk, (c + 1) * chunk)
            attn = lax.dot_general(q_in[sl], k_in[sl], _NT, preferred_element_type=F32)
            attn = jnp.where(mask, attn, 0.0).astype(BF16)
            o_intra = jnp.dot(attn, v[sl], preferred_element_type=F32)
            o_inter = jnp.dot(q_in[sl], st.astype(BF16), preferred_element_type=F32)
            kv = lax.dot_general(k_out[sl], v[sl], _TN, preferred_element_type=F32)
            st = decay * st + kv
            outs.append(o_intra + o_inter)
        st_ref[hh] = st
        o = outs[0] if n_chunks == 1 else jnp.concatenate(outs, axis=0)
        c0 = o - jnp.mean(o, axis=-1, keepdims=True)
        o = c0 * lax.rsqrt(jnp.mean(c0 * c0, axis=-1, keepdims=True) + EPS)
        o = o * gn_ref[:, ws] * jax.nn.silu(g_ref[:, ws])
        o_ref[:, ws] = o.astype(o_ref.dtype)
        finals.append(st)

    @pl.when(last)
    def _():
        for hh in range(heads):
            sout_ref[0, hh] = finals[hh]


def _rotary_tables(pos):
    half = HEAD_LANES // 2
    inv = 1.0 / (ROPE_BASE ** jnp.linspace(0.0, 1.0, half, dtype=F32))
    ang = pos.astype(F32)[:, None] * inv[None, :]
    cos, sin = jnp.cos(ang), jnp.sin(ang)
    return jnp.concatenate([cos, cos], axis=-1), jnp.concatenate([-sin, sin], axis=-1)


def _even_mixer(proj, batch, seq, pos, layer, s_a, s_b, a_shape, b_shape, lb_logits, g_a, g_b,
                rows_per_step=1024, heads_per_step=4):
    m = proj.shape[0]
    a_heads, dk, dv_a = a_shape
    b_heads, dk_b, dv_b = b_shape
    assert dk == HEAD_LANES and dv_a == HEAD_LANES and dk_b == HEAD_LANES
    chunk = min(CHUNK, seq)
    rows = min(rows_per_step, seq)
    assert seq % rows == 0 and rows % chunk == 0
    nt = seq // rows
    has_state = s_a is not None
    hp_a = a_heads if nt == 1 else heads_per_step
    hp_b = b_heads if nt == 1 else heads_per_step
    assert a_heads % hp_a == 0 and b_heads % hp_b == 0
    a_w, a_v = a_heads * dk, a_heads * dv_a
    b_qk, b_v = b_heads * HEAD_LANES, b_heads * dv_b
    off_b = 2 * a_w + 2 * a_v
    grid_sem = ("parallel", "parallel", "arbitrary")

    def cols(offset, width, hp):
        blk = hp * width
        assert offset % blk == 0
        return pl.BlockSpec((rows, blk), lambda b, h, t: (b * nt + t, offset // blk + h))

    n_layers = lb_logits.shape[0]
    wa = hp_a * HEAD_LANES
    in_specs = [cols(0, dk, hp_a), cols(a_w, dk, hp_a), cols(2 * a_w, dv_a, hp_a), cols(2 * a_w + a_v, dv_a, hp_a),
                pl.BlockSpec((n_layers, wa), lambda b, h, t: (0, h)),
                pl.BlockSpec((1, wa), lambda b, h, t: (0, h))]
    args = [proj, proj, proj, proj, lb_logits, g_a.reshape(1, a_v)]
    if has_state:
        in_specs.append(pl.BlockSpec((1, hp_a, dk, dv_a), lambda b, h, t: (b, h, 0, 0)))
        args.append(s_a)
    o_a, new_a = pl.pallas_call(
        functools.partial(_hgrn_kernel, chunk=chunk, n_chunks=rows // chunk, layer=layer, has_state=has_state,
                          heads=hp_a),
        out_shape=(jax.ShapeDtypeStruct((m, a_v), BF16),
                   jax.ShapeDtypeStruct((batch, a_heads, dk, dv_a), F32)),
        grid=(batch, a_heads // hp_a, nt),
        in_specs=in_specs,
        out_specs=(pl.BlockSpec((rows, wa), lambda b, h, t: (b * nt + t, h)),
                   pl.BlockSpec((1, hp_a, dk, dv_a), lambda b, h, t: (b, h, 0, 0))),
        scratch_shapes=[pltpu.VMEM((hp_a, dv_a, dk), F32)],
        compiler_params=_cparams(grid_sem, 48 << 20),
        name="hgrn2",
    )(*args)

    cosf, sinf = _rotary_tables(pos)
    log_gamma = jnp.log(1.0 - jnp.exp2(-5.0 - jnp.arange(b_heads, dtype=F32)))
    lg = jnp.broadcast_to(log_gamma[:, None, None], (b_heads, 1, dv_b))
    wv = hp_b * dv_b
    in_specs = [cols(off_b, HEAD_LANES, hp_b), cols(off_b + b_qk, HEAD_LANES, hp_b),
                cols(off_b + 2 * b_qk, dv_b, hp_b), cols(off_b + 2 * b_qk + b_v, dv_b, hp_b),
                pl.BlockSpec((rows, HEAD_LANES), lambda b, h, t: (t, 0)),
                pl.BlockSpec((rows, HEAD_LANES), lambda b, h, t: (t, 0)),
                pl.BlockSpec((hp_b, 1, dv_b), lambda b, h, t: (h, 0, 0)),
                pl.BlockSpec((1, wv), lambda b, h, t: (0, h))]
    args = [proj, proj, proj, proj, cosf, sinf, lg, g_b.reshape(1, b_v)]
    if has_state:
        in_specs.append(pl.BlockSpec((1, hp_b, HEAD_LANES, dv_b), lambda b, h, t: (b, h, 0, 0)))
        args.append(s_b)
    o_b, new_b = pl.pallas_call(
        functools.partial(_ret_kernel, chunk=chunk, n_chunks=rows // chunk, has_state=has_state, heads=hp_b,
                          dv=dv_b),
        out_shape=(jax.ShapeDtypeStruct((m, b_v), BF16),
                   jax.ShapeDtypeStruct((batch, b_heads, HEAD_LANES, dv_b), F32)),
        grid=(batch, b_heads // hp_b, nt),
        in_specs=in_specs,
        out_specs=(pl.BlockSpec((rows, wv), lambda b, h, t: (b * nt + t, h)),
                   pl.BlockSpec((1, hp_b, HEAD_LANES, dv_b), lambda b, h, t: (b, h, 0, 0))),
        scratch_shapes=[pltpu.VMEM((hp_b, HEAD_LANES, dv_b), F32)],
        compiler_params=_cparams(grid_sem, 48 << 20),
        name="retention",
    )(*args)
    return o_a, o_b, new_a, new_b


SUBLANES = 8


def _rglru_block(xb, yb, pos_base, rows, first_pos_zero, cw_ref, cb_ref, wa_ref, wx_ref, gab_ref, gxb_ref, lam_ref,
                 prev_ref, h_ref, a_ref, b_ref):
    width = cw_ref.shape[0]
    assert width - 1 <= SUBLANES <= rows
    grouped = (rows // SUBLANES, SUBLANES, xb.shape[1])
    sub = lax.broadcasted_iota(jnp.int32, grouped, 1)
    x3 = xb.reshape(grouped)
    prev = prev_ref[...]
    cw = cw_ref[...]
    acc = None
    for j in range(width):
        back = width - 1 - j
        if back == 0:
            shifted = x3
        else:
            rot = pltpu.roll(x3, back, 1)
            rot_before = jnp.concatenate([pltpu.roll(prev, back, 0)[None], rot[:-1]], axis=0)
            shifted = jnp.where(sub >= back, rot, rot_before)
        term = shifted * cw[j:j + 1]
        acc = term if acc is None else acc + term
    conv = (cb_ref[...] + acc).reshape(xb.shape)
    prev_ref[...] = xb[rows - SUBLANES:rows]
    tail = xb[rows - (width - 1):rows]

    conv_bf = conv.astype(BF16)
    r = jax.nn.sigmoid(jnp.dot(conv_bf, wa_ref[0], preferred_element_type=F32) + gab_ref[...])
    i = jax.nn.sigmoid(jnp.dot(conv_bf, wx_ref[0], preferred_element_type=F32) + gxb_ref[...])
    log_a = -RG_C * r * jax.nn.softplus(-lam_ref[...])
    a = jnp.exp(log_a)
    gap = 1.0 - a * a
    mult = jnp.where(gap == 0.0, 0.0, gap * lax.rsqrt(gap))
    gated = i * conv
    bb = mult * gated
    if first_pos_zero:
        row = lax.broadcasted_iota(jnp.int32, (SUBLANES, a.shape[1]), 0)
        top = jnp.where(row + pos_base == 0, gated[:SUBLANES], bb[:SUBLANES])
        bb = jnp.concatenate([top, bb[SUBLANES:]], axis=0)

    a3, b3 = a.reshape(grouped), bb.reshape(grouped)
    shift = 1
    while shift < SUBLANES:
        take = sub >= shift
        b3 = jnp.where(take, a3 * pltpu.roll(b3, shift, 1) + b3, b3)
        a3 = jnp.where(take, a3 * pltpu.roll(a3, shift, 1), a3)
        shift *= 2
    a_ref[...] = a3.reshape(a.shape)
    b_ref[...] = b3.reshape(a.shape)

    def group(gidx, carry):
        r0 = pl.multiple_of(gidx * SUBLANES, SUBLANES)
        hg = a_ref[pl.ds(r0, SUBLANES), :] * carry + b_ref[pl.ds(r0, SUBLANES), :]
        b_ref[pl.ds(r0, SUBLANES), :] = hg
        return hg[SUBLANES - 1:SUBLANES, :]

    h_ref[...] = lax.fori_loop(0, rows // SUBLANES, group, h_ref[...], unroll=True)
    return jax.nn.gelu(yb, approximate=True) * b_ref[...], tail


def _griffin_kernel(*refs, rows, streams, first_pos_zero, has_state):
    if has_state:
        (xb_ref, yb_ref, cw_ref, cb_ref, wa_ref, wx_ref, gab_ref, gxb_ref, lam_ref, sconv_ref, h0_ref,
         o_ref, hlast_ref, nconv_ref, xpad_ref, h_ref, a_ref, b_ref) = refs
    else:
        (xb_ref, yb_ref, cw_ref, cb_ref, wa_ref, wx_ref, gab_ref, gxb_ref, lam_ref,
         o_ref, hlast_ref, nconv_ref, xpad_ref, h_ref, a_ref, b_ref) = refs
        sconv_ref = h0_ref = None
    t = pl.program_id(2)

    @pl.when(t == 0)
    def _():
        for q in range(streams):
            xpad_ref[q] = jnp.zeros(xpad_ref.shape[1:], F32)
            if has_state:
                xpad_ref[q, SUBLANES - sconv_ref.shape[1]:SUBLANES, :] = sconv_ref[q]
                h_ref[q] = h0_ref[q]
            else:
                h_ref[q] = jnp.zeros(h_ref.shape[1:], F32)

    tails = []
    for q in range(streams):
        rs = slice(q * rows, (q + 1) * rows)
        o, tail = _rglru_block(xb_ref[rs, :], yb_ref[rs, :], t * rows, rows, first_pos_zero, cw_ref, cb_ref, wa_ref,
                               wx_ref, gab_ref, gxb_ref, lam_ref, xpad_ref.at[q], h_ref.at[q], a_ref.at[q],
                               b_ref.at[q])
        o_ref[rs, :] = o.astype(o_ref.dtype)
        tails.append(tail)

    @pl.when(t == pl.num_programs(2) - 1)
    def _():
        for q in range(streams):
            hlast_ref[q] = h_ref[q]
            nconv_ref[q] = tails[q]


def _pair_blockdiag(w, pair):
    n, c, _ = w.shape
    w = w.reshape(n // pair, pair, c, c)
    rows = []
    for p in range(pair):
        blocks = [w[:, p] if q == p else jnp.zeros_like(w[:, p]) for q in range(pair)]
        rows.append(jnp.concatenate(blocks, axis=-1))
    return jnp.concatenate(rows, axis=-2)


def _odd_mixer(xy, batch, seq, pos0, h0, s_conv, conv_w, conv_b, wa, wx, ga_b, gx_b, lam,
               rows_per_step=1024, short_streams_per_step=8):
    m = xy.shape[0]
    d_rnn = conv_w.shape[1]
    width = conv_w.shape[0]
    blk = wa.shape[1]
    nblk = d_rnn // blk
    rows = min(rows_per_step, seq)
    assert seq % rows == 0 and rows % SUBLANES == 0 and rows >= width - 1
    nt = seq // rows
    streams = short_streams_per_step if (nt == 1 and batch % short_streams_per_step == 0) else 1
    has_state = h0 is not None

    def vec(a):
        return a.reshape(1, d_rnn)

    def vspec():
        return pl.BlockSpec((1, blk), lambda b, j, t: (0, j))

    def rows_spec(col0):
        return pl.BlockSpec((streams * rows, blk), lambda b, j, t: (b * nt + t, col0 + j))

    def state_spec(n):
        return pl.BlockSpec((streams, n, blk), lambda b, j, t: (b, 0, j))

    in_specs = [rows_spec(0), rows_spec(nblk),
                pl.BlockSpec((width, blk), lambda b, j, t: (0, j)),
                vspec(),
                pl.BlockSpec((1, blk, blk), lambda b, j, t: (j, 0, 0)),
                pl.BlockSpec((1, blk, blk), lambda b, j, t: (j, 0, 0)),
                vspec(), vspec(), vspec()]
    args = [xy, xy, conv_w, vec(conv_b), wa, wx, vec(ga_b), vec(gx_b), vec(lam)]
    if has_state:
        in_specs += [state_spec(width - 1), state_spec(1)]
        args += [s_conv, h0.reshape(batch, 1, d_rnn)]
    o, h_last, new_conv = pl.pallas_call(
        functools.partial(_griffin_kernel, rows=rows, streams=streams, first_pos_zero=(pos0 == 0),
                          has_state=has_state),
        out_shape=(jax.ShapeDtypeStruct((m, d_rnn), BF16),
                   jax.ShapeDtypeStruct((batch, 1, d_rnn), F32),
                   jax.ShapeDtypeStruct((batch, width - 1, d_rnn), F32)),
        grid=(batch // streams, nblk, nt),
        in_specs=in_specs,
        out_specs=(rows_spec(0), state_spec(1), state_spec(width - 1)),
        scratch_shapes=[pltpu.VMEM((streams, SUBLANES, blk), F32),
                        pltpu.VMEM((streams, 1, blk), F32),
                        pltpu.VMEM((streams, rows, blk), F32),
                        pltpu.VMEM((streams, rows, blk), F32)],
        compiler_params=_cparams(("parallel", "parallel", "arbitrary"), 32 << 20),
        name="griffin",
    )(*args)
    return o, h_last.reshape(batch, d_rnn), new_conv


def _trunk(x, pos0, states, p):
    batch, seq, d = x.shape
    depth = p["norm_mix"].shape[0]
    pos = pos0 + jnp.arange(seq, dtype=jnp.int32)
    h = x.reshape(batch * seq, d)
    many_rows = h.shape[0] >= 8 * 1024

    def project(xin, w, layer, **kw):
        if many_rows:
            return _matmul_stationary(xin, w, layer, tn=(512 if w.shape[1] > 4096 else 1024), **kw)
        return _matmul(xin, w, layer, tn=512, tk=w.shape[1], **kw)

    n_hgrn, n_ret, n_rg, n_conv = [], [], [], []
    hn = _rms_norm(h, p["norm_mix"][0], BF16)
    for l in range(depth):
        if l % 2 == 0:
            e = l // 2
            proj = project(hn, p["w_in_even"], e)
            s_a, s_b = (None, None) if states is None else (states[0][e], states[1][e])
            o_a, o_b, sa, sb = _even_mixer(proj, batch, seq, pos, l, s_a, s_b, p["hgrn_shape"], p["ret_shape"],
                                           p["hgrn_lb_logits"], p["hgrn_norm"][e], p["ret_norm"][e])
            n_hgrn.append(sa)
            n_ret.append(sb)
            h = project((o_a, o_b), p["w_out_even"], e, mode="residual", res=h)
        else:
            od = l // 2
            xy = project(hn, p["w_in_odd"], od)
            h0, sc = (None, None) if states is None else (states[2][od], states[3][od])
            o, sh, scn = _odd_mixer(xy, batch, seq, pos0, h0, sc, p["conv_w"][od], p["conv_b"][od],
                                    p["gate_a_w"][od], p["gate_x_w"][od], p["gate_a_b"][od], p["gate_x_b"][od],
                                    p["rglru_lambda"][od])
            n_rg.append(sh)
            n_conv.append(scn)
            h = project(o, p["w_out_odd"], od, mode="residual", res=h)
        hm = _rms_norm(h, p["norm_mlp"][l], BF16)
        up = project(hm, p["w_up"], l, mode="relu2", out_dtype=BF16)
        if many_rows:
            n_slabs = p["w_down"].shape[1] // d
            for s in range(n_slabs):
                h = _matmul_stationary(up, p["w_down"], l, mode="residual", res=h, k_slab=(s, n_slabs))
        else:
            h = _matmul(up, p["w_down"], l, mode="residual", res=h, tn=512, tk=4096)
        if l + 1 < depth:
            hn = _rms_norm(h, p["norm_mix"][l + 1], BF16)
    out = _rms_norm(h, p["norm_final"], x.dtype).reshape(batch, seq, d)
    return out, jnp.stack(n_hgrn), jnp.stack(n_ret), jnp.stack(n_rg), jnp.stack(n_conv)


def kernel(x_prompt, x_sample, state_hgrn, state_ret, state_rglru, state_conv, norm_mix, norm_mlp, norm_final,
           w_in_even, hgrn_lb_logits, hgrn_norm, ret_norm, w_out_even, w_in_odd, conv_w, conv_b, gate_a_w,
           gate_a_b, gate_x_w, gate_x_b, rglru_lambda, w_out_odd, w_up, w_down):
    past_len = 4096

    p = dict(
        norm_mix=norm_mix, norm_mlp=norm_mlp, norm_final=norm_final,
        w_in_even=w_in_even, w_out_even=w_out_even, w_in_odd=w_in_odd, w_out_odd=w_out_odd, w_up=w_up,
        w_down=w_down,
        hgrn_lb_logits=hgrn_lb_logits, hgrn_norm=hgrn_norm, ret_norm=ret_norm,
        conv_w=conv_w, conv_b=conv_b,
        gate_a_w=jnp.stack([_pair_blockdiag(w, RG_PAIR) for w in gate_a_w.astype(BF16)]),
        gate_x_w=jnp.stack([_pair_blockdiag(w, RG_PAIR) for w in gate_x_w.astype(BF16)]),
        gate_a_b=gate_a_b, gate_x_b=gate_x_b, rglru_lambda=rglru_lambda,
        hgrn_shape=state_hgrn.shape[2:], ret_shape=state_ret.shape[2:],
    )
    y_p, hg_p, rt_p, rg_p, cv_p = _trunk(x_prompt, 0, None, p)
    y_s, hg_s, rt_s, rg_s, cv_s = _trunk(x_sample, past_len, (state_hgrn, state_ret, state_rglru, state_conv), p)
    return (y_p, y_s, hg_p, rt_p, rg_p, cv_p, hg_s, rt_s, rg_s, cv_s)
```

```python
import functools

import jax
import jax.numpy as jnp
from jax import lax
from jax.experimental import pallas as pl
from jax.experimental.pallas import tpu as pltpu

EPS = 1e-6
CHUNK = 64
ROPE_BASE = 10000.0
RG_C = 8.0
HEAD_LANES = 128
RG_PAIR = 2

V7X_VMEM_BYTES = 64 * 1024 * 1024
VMEM_CAP_BYTES = V7X_VMEM_BYTES - 6 * 1024 * 1024

F32 = jnp.float32
BF16 = jnp.bfloat16


def _cparams(semantics, vmem_bytes):
    return pltpu.CompilerParams(dimension_semantics=semantics,
                                vmem_limit_bytes=int(min(max(vmem_bytes, 32 * 1024 * 1024), VMEM_CAP_BYTES)))


def _rms_kernel(x_ref, g_ref, o_ref):
    x = x_ref[...]
    y = x * lax.rsqrt(jnp.mean(x * x, axis=-1, keepdims=True) + EPS)
    o_ref[...] = (y * g_ref[...]).astype(o_ref.dtype)


def _rms_norm(x, g, out_dtype, tm=512):
    m, d = x.shape
    tm = min(tm, m)
    io_bytes = 2 * tm * d * (4 + jnp.dtype(out_dtype).itemsize)
    return pl.pallas_call(
        _rms_kernel,
        out_shape=jax.ShapeDtypeStruct((m, d), out_dtype),
        grid=(m // tm,),
        in_specs=[pl.BlockSpec((tm, d), lambda i: (i, 0)),
                  pl.BlockSpec((1, d), lambda i: (0, 0))],
        out_specs=pl.BlockSpec((tm, d), lambda i: (i, 0)),
        compiler_params=_cparams(("parallel",), io_bytes + 3 * tm * d * 4),
        name="rms_norm",
    )(x, g.reshape(1, d))


def _epilogue(acc, mode, res_ref, o_ref):
    if mode == "relu2":
        acc = jnp.square(jnp.maximum(acc, 0.0))
    elif mode == "residual":
        acc = res_ref[...] + acc
    o_ref[...] = acc.astype(o_ref.dtype)


def _split_dot(x_refs, k_sizes, w_rows):
    acc, off = None, 0
    for x_ref, ks in zip(x_refs, k_sizes):
        part = jnp.dot(x_ref[...], w_rows(off, ks), preferred_element_type=F32)
        acc = part if acc is None else acc + part
        off += ks
    return acc


def _mm_kernel_single(*refs, mode, k_sizes):
    n_x = len(k_sizes)
    x_refs, w_ref = refs[:n_x], refs[n_x]
    if mode == "residual":
        res_ref, o_ref = refs[n_x + 1:]
    else:
        (o_ref,), res_ref = refs[n_x + 1:], None
    acc = _split_dot(x_refs, k_sizes, lambda off, ks: w_ref[off:off + ks, :].astype(BF16))
    _epilogue(acc, mode, res_ref, o_ref)


def _mm_kernel_multi(*refs, mode):
    if mode == "residual":
        x_ref, w_ref, res_ref, o_ref = refs
    else:
        x_ref, w_ref, o_ref, acc_ref = refs
    k = pl.program_id(2)
    part = jnp.dot(x_ref[...], w_ref[...].astype(BF16), preferred_element_type=F32)

    if mode == "residual":
        @pl.when(k == 0)
        def _():
            o_ref[...] = res_ref[...] + part

        @pl.when(k > 0)
        def _():
            o_ref[...] += part
    else:
        @pl.when(k == 0)
        def _():
            acc_ref[...] = part

        @pl.when(k > 0)
        def _():
            acc_ref[...] += part

        @pl.when(k == pl.num_programs(2) - 1)
        def _():
            _epilogue(acc_ref[...], mode, None, o_ref)


def _weight_spec(layer, block, index_map):
    return pl.BlockSpec((None,) + block, lambda *g: (layer,) + tuple(index_map(*g)))


def _matmul(x, w, layer, *, mode="none", res=None, out_dtype=F32, tm=1024, tn=1024, tk=4096):
    xs = x if isinstance(x, (tuple, list)) else (x,)
    m = xs[0].shape[0]
    k_sizes = tuple(xi.shape[1] for xi in xs)
    _, kdim, n = w.shape
    assert sum(k_sizes) == kdim
    tm, tn, tk = min(tm, m), min(tn, n), min(tk, kdim)
    nk = kdim // tk
    assert m % tm == 0 and n % tn == 0 and kdim % tk == 0
    out_bytes = jnp.dtype(out_dtype).itemsize
    w_bytes = jnp.dtype(w.dtype).itemsize
    vmem = 2 * (tm * tk * 2 + tk * tn * w_bytes + tm * tn * out_bytes) + tm * tn * 4 + (4 << 20)
    if w_bytes != 2:
        vmem += tk * tn * 2
    if nk == 1:
        grid = (m // tm, n // tn)
        in_specs = [pl.BlockSpec((tm, ks), lambda i, j: (i, 0)) for ks in k_sizes]
        in_specs.append(_weight_spec(layer, (tk, tn), lambda i, j: (0, j)))
        res_spec = pl.BlockSpec((tm, tn), lambda i, j: (i, j))
        out_spec = pl.BlockSpec((tm, tn), lambda i, j: (i, j))
        body = functools.partial(_mm_kernel_single, mode=mode, k_sizes=k_sizes)
        scratch, sem = [], ("parallel", "parallel")
    else:
        assert len(xs) == 1
        grid = (m // tm, n // tn, nk)
        in_specs = [pl.BlockSpec((tm, tk), lambda i, j, k: (i, k)),
                    _weight_spec(layer, (tk, tn), lambda i, j, k: (k, j))]
        res_spec = pl.BlockSpec((tm, tn), lambda i, j, k: (i, j))
        out_spec = pl.BlockSpec((tm, tn), lambda i, j, k: (i, j))
        body = functools.partial(_mm_kernel_multi, mode=mode)
        sem = ("parallel", "parallel", "arbitrary")
        if mode == "residual":
            assert out_dtype == F32
            scratch = []
            vmem += tm * tn * 4
        else:
            scratch = [pltpu.VMEM((tm, tn), F32)]
            vmem += tm * tn * 4
    args = list(xs) + [w]
    if mode == "residual":
        in_specs.append(res_spec)
        args.append(res)
        vmem += 2 * tm * tn * 4
    return pl.pallas_call(
        body,
        out_shape=jax.ShapeDtypeStruct((m, n), out_dtype),
        grid=grid,
        in_specs=in_specs,
        out_specs=out_spec,
        scratch_shapes=scratch,
        compiler_params=_cparams(sem, vmem),
        name="proj_" + mode,
    )(*args)


def _mm_kernel_stationary(*refs, mode, k_sizes, chunk_rows):
    n_x = len(k_sizes)
    x_refs, wchunk_ref = refs[:n_x], refs[n_x]
    if mode == "residual":
        res_ref, o_ref, wbf_ref = refs[n_x + 1:]
    else:
        (o_ref, wbf_ref), res_ref = refs[n_x + 1:], None
    j, i = pl.program_id(0), pl.program_id(1)
    last = pl.num_programs(0) - 1

    def stage():
        r0 = pl.multiple_of(i * chunk_rows, chunk_rows)
        wbf_ref[j % 2, pl.ds(r0, chunk_rows), :] = wchunk_ref[...].astype(BF16)

    def multiply():
        w_tile = wbf_ref.at[(j + 1) % 2]
        acc = _split_dot(x_refs, k_sizes, lambda off, ks: w_tile[off:off + ks, :])
        _epilogue(acc, mode, res_ref, o_ref)

    @pl.when(j == 0)
    def _():
        stage()

    @pl.when(jnp.logical_and(j > 0, j < last))
    def _():
        multiply()
        stage()

    @pl.when(j == last)
    def _():
        multiply()


def _matmul_stationary(x, w, layer, *, mode="none", res=None, out_dtype=F32, tm=1024, tn=1024, k_slab=None):
    xs = x if isinstance(x, (tuple, list)) else (x,)
    m = xs[0].shape[0]
    slab, n_slabs = (0, 1) if k_slab is None else k_slab
    assert len(xs) == 1 or n_slabs == 1
    k_sizes = tuple(xi.shape[1] // n_slabs for xi in xs)
    _, k_total, n = w.shape
    kdim = k_total // n_slabs
    assert sum(k_sizes) == kdim and m % tm == 0 and n % tn == 0
    n_i, n_j = m // tm, n // tn
    assert kdim % n_i == 0
    chunk_rows = kdim // n_i
    assert chunk_rows % 16 == 0
    out_bytes = jnp.dtype(out_dtype).itemsize

    def row_tile(j, i):
        return jnp.where(j == 0, 0, i)

    in_specs = [pl.BlockSpec((tm, ks), lambda j, i: (row_tile(j, i), slab)) for ks in k_sizes]
    in_specs.append(_weight_spec(layer, (chunk_rows, tn), lambda j, i: (slab * n_i + i, jnp.minimum(j, n_j - 1))))
    io_spec = pl.BlockSpec((tm, tn), lambda j, i: (row_tile(j, i), jnp.maximum(j - 1, 0)))
    args = list(xs) + [w]
    vmem = (2 * (tm * kdim * 2 + chunk_rows * tn * 4 + tm * tn * out_bytes) + 2 * kdim * tn * 2
            + tm * tn * 4 + (4 << 20))
    if mode == "residual":
        in_specs.append(io_spec)
        args.append(res)
        vmem += 2 * tm * tn * 4
    return pl.pallas_call(
        functools.partial(_mm_kernel_stationary, mode=mode, k_sizes=k_sizes, chunk_rows=chunk_rows),
        out_shape=jax.ShapeDtypeStruct((m, n), out_dtype),
        grid=(n_j + 1, n_i),
        in_specs=in_specs,
        out_specs=io_spec,
        scratch_shapes=[pltpu.VMEM((2, kdim, tn), BF16)],
        compiler_params=_cparams(("arbitrary", "arbitrary"), vmem),
        name="projws_" + mode,
    )(*args)


def _chunk_pos(shape, chunk):
    return lax.broadcasted_iota(jnp.int32, shape, 0) % chunk


def _cumsum_in_chunks(x, chunk):
    pos = _chunk_pos(x.shape, chunk)
    shift = 1
    while shift < chunk:
        x = x + jnp.where(pos >= shift, pltpu.roll(x, shift, 0), 0.0)
        shift *= 2
    return x


def _causal_mask(chunk):
    r = lax.broadcasted_iota(jnp.int32, (chunk, chunk), 0)
    c = lax.broadcasted_iota(jnp.int32, (chunk, chunk), 1)
    return r >= c


_NT = (((1,), (1,)), ((), ()))
_TN = (((0,), (0,)), ((), ()))


def _lower_bound(logits, layer):
    rows = [logits[i:i + 1] for i in range(logits.shape[0])]
    mx = functools.reduce(jnp.maximum, rows)
    es = [jnp.exp(r - mx) for r in rows]
    den = functools.reduce(lambda a, b: a + b, es)
    lb = es[0] / den
    for i in range(1, layer + 1):
        lb = lb + es[i] / den
    return lb


def _hgrn_kernel(*refs, chunk, n_chunks, layer, has_state, heads):
    if has_state:
        q_ref, f_ref, v_ref, g_ref, lbl_ref, gn_ref, s0_ref, o_ref, sout_ref, st_ref = refs
    else:
        (q_ref, f_ref, v_ref, g_ref, lbl_ref, gn_ref, o_ref, sout_ref, st_ref), s0_ref = refs, None
    t = pl.program_id(2)
    last = t == pl.num_programs(2) - 1
    mask = _causal_mask(chunk)

    @pl.when(t == 0)
    def _():
        for hh in range(heads):
            if has_state:
                st_ref[hh] = s0_ref[0, hh].T
            else:
                st_ref[hh] = jnp.zeros(st_ref.shape[1:], F32)

    finals = []
    for hh in range(heads):
        ls = slice(hh * HEAD_LANES, (hh + 1) * HEAD_LANES)
        lb = _lower_bound(lbl_ref[:, ls], layer)
        f = lb + (1.0 - lb) * jax.nn.sigmoid(f_ref[:, ls])
        q = jax.nn.silu(q_ref[:, ls])
        k = 1.0 - f
        b = _cumsum_in_chunks(jnp.log(f), chunk)
        q_in = (q * jnp.exp(b)).astype(BF16)
        k_in = (k * jnp.exp(-b)).astype(BF16)
        v = v_ref[:, ls].astype(BF16)
        st = st_ref[hh]
        outs = []
        for c in range(n_chunks):
            sl = slice(c * chunk, (c + 1) * chunk)
            b_c = b[sl]
            b_last = b_c[chunk - 1:chunk]
            k_out = (k[sl] * jnp.exp(b_last - b_c)).astype(BF16)
            attn = lax.dot_general(q_in[sl], k_in[sl], _NT, preferred_element_type=F32)
            attn = jnp.where(mask, attn, 0.0).astype(BF16)
            o_intra = jnp.dot(attn, v[sl], preferred_element_type=F32)
            o_inter = lax.dot_general(q_in[sl], st.astype(BF16), _NT, preferred_element_type=F32)
            kv_t = lax.dot_general(v[sl], k_out, _TN, preferred_element_type=F32)
            st = jnp.exp(b_last) * st + kv_t
            outs.append(o_intra + o_inter)
        st_ref[hh] = st
        o = outs[0] if n_chunks == 1 else jnp.concatenate(outs, axis=0)
        o = o * lax.rsqrt(jnp.mean(o * o, axis=-1, keepdims=True) + EPS)
        o = o * gn_ref[:, ls] * jax.nn.silu(g_ref[:, ls])
        o_ref[:, ls] = o.astype(o_ref.dtype)
        finals.append(st)

    @pl.when(last)
    def _():
        for hh in range(heads):
            sout_ref[0, hh] = finals[hh].T


def _ret_kernel(*refs, chunk, n_chunks, has_state, heads, dv):
    if has_state:
        q_ref, k_ref, v_ref, g_ref, cos_ref, sin_ref, lg_ref, gn_ref, s0_ref, o_ref, sout_ref, st_ref = refs
    else:
        (q_ref, k_ref, v_ref, g_ref, cos_ref, sin_ref, lg_ref, gn_ref, o_ref, sout_ref, st_ref), s0_ref = refs, None
    t = pl.program_id(2)
    last = t == pl.num_programs(2) - 1
    cosf, sinf = cos_ref[...], sin_ref[...]
    half = HEAD_LANES // 2
    mask = _causal_mask(chunk)
    pos1 = (_chunk_pos(cosf.shape, chunk) + 1).astype(F32)

    @pl.when(t == 0)
    def _():
        for hh in range(heads):
            if has_state:
                st_ref[hh] = s0_ref[0, hh]
            else:
                st_ref[hh] = jnp.zeros(st_ref.shape[1:], F32)

    finals = []
    for hh in range(heads):
        ls = slice(hh * HEAD_LANES, (hh + 1) * HEAD_LANES)
        ws = slice(hh * dv, (hh + 1) * dv)
        q, k = q_ref[:, ls], k_ref[:, ls]
        qr = q * cosf + pltpu.roll(q, half, 1) * sinf
        kr = (k * cosf + pltpu.roll(k, half, 1) * sinf) * (HEAD_LANES ** -0.5)
        lg_wide = lg_ref[hh]
        lg = lg_wide[:, :HEAD_LANES]
        b = pos1 * lg
        b_last = float(chunk) * lg
        q_in = (qr * jnp.exp(b)).astype(BF16)
        k_in = (kr * jnp.exp(-b)).astype(BF16)
        k_out = (kr * jnp.exp(b_last - b)).astype(BF16)
        decay = jnp.exp(float(chunk) * lg_wide)
        v = v_ref[:, ws].astype(BF16)
        st = st_ref[hh]
        outs = []
        for c in range(n_chunks):
            sl = slice(c * chunk, (c + 1) * chunk)
            attn = lax.dot_general(q_in[sl], k_in[sl], _NT, preferred_element_type=F32)
            attn = jnp.where(mask, attn, 0.0).astype(BF16)
            o_intra = jnp.dot(attn, v[sl], preferred_element_type=F32)
            o_inter = jnp.dot(q_in[sl], st.astype(BF16), preferred_element_type=F32)
            kv = lax.dot_general(k_out[sl], v[sl], _TN, preferred_element_type=F32)
            st = decay * st + kv
            outs.append(o_intra + o_inter)
        st_ref[hh] = st
        o = outs[0] if n_chunks == 1 else jnp.concatenate(outs, axis=0)
        c0 = o - jnp.mean(o, axis=-1, keepdims=True)
        o = c0 * lax.rsqrt(jnp.mean(c0 * c0, axis=-1, keepdims=True) + EPS)
        o = o * gn_ref[:, ws] * jax.nn.silu(g_ref[:, ws])
        o_ref[:, ws] = o.astype(o_ref.dtype)
        finals.append(st)

    @pl.when(last)
    def _():
        for hh in range(heads):
            sout_ref[0, hh] = finals[hh]


def _rotary_tables(pos):
    half = HEAD_LANES // 2
    inv = 1.0 / (ROPE_BASE ** jnp.linspace(0.0, 1.0, half, dtype=F32))
    ang = pos.astype(F32)[:, None] * inv[None, :]
    cos, sin = jnp.cos(ang), jnp.sin(ang)
    return jnp.concatenate([cos, cos], axis=-1), jnp.concatenate([-sin, sin], axis=-1)


def _even_mixer(proj, batch, seq, pos, layer, s_a, s_b, a_shape, b_shape, lb_logits, g_a, g_b,
                rows_per_step=1024, heads_per_step=4):
    m = proj.shape[0]
    a_heads, dk, dv_a = a_shape
    b_heads, dk_b, dv_b = b_shape
    assert dk == HEAD_LANES and dv_a == HEAD_LANES and dk_b == HEAD_LANES
    chunk = min(CHUNK, seq)
    rows = min(rows_per_step, seq)
    assert seq % rows == 0 and rows % chunk == 0
    nt = seq // rows
    has_state = s_a is not None
    hp_a = a_heads if nt == 1 else heads_per_step
    hp_b = b_heads if nt == 1 else heads_per_step
    assert a_heads % hp_a == 0 and b_heads % hp_b == 0
    a_w, a_v = a_heads * dk, a_heads * dv_a
    b_qk, b_v = b_heads * HEAD_LANES, b_heads * dv_b
    off_b = 2 * a_w + 2 * a_v
    grid_sem = ("parallel", "parallel", "arbitrary")

    def cols(offset, width, hp):
        blk = hp * width
        assert offset % blk == 0
        return pl.BlockSpec((rows, blk), lambda b, h, t: (b * nt + t, offset // blk + h))

    n_layers = lb_logits.shape[0]
    wa = hp_a * HEAD_LANES
    in_specs = [cols(0, dk, hp_a), cols(a_w, dk, hp_a), cols(2 * a_w, dv_a, hp_a), cols(2 * a_w + a_v, dv_a, hp_a),
                pl.BlockSpec((n_layers, wa), lambda b, h, t: (0, h)),
                pl.BlockSpec((1, wa), lambda b, h, t: (0, h))]
    args = [proj, proj, proj, proj, lb_logits, g_a.reshape(1, a_v)]
    if has_state:
        in_specs.append(pl.BlockSpec((1, hp_a, dk, dv_a), lambda b, h, t: (b, h, 0, 0)))
        args.append(s_a)
    o_a, new_a = pl.pallas_call(
        functools.partial(_hgrn_kernel, chunk=chunk, n_chunks=rows // chunk, layer=layer, has_state=has_state,
                          heads=hp_a),
        out_shape=(jax.ShapeDtypeStruct((m, a_v), BF16),
                   jax.ShapeDtypeStruct((batch, a_heads, dk, dv_a), F32)),
        grid=(batch, a_heads // hp_a, nt),
        in_specs=in_specs,
        out_specs=(pl.BlockSpec((rows, wa), lambda b, h, t: (b * nt + t, h)),
                   pl.BlockSpec((1, hp_a, dk, dv_a), lambda b, h, t: (b, h, 0, 0))),
        scratch_shapes=[pltpu.VMEM((hp_a, dv_a, dk), F32)],
        compiler_params=_cparams(grid_sem, 48 << 20),
        name="hgrn2",
    )(*args)

    cosf, sinf = _rotary_tables(pos)
    log_gamma = jnp.log(1.0 - jnp.exp2(-5.0 - jnp.arange(b_heads, dtype=F32)))
    lg = jnp.broadcast_to(log_gamma[:, None, None], (b_heads, 1, dv_b))
    wv = hp_b * dv_b
    in_specs = [cols(off_b, HEAD_LANES, hp_b), cols(off_b + b_qk, HEAD_LANES, hp_b),
                cols(off_b + 2 * b_qk, dv_b, hp_b), cols(off_b + 2 * b_qk + b_v, dv_b, hp_b),
                pl.BlockSpec((rows, HEAD_LANES), lambda b, h, t: (t, 0)),
                pl.BlockSpec((rows, HEAD_LANES), lambda b, h, t: (t, 0)),
                pl.BlockSpec((hp_b, 1, dv_b), lambda b, h, t: (h, 0, 0)),
                pl.BlockSpec((1, wv), lambda b, h, t: (0, h))]
    args = [proj, proj, proj, proj, cosf, sinf, lg, g_b.reshape(1, b_v)]
    if has_state:
        in_specs.append(pl.BlockSpec((1, hp_b, HEAD_LANES, dv_b), lambda b, h, t: (b, h, 0, 0)))
        args.append(s_b)
    o_b, new_b = pl.pallas_call(
        functools.partial(_ret_kernel, chunk=chunk, n_chunks=rows // chunk, has_state=has_state, heads=hp_b,
                          dv=dv_b),
        out_shape=(jax.ShapeDtypeStruct((m, b_v), BF16),
                   jax.ShapeDtypeStruct((batch, b_heads, HEAD_LANES, dv_b), F32)),
        grid=(batch, b_heads // hp_b, nt),
        in_specs=in_specs,
        out_specs=(pl.BlockSpec((rows, wv), lambda b, h, t: (b * nt + t, h)),
                   pl.BlockSpec((1, hp_b, HEAD_LANES, dv_b), lambda b, h, t: (b, h, 0, 0))),
        scratch_shapes=[pltpu.VMEM((hp_b, HEAD_LANES, dv_b), F32)],
        compiler_params=_cparams(grid_sem, 48 << 20),
        name="retention",
    )(*args)
    return o_a, o_b, new_a, new_b


SUBLANES = 8


def _rglru_block(xb, yb, pos_base, rows, first_pos_zero, cw_ref, cb_ref, wa_ref, wx_ref, gab_ref, gxb_ref, lam_ref,
                 prev_ref, h_ref, a_ref, b_ref):
    width = cw_ref.shape[0]
    assert width - 1 <= SUBLANES <= rows
    grouped = (rows // SUBLANES, SUBLANES, xb.shape[1])
    sub = lax.broadcasted_iota(jnp.int32, grouped, 1)
    x3 = xb.reshape(grouped)
    prev = prev_ref[...]
    cw = cw_ref[...]
    acc = None
    for j in range(width):
        back = width - 1 - j
        if back == 0:
            shifted = x3
        else:
            rot = pltpu.roll(x3, back, 1)
            rot_before = jnp.concatenate([pltpu.roll(prev, back, 0)[None], rot[:-1]], axis=0)
            shifted = jnp.where(sub >= back, rot, rot_before)
        term = shifted * cw[j:j + 1]
        acc = term if acc is None else acc + term
    conv = (cb_ref[...] + acc).reshape(xb.shape)
    prev_ref[...] = xb[rows - SUBLANES:rows]
    tail = xb[rows - (width - 1):rows]

    conv_bf = conv.astype(BF16)
    r = jax.nn.sigmoid(jnp.dot(conv_bf, wa_ref[0], preferred_element_type=F32) + gab_ref[...])
    i = jax.nn.sigmoid(jnp.dot(conv_bf, wx_ref[0], preferred_element_type=F32) + gxb_ref[...])
    log_a = -RG_C * r * jax.nn.softplus(-lam_ref[...])
    a = jnp.exp(log_a)
    gap = 1.0 - a * a
    mult = jnp.where(gap == 0.0, 0.0, gap * lax.rsqrt(gap))
    gated = i * conv
    bb = mult * gated
    if first_pos_zero:
        row = lax.broadcasted_iota(jnp.int32, (SUBLANES, a.shape[1]), 0)
        top = jnp.where(row + pos_base == 0, gated[:SUBLANES], bb[:SUBLANES])
        bb = jnp.concatenate([top, bb[SUBLANES:]], axis=0)

    a3, b3 = a.reshape(grouped), bb.reshape(grouped)
    shift = 1
    while shift < SUBLANES:
        take = sub >= shift
        b3 = jnp.where(take, a3 * pltpu.roll(b3, shift, 1) + b3, b3)
        a3 = jnp.where(take, a3 * pltpu.roll(a3, shift, 1), a3)
        shift *= 2
    a_ref[...] = a3.reshape(a.shape)
    b_ref[...] = b3.reshape(a.shape)

    def group(gidx, carry):
        r0 = pl.multiple_of(gidx * SUBLANES, SUBLANES)
        hg = a_ref[pl.ds(r0, SUBLANES), :] * carry + b_ref[pl.ds(r0, SUBLANES), :]
        b_ref[pl.ds(r0, SUBLANES), :] = hg
        return hg[SUBLANES - 1:SUBLANES, :]

    h_ref[...] = lax.fori_loop(0, rows // SUBLANES, group, h_ref[...], unroll=True)
    return jax.nn.gelu(yb, approximate=True) * b_ref[...], tail


def _griffin_kernel(*refs, rows, streams, first_pos_zero, has_state):
    if has_state:
        (xb_ref, yb_ref, cw_ref, cb_ref, wa_ref, wx_ref, gab_ref, gxb_ref, lam_ref, sconv_ref, h0_ref,
         o_ref, hlast_ref, nconv_ref, xpad_ref, h_ref, a_ref, b_ref) = refs
    else:
        (xb_ref, yb_ref, cw_ref, cb_ref, wa_ref, wx_ref, gab_ref, gxb_ref, lam_ref,
         o_ref, hlast_ref, nconv_ref, xpad_ref, h_ref, a_ref, b_ref) = refs
        sconv_ref = h0_ref = None
    t = pl.program_id(2)

    @pl.when(t == 0)
    def _():
        for q in range(streams):
            xpad_ref[q] = jnp.zeros(xpad_ref.shape[1:], F32)
            if has_state:
                xpad_ref[q, SUBLANES - sconv_ref.shape[1]:SUBLANES, :] = sconv_ref[q]
                h_ref[q] = h0_ref[q]
            else:
                h_ref[q] = jnp.zeros(h_ref.shape[1:], F32)

    tails = []
    for q in range(streams):
        rs = slice(q * rows, (q + 1) * rows)
        o, tail = _rglru_block(xb_ref[rs, :], yb_ref[rs, :], t * rows, rows, first_pos_zero, cw_ref, cb_ref, wa_ref,
                               wx_ref, gab_ref, gxb_ref, lam_ref, xpad_ref.at[q], h_ref.at[q], a_ref.at[q],
                               b_ref.at[q])
        o_ref[rs, :] = o.astype(o_ref.dtype)
        tails.append(tail)

    @pl.when(t == pl.num_programs(2) - 1)
    def _():
        for q in range(streams):
            hlast_ref[q] = h_ref[q]
            nconv_ref[q] = tails[q]


def _pair_blockdiag(w, pair):
    n, c, _ = w.shape
    w = w.reshape(n // pair, pair, c, c)
    rows = []
    for p in range(pair):
        blocks = [w[:, p] if q == p else jnp.zeros_like(w[:, p]) for q in range(pair)]
        rows.append(jnp.concatenate(blocks, axis=-1))
    return jnp.concatenate(rows, axis=-2)


def _odd_mixer(xy, batch, seq, pos0, h0, s_conv, conv_w, conv_b, wa, wx, ga_b, gx_b, lam,
               rows_per_step=1024, short_streams_per_step=8):
    m = xy.shape[0]
    d_rnn = conv_w.shape[1]
    width = conv_w.shape[0]
    blk = wa.shape[1]
    nblk = d_rnn // blk
    rows = min(rows_per_step, seq)
    assert seq % rows == 0 and rows % SUBLANES == 0 and rows >= width - 1
    nt = seq // rows
    streams = short_streams_per_step if (nt == 1 and batch % short_streams_per_step == 0) else 1
    has_state = h0 is not None

    def vec(a):
        return a.reshape(1, d_rnn)

    def vspec():
        return pl.BlockSpec((1, blk), lambda b, j, t: (0, j))

    def rows_spec(col0):
        return pl.BlockSpec((streams * rows, blk), lambda b, j, t: (b * nt + t, col0 + j))

    def state_spec(n):
        return pl.BlockSpec((streams, n, blk), lambda b, j, t: (b, 0, j))

    in_specs = [rows_spec(0), rows_spec(nblk),
                pl.BlockSpec((width, blk), lambda b, j, t: (0, j)),
                vspec(),
                pl.BlockSpec((1, blk, blk), lambda b, j, t: (j, 0, 0)),
                pl.BlockSpec((1, blk, blk), lambda b, j, t: (j, 0, 0)),
                vspec(), vspec(), vspec()]
    args = [xy, xy, conv_w, vec(conv_b), wa, wx, vec(ga_b), vec(gx_b), vec(lam)]
    if has_state:
        in_specs += [state_spec(width - 1), state_spec(1)]
        args += [s_conv, h0.reshape(batch, 1, d_rnn)]
    o, h_last, new_conv = pl.pallas_call(
        functools.partial(_griffin_kernel, rows=rows, streams=streams, first_pos_zero=(pos0 == 0),
                          has_state=has_state),
        out_shape=(jax.ShapeDtypeStruct((m, d_rnn), BF16),
                   jax.ShapeDtypeStruct((batch, 1, d_rnn), F32),
                   jax.ShapeDtypeStruct((batch, width - 1, d_rnn), F32)),
        grid=(batch // streams, nblk, nt),
        in_specs=in_specs,
        out_specs=(rows_spec(0), state_spec(1), state_spec(width - 1)),
        scratch_shapes=[pltpu.VMEM((streams, SUBLANES, blk), F32),
                        pltpu.VMEM((streams, 1, blk), F32),
                        pltpu.VMEM((streams, rows, blk), F32),
                        pltpu.VMEM((streams, rows, blk), F32)],
        compiler_params=_cparams(("parallel", "parallel", "arbitrary"), 32 << 20),
        name="griffin",
    )(*args)
    return o, h_last.reshape(batch, d_rnn), new_conv


def _trunk(x, pos0, states, p):
    batch, seq, d = x.shape
    depth = p["norm_mix"].shape[0]
    pos = pos0 + jnp.arange(seq, dtype=jnp.int32)
    h = x.reshape(batch * seq, d)
    many_rows = h.shape[0] >= 8 * 1024

    def project(xin, w, layer, **kw):
        if many_rows:
            return _matmul_stationary(xin, w, layer, tn=(512 if w.shape[1] > 4096 else 1024), **kw)
        return _matmul(xin, w, layer, tn=512, tk=w.shape[1], **kw)

    n_hgrn, n_ret, n_rg, n_conv = [], [], [], []
    hn = _rms_norm(h, p["norm_mix"][0], BF16)
    for l in range(depth):
        if l % 2 == 0:
            e = l // 2
            proj = project(hn, p["w_in_even"], e)
            s_a, s_b = (None, None) if states is None else (states[0][e], states[1][e])
            o_a, o_b, sa, sb = _even_mixer(proj, batch, seq, pos, l, s_a, s_b, p["hgrn_shape"], p["ret_shape"],
                                           p["hgrn_lb_logits"], p["hgrn_norm"][e], p["ret_norm"][e])
            n_hgrn.append(sa)
            n_ret.append(sb)
            h = project((o_a, o_b), p["w_out_even"], e, mode="residual", res=h)
        else:
            od = l // 2
            xy = project(hn, p["w_in_odd"], od)
            h0, sc = (None, None) if states is None else (states[2][od], states[3][od])
            o, sh, scn = _odd_mixer(xy, batch, seq, pos0, h0, sc, p["conv_w"][od], p["conv_b"][od],
                                    p["gate_a_w"][od], p["gate_x_w"][od], p["gate_a_b"][od], p["gate_x_b"][od],
                                    p["rglru_lambda"][od])
            n_rg.append(sh)
            n_conv.append(scn)
            h = project(o, p["w_out_odd"], od, mode="residual", res=h)
        hm = _rms_norm(h, p["norm_mlp"][l], BF16)
        up = project(hm, p["w_up"], l, mode="relu2", out_dtype=BF16)
        if many_rows:
            n_slabs = p["w_down"].shape[1] // d
            for s in range(n_slabs):
                h = _matmul_stationary(up, p["w_down"], l, mode="residual", res=h, k_slab=(s, n_slabs))
        else:
            h = _matmul(up, p["w_down"], l, mode="residual", res=h, tn=512, tk=4096)
        if l + 1 < depth:
            hn = _rms_norm(h, p["norm_mix"][l + 1], BF16)
    out = _rms_norm(h, p["norm_final"], x.dtype).reshape(batch, seq, d)
    return out, jnp.stack(n_hgrn), jnp.stack(n_ret), jnp.stack(n_rg), jnp.stack(n_conv)


def kernel(x_prompt, x_sample, state_hgrn, state_ret, state_rglru, state_conv, norm_mix, norm_mlp, norm_final,
           w_in_even, hgrn_lb_logits, hgrn_norm, ret_norm, w_out_even, w_in_odd, conv_w, conv_b, gate_a_w,
           gate_a_b, gate_x_w, gate_x_b, rglru_lambda, w_out_odd, w_up, w_down):
    past_len = 4096

    p = dict(
        norm_mix=norm_mix, norm_mlp=norm_mlp, norm_final=norm_final,
        w_in_even=w_in_even, w_out_even=w_out_even, w_in_odd=w_in_odd, w_out_odd=w_out_odd, w_up=w_up,
        w_down=w_down,
        hgrn_lb_logits=hgrn_lb_logits, hgrn_norm=hgrn_norm, ret_norm=ret_norm,
        conv_w=conv_w, conv_b=conv_b,
        gate_a_w=jnp.stack([_pair_blockdiag(w, RG_PAIR) for w in gate_a_w.astype(BF16)]),
        gate_x_w=jnp.stack([_pair_blockdiag(w, RG_PAIR) for w in gate_x_w.astype(BF16)]),
        gate_a_b=gate_a_b, gate_x_b=gate_x_b, rglru_lambda=rglru_lambda,
        hgrn_shape=state_hgrn.shape[2:], ret_shape=state_ret.shape[2:],
    )
    y_p, hg_p, rt_p, rg_p, cv_p = _trunk(x_prompt, 0, None, p)
    y_s, hg_s, rt_s, rg_s, cv_s = _trunk(x_sample, past_len, (state_hgrn, state_ret, state_rglru, state_conv), p)
    return (y_p, y_s, hg_p, rt_p, rg_p, cv_p, hg_s, rt_s, rg_s, cv_s)
```

```python
import functools

import jax
import jax.numpy as jnp
from jax import lax
from jax.experimental import pallas as pl
from jax.experimental.pallas import tpu as pltpu

EPS = 1e-6
CHUNK = 64
ROPE_BASE = 10000.0
RG_C = 8.0
HEAD_LANES = 128
RG_PAIR = 2
MANY_ROWS = 8 * 1024
D_WIDE = 4096

V7X_VMEM_BYTES = 64 * 1024 * 1024
VMEM_CAP_BYTES = V7X_VMEM_BYTES - 4 * 1024 * 1024

F32 = jnp.float32
BF16 = jnp.bfloat16


def _cparams(semantics, vmem_bytes):
    return pltpu.CompilerParams(dimension_semantics=semantics,
                                vmem_limit_bytes=int(min(max(vmem_bytes, 32 * 1024 * 1024), VMEM_CAP_BYTES)))


def _rms_kernel(x_ref, g_ref, o_ref):
    x = x_ref[...]
    y = x * lax.rsqrt(jnp.mean(x * x, axis=-1, keepdims=True) + EPS)
    o_ref[...] = (y * g_ref[...]).astype(o_ref.dtype)


def _rms_norm(x, g, out_dtype, tm=512):
    m, d = x.shape
    tm = min(tm, m)
    io_bytes = 2 * tm * d * (4 + jnp.dtype(out_dtype).itemsize)
    return pl.pallas_call(
        _rms_kernel,
        out_shape=jax.ShapeDtypeStruct((m, d), out_dtype),
        grid=(m // tm,),
        in_specs=[pl.BlockSpec((tm, d), lambda i: (i, 0)),
                  pl.BlockSpec((1, d), lambda i: (0, 0))],
        out_specs=pl.BlockSpec((tm, d), lambda i: (i, 0)),
        compiler_params=_cparams(("parallel",), io_bytes + 3 * tm * d * 4),
        name="rms_norm",
    )(x, g.reshape(1, d))


def _epilogue(acc, mode, res_ref, o_ref):
    if mode == "relu2":
        acc = jnp.square(jnp.maximum(acc, 0.0))
    elif mode == "residual":
        acc = res_ref[...] + acc
    o_ref[...] = acc.astype(o_ref.dtype)


def _split_dot(x_refs, k_sizes, w_rows):
    acc, off = None, 0
    for x_ref, ks in zip(x_refs, k_sizes):
        part = jnp.dot(x_ref[...], w_rows(off, ks), preferred_element_type=F32)
        acc = part if acc is None else acc + part
        off += ks
    return acc


def _mm_kernel_single(*refs, mode, k_sizes):
    n_x = len(k_sizes)
    x_refs, w_ref = refs[:n_x], refs[n_x]
    if mode == "residual":
        res_ref, o_ref = refs[n_x + 1:]
    else:
        (o_ref,), res_ref = refs[n_x + 1:], None
    acc = _split_dot(x_refs, k_sizes, lambda off, ks: w_ref[off:off + ks, :].astype(BF16))
    _epilogue(acc, mode, res_ref, o_ref)


def _mm_kernel_multi(*refs, mode):
    if mode == "residual":
        x_ref, w_ref, res_ref, o_ref = refs
    else:
        x_ref, w_ref, o_ref, acc_ref = refs
    k = pl.program_id(2)
    part = jnp.dot(x_ref[...], w_ref[...].astype(BF16), preferred_element_type=F32)

    if mode == "residual":
        @pl.when(k == 0)
        def _():
            o_ref[...] = res_ref[...] + part

        @pl.when(k > 0)
        def _():
            o_ref[...] += part
    else:
        @pl.when(k == 0)
        def _():
            acc_ref[...] = part

        @pl.when(k > 0)
        def _():
            acc_ref[...] += part

        @pl.when(k == pl.num_programs(2) - 1)
        def _():
            _epilogue(acc_ref[...], mode, None, o_ref)


def _weight_spec(layer, block, index_map):
    return pl.BlockSpec((None,) + block, lambda *g: (layer,) + tuple(index_map(*g)))


def _matmul(x, w, layer, *, mode="none", res=None, out_dtype=F32, tm=1024, tn=1024, tk=4096):
    xs = x if isinstance(x, (tuple, list)) else (x,)
    m = xs[0].shape[0]
    k_sizes = tuple(xi.shape[1] for xi in xs)
    _, kdim, n = w.shape
    assert sum(k_sizes) == kdim
    tm, tn, tk = min(tm, m), min(tn, n), min(tk, kdim)
    nk = kdim // tk
    assert m % tm == 0 and n % tn == 0 and kdim % tk == 0
    out_bytes = jnp.dtype(out_dtype).itemsize
    w_bytes = jnp.dtype(w.dtype).itemsize
    vmem = 2 * (tm * tk * 2 + tk * tn * w_bytes + tm * tn * out_bytes) + tm * tn * 4 + (4 << 20)
    if w_bytes != 2:
        vmem += tk * tn * 2
    if nk == 1:
        grid = (m // tm, n // tn)
        in_specs = [pl.BlockSpec((tm, ks), lambda i, j: (i, 0)) for ks in k_sizes]
        in_specs.append(_weight_spec(layer, (tk, tn), lambda i, j: (0, j)))
        res_spec = pl.BlockSpec((tm, tn), lambda i, j: (i, j))
        out_spec = pl.BlockSpec((tm, tn), lambda i, j: (i, j))
        body = functools.partial(_mm_kernel_single, mode=mode, k_sizes=k_sizes)
        scratch, sem = [], ("parallel", "parallel")
    else:
        assert len(xs) == 1
        grid = (m // tm, n // tn, nk)
        in_specs = [pl.BlockSpec((tm, tk), lambda i, j, k: (i, k)),
                    _weight_spec(layer, (tk, tn), lambda i, j, k: (k, j))]
        res_spec = pl.BlockSpec((tm, tn), lambda i, j, k: (i, j))
        out_spec = pl.BlockSpec((tm, tn), lambda i, j, k: (i, j))
        body = functools.partial(_mm_kernel_multi, mode=mode)
        sem = ("parallel", "parallel", "arbitrary")
        if mode == "residual":
            assert out_dtype == F32
            scratch = []
            vmem += tm * tn * 4
        else:
            scratch = [pltpu.VMEM((tm, tn), F32)]
            vmem += tm * tn * 4
    args = list(xs) + [w]
    if mode == "residual":
        in_specs.append(res_spec)
        args.append(res)
        vmem += 2 * tm * tn * 4
    return pl.pallas_call(
        body,
        out_shape=jax.ShapeDtypeStruct((m, n), out_dtype),
        grid=grid,
        in_specs=in_specs,
        out_specs=out_spec,
        scratch_shapes=scratch,
        compiler_params=_cparams(sem, vmem),
        name="proj_" + mode,
    )(*args)


def _mm_kernel_stationary(*refs, mode, k_sizes, chunk_rows):
    n_x = len(k_sizes)
    x_refs, wchunk_ref = refs[:n_x], refs[n_x]
    if mode == "residual":
        res_ref, o_ref, wout_ref, wbf_ref = refs[n_x + 1:]
    else:
        (o_ref, wout_ref, wbf_ref), res_ref = refs[n_x + 1:], None
    j, i = pl.program_id(0), pl.program_id(1)
    last = pl.num_programs(0) - 1

    def stage():
        r0 = pl.multiple_of(i * chunk_rows, chunk_rows)
        chunk = wchunk_ref[...].astype(BF16)
        wbf_ref[j % 2, pl.ds(r0, chunk_rows), :] = chunk
        wout_ref[...] = chunk

    def multiply():
        w_tile = wbf_ref.at[(j + 1) % 2]
        acc = _split_dot(x_refs, k_sizes, lambda off, ks: w_tile[off:off + ks, :])
        _epilogue(acc, mode, res_ref, o_ref)

    @pl.when(j == 0)
    def _():
        stage()

    @pl.when(jnp.logical_and(j > 0, j < last))
    def _():
        multiply()
        stage()

    @pl.when(j == last)
    def _():
        multiply()


def _matmul_stationary(x, w, layer, *, mode="none", res=None, out_dtype=F32, tm=1024, tn=1024, k_slab=None):
    xs = x if isinstance(x, (tuple, list)) else (x,)
    m = xs[0].shape[0]
    slab, n_slabs = (0, 1) if k_slab is None else k_slab
    assert len(xs) == 1 or n_slabs == 1
    k_sizes = tuple(xi.shape[1] // n_slabs for xi in xs)
    _, k_total, n = w.shape
    kdim = k_total // n_slabs
    assert sum(k_sizes) == kdim and m % tm == 0 and n % tn == 0
    n_i, n_j = m // tm, n // tn
    assert kdim % n_i == 0
    chunk_rows = kdim // n_i
    assert chunk_rows % 16 == 0
    out_bytes = jnp.dtype(out_dtype).itemsize

    def row_tile(j, i):
        return jnp.where(j == 0, 0, i)

    in_specs = [pl.BlockSpec((tm, ks), lambda j, i: (row_tile(j, i), slab)) for ks in k_sizes]
    in_specs.append(_weight_spec(layer, (chunk_rows, tn), lambda j, i: (slab * n_i + i, jnp.minimum(j, n_j - 1))))
    io_spec = pl.BlockSpec((tm, tn), lambda j, i: (row_tile(j, i), jnp.maximum(j - 1, 0)))
    wout_spec = pl.BlockSpec((chunk_rows, tn), lambda j, i: (jnp.where(j < n_j, i, n_i - 1), jnp.minimum(j, n_j - 1)))
    args = list(xs) + [w]
    vmem = (2 * (tm * kdim * 2 + chunk_rows * tn * 4 + tm * tn * out_bytes) + 2 * kdim * tn * 2
            + 2 * chunk_rows * tn * 2 + tm * tn * 4 + (4 << 20))
    if mode == "residual":
        in_specs.append(io_spec)
        args.append(res)
        vmem += 2 * tm * tn * 4
    return pl.pallas_call(
        functools.partial(_mm_kernel_stationary, mode=mode, k_sizes=k_sizes, chunk_rows=chunk_rows),
        out_shape=(jax.ShapeDtypeStruct((m, n), out_dtype), jax.ShapeDtypeStruct((kdim, n), BF16)),
        grid=(n_j + 1, n_i),
        in_specs=in_specs,
        out_specs=(io_spec, wout_spec),
        scratch_shapes=[pltpu.VMEM((2, kdim, tn), BF16)],
        compiler_params=_cparams(("arbitrary", "arbitrary"), vmem),
        name="projws_" + mode,
    )(*args)


def _chunk_pos(shape, chunk):
    return lax.broadcasted_iota(jnp.int32, shape, 0) % chunk


def _cumsum_in_chunks(x, chunk):
    pos = _chunk_pos(x.shape, chunk)
    shift = 1
    while shift < chunk:
        x = x + jnp.where(pos >= shift, pltpu.roll(x, shift, 0), 0.0)
        shift *= 2
    return x


def _causal_mask(chunk):
    r = lax.broadcasted_iota(jnp.int32, (chunk, chunk), 0)
    c = lax.broadcasted_iota(jnp.int32, (chunk, chunk), 1)
    return r >= c


_NT = (((1,), (1,)), ((), ()))
_TN = (((0,), (0,)), ((), ()))


def _lower_bound(logits, layer):
    rows = [logits[i:i + 1] for i in range(logits.shape[0])]
    mx = functools.reduce(jnp.maximum, rows)
    es = [jnp.exp(r - mx) for r in rows]
    den = functools.reduce(lambda a, b: a + b, es)
    lb = es[0] / den
    for i in range(1, layer + 1):
        lb = lb + es[i] / den
    return lb


def _hgrn_kernel(*refs, chunk, n_chunks, layer, has_state, heads):
    if has_state:
        q_ref, f_ref, v_ref, g_ref, lbl_ref, gn_ref, s0_ref, o_ref, sout_ref, st_ref = refs
    else:
        (q_ref, f_ref, v_ref, g_ref, lbl_ref, gn_ref, o_ref, sout_ref, st_ref), s0_ref = refs, None
    t = pl.program_id(2)
    last = t == pl.num_programs(2) - 1
    mask = _causal_mask(chunk)

    @pl.when(t == 0)
    def _():
        for hh in range(heads):
            if has_state:
                st_ref[hh] = s0_ref[0, hh].T
            else:
                st_ref[hh] = jnp.zeros(st_ref.shape[1:], F32)

    finals = []
    for hh in range(heads):
        ls = slice(hh * HEAD_LANES, (hh + 1) * HEAD_LANES)
        lb = _lower_bound(lbl_ref[:, ls], layer)
        f = lb + (1.0 - lb) * jax.nn.sigmoid(f_ref[:, ls])
        q = jax.nn.silu(q_ref[:, ls])
        k = 1.0 - f
        b = _cumsum_in_chunks(jnp.log(f), chunk)
        q_in = (q * jnp.exp(b)).astype(BF16)
        k_in = (k * jnp.exp(-b)).astype(BF16)
        v = v_ref[:, ls].astype(BF16)
        st = st_ref[hh]
        outs = []
        for c in range(n_chunks):
            sl = slice(c * chunk, (c + 1) * chunk)
            b_c = b[sl]
            b_last = b_c[chunk - 1:chunk]
            k_out = (k[sl] * jnp.exp(b_last - b_c)).astype(BF16)
            attn = lax.dot_general(q_in[sl], k_in[sl], _NT, preferred_element_type=F32)
            attn = jnp.where(mask, attn, 0.0).astype(BF16)
            o_intra = jnp.dot(attn, v[sl], preferred_element_type=F32)
            o_inter = lax.dot_general(q_in[sl], st.astype(BF16), _NT, preferred_element_type=F32)
            kv_t = lax.dot_general(v[sl], k_out, _TN, preferred_element_type=F32)
            st = jnp.exp(b_last) * st + kv_t
            outs.append(o_intra + o_inter)
        st_ref[hh] = st
        o = outs[0] if n_chunks == 1 else jnp.concatenate(outs, axis=0)
        o = o * lax.rsqrt(jnp.mean(o * o, axis=-1, keepdims=True) + EPS)
        o = o * gn_ref[:, ls] * jax.nn.silu(g_ref[:, ls])
        o_ref[:, ls] = o.astype(o_ref.dtype)
        finals.append(st)

    @pl.when(last)
    def _():
        for hh in range(heads):
            sout_ref[0, hh] = finals[hh].T


def _ret_kernel(*refs, chunk, n_chunks, has_state, heads, dv):
    if has_state:
        q_ref, k_ref, v_ref, g_ref, cos_ref, sin_ref, lg_ref, gn_ref, s0_ref, o_ref, sout_ref, st_ref = refs
    else:
        (q_ref, k_ref, v_ref, g_ref, cos_ref, sin_ref, lg_ref, gn_ref, o_ref, sout_ref, st_ref), s0_ref = refs, None
    t = pl.program_id(2)
    last = t == pl.num_programs(2) - 1
    cosf, sinf = cos_ref[...], sin_ref[...]
    half = HEAD_LANES // 2
    mask = _causal_mask(chunk)
    pos1 = (_chunk_pos(cosf.shape, chunk) + 1).astype(F32)

    @pl.when(t == 0)
    def _():
        for hh in range(heads):
            if has_state:
                st_ref[hh] = s0_ref[0, hh]
            else:
                st_ref[hh] = jnp.zeros(st_ref.shape[1:], F32)

    finals = []
    for hh in range(heads):
        ls = slice(hh * HEAD_LANES, (hh + 1) * HEAD_LANES)
        ws = slice(hh * dv, (hh + 1) * dv)
        q, k = q_ref[:, ls], k_ref[:, ls]
        qr = q * cosf + pltpu.roll(q, half, 1) * sinf
        kr = (k * cosf + pltpu.roll(k, half, 1) * sinf) * (HEAD_LANES ** -0.5)
        lg_wide = lg_ref[hh]
        lg = lg_wide[:, :HEAD_LANES]
        b = pos1 * lg
        b_last = float(chunk) * lg
        q_in = (qr * jnp.exp(b)).astype(BF16)
        k_in = (kr * jnp.exp(-b)).astype(BF16)
        k_out = (kr * jnp.exp(b_last - b)).astype(BF16)
        decay = jnp.exp(float(chunk) * lg_wide)
        v = v_ref[:, ws].astype(BF16)
        st = st_ref[hh]
        outs = []
        for c in range(n_chunks):
            sl = slice(c * chunk, (c + 1) * chunk)
            attn = lax.dot_general(q_in[sl], k_in[sl], _NT, preferred_element_type=F32)
            attn = jnp.where(mask, attn, 0.0).astype(BF16)
            o_intra = jnp.dot(attn, v[sl], preferred_element_type=F32)
            o_inter = jnp.dot(q_in[sl], st.astype(BF16), preferred_element_type=F32)
            kv = lax.dot_general(k_out[sl], v[sl], _TN, preferred_element_type=F32)
            st = decay * st + kv
            outs.append(o_intra + o_inter)
        st_ref[hh] = st
        o = outs[0] if n_chunks == 1 else jnp.concatenate(outs, axis=0)
        c0 = o - jnp.mean(o, axis=-1, keepdims=True)
        o = c0 * lax.rsqrt(jnp.mean(c0 * c0, axis=-1, keepdims=True) + EPS)
        o = o * gn_ref[:, ws] * jax.nn.silu(g_ref[:, ws])
        o_ref[:, ws] = o.astype(o_ref.dtype)
        finals.append(st)

    @pl.when(last)
    def _():
        for hh in range(heads):
            sout_ref[0, hh] = finals[hh]


def _rotary_tables(pos):
    half = HEAD_LANES // 2
    inv = 1.0 / (ROPE_BASE ** jnp.linspace(0.0, 1.0, half, dtype=F32))
    ang = pos.astype(F32)[:, None] * inv[None, :]
    cos, sin = jnp.cos(ang), jnp.sin(ang)
    return jnp.concatenate([cos, cos], axis=-1), jnp.concatenate([-sin, sin], axis=-1)


def _even_mixer(proj, batch, seq, pos, layer, s_a, s_b, a_shape, b_shape, lb_logits, g_a, g_b,
                rows_per_step=1024, heads_per_step=4):
    m = proj.shape[0]
    a_heads, dk, dv_a = a_shape
    b_heads, dk_b, dv_b = b_shape
    assert dk == HEAD_LANES and dv_a == HEAD_LANES and dk_b == HEAD_LANES
    chunk = min(CHUNK, seq)
    rows = min(rows_per_step, seq)
    assert seq % rows == 0 and rows % chunk == 0
    nt = seq // rows
    has_state = s_a is not None
    hp_a = a_heads if nt == 1 else heads_per_step
    hp_b = b_heads if nt == 1 else heads_per_step
    assert a_heads % hp_a == 0 and b_heads % hp_b == 0
    a_w, a_v = a_heads * dk, a_heads * dv_a
    b_qk, b_v = b_heads * HEAD_LANES, b_heads * dv_b
    off_b = 2 * a_w + 2 * a_v
    grid_sem = ("parallel", "parallel", "arbitrary")

    def cols(offset, width, hp):
        blk = hp * width
        assert offset % blk == 0
        return pl.BlockSpec((rows, blk), lambda b, h, t: (b * nt + t, offset // blk + h))

    n_layers = lb_logits.shape[0]
    wa = hp_a * HEAD_LANES
    in_specs = [cols(0, dk, hp_a), cols(a_w, dk, hp_a), cols(2 * a_w, dv_a, hp_a), cols(2 * a_w + a_v, dv_a, hp_a),
                pl.BlockSpec((n_layers, wa), lambda b, h, t: (0, h)),
                pl.BlockSpec((1, wa), lambda b, h, t: (0, h))]
    args = [proj, proj, proj, proj, lb_logits, g_a.reshape(1, a_v)]
    if has_state:
        in_specs.append(pl.BlockSpec((1, hp_a, dk, dv_a), lambda b, h, t: (b, h, 0, 0)))
        args.append(s_a)
    o_a, new_a = pl.pallas_call(
        functools.partial(_hgrn_kernel, chunk=chunk, n_chunks=rows // chunk, layer=layer, has_state=has_state,
                          heads=hp_a),
        out_shape=(jax.ShapeDtypeStruct((m, a_v), BF16),
                   jax.ShapeDtypeStruct((batch, a_heads, dk, dv_a), F32)),
        grid=(batch, a_heads // hp_a, nt),
        in_specs=in_specs,
        out_specs=(pl.BlockSpec((rows, wa), lambda b, h, t: (b * nt + t, h)),
                   pl.BlockSpec((1, hp_a, dk, dv_a), lambda b, h, t: (b, h, 0, 0))),
        scratch_shapes=[pltpu.VMEM((hp_a, dv_a, dk), F32)],
        compiler_params=_cparams(grid_sem, 48 << 20),
        name="hgrn2",
    )(*args)

    cosf, sinf = _rotary_tables(pos)
    log_gamma = jnp.log(1.0 - jnp.exp2(-5.0 - jnp.arange(b_heads, dtype=F32)))
    lg = jnp.broadcast_to(log_gamma[:, None, None], (b_heads, 1, dv_b))
    wv = hp_b * dv_b
    in_specs = [cols(off_b, HEAD_LANES, hp_b), cols(off_b + b_qk, HEAD_LANES, hp_b),
                cols(off_b + 2 * b_qk, dv_b, hp_b), cols(off_b + 2 * b_qk + b_v, dv_b, hp_b),
                pl.BlockSpec((rows, HEAD_LANES), lambda b, h, t: (t, 0)),
                pl.BlockSpec((rows, HEAD_LANES), lambda b, h, t: (t, 0)),
                pl.BlockSpec((hp_b, 1, dv_b), lambda b, h, t: (h, 0, 0)),
                pl.BlockSpec((1, wv), lambda b, h, t: (0, h))]
    args = [proj, proj, proj, proj, cosf, sinf, lg, g_b.reshape(1, b_v)]
    if has_state:
        in_specs.append(pl.BlockSpec((1, hp_b, HEAD_LANES, dv_b), lambda b, h, t: (b, h, 0, 0)))
        args.append(s_b)
    o_b, new_b = pl.pallas_call(
        functools.partial(_ret_kernel, chunk=chunk, n_chunks=rows // chunk, has_state=has_state, heads=hp_b,
                          dv=dv_b),
        out_shape=(jax.ShapeDtypeStruct((m, b_v), BF16),
                   jax.ShapeDtypeStruct((batch, b_heads, HEAD_LANES, dv_b), F32)),
        grid=(batch, b_heads // hp_b, nt),
        in_specs=in_specs,
        out_specs=(pl.BlockSpec((rows, wv), lambda b, h, t: (b * nt + t, h)),
                   pl.BlockSpec((1, hp_b, HEAD_LANES, dv_b), lambda b, h, t: (b, h, 0, 0))),
        scratch_shapes=[pltpu.VMEM((hp_b, HEAD_LANES, dv_b), F32)],
        compiler_params=_cparams(grid_sem, 48 << 20),
        name="retention",
    )(*args)
    return o_a, o_b, new_a, new_b


SUBLANES = 8


def _rglru_block(xb, yb, pos_base, rows, first_pos_zero, cw_ref, cb_ref, wa_ref, wx_ref, gab_ref, gxb_ref, lam_ref,
                 prev_ref, h_ref, a_ref, b_ref):
    width = cw_ref.shape[0]
    assert width - 1 <= SUBLANES <= rows
    grouped = (rows // SUBLANES, SUBLANES, xb.shape[1])
    sub = lax.broadcasted_iota(jnp.int32, grouped, 1)
    x3 = xb.reshape(grouped)
    prev = prev_ref[...]
    cw = cw_ref[...]
    acc = None
    for j in range(width):
        back = width - 1 - j
        if back == 0:
            shifted = x3
        else:
            rot = pltpu.roll(x3, back, 1)
            rot_before = jnp.concatenate([pltpu.roll(prev, back, 0)[None], rot[:-1]], axis=0)
            shifted = jnp.where(sub >= back, rot, rot_before)
        term = shifted * cw[j:j + 1]
        acc = term if acc is None else acc + term
    conv = (cb_ref[...] + acc).reshape(xb.shape)
    prev_ref[...] = xb[rows - SUBLANES:rows]
    tail = xb[rows - (width - 1):rows]

    conv_bf = conv.astype(BF16)
    r = jax.nn.sigmoid(jnp.dot(conv_bf, wa_ref[0], preferred_element_type=F32) + gab_ref[...])
    i = jax.nn.sigmoid(jnp.dot(conv_bf, wx_ref[0], preferred_element_type=F32) + gxb_ref[...])
    log_a = -RG_C * r * jax.nn.softplus(-lam_ref[...])
    a = jnp.exp(log_a)
    gap = 1.0 - a * a
    mult = jnp.where(gap == 0.0, 0.0, gap * lax.rsqrt(gap))
    gated = i * conv
    bb = mult * gated
    if first_pos_zero:
        row = lax.broadcasted_iota(jnp.int32, (SUBLANES, a.shape[1]), 0)
        top = jnp.where(row + pos_base == 0, gated[:SUBLANES], bb[:SUBLANES])
        bb = jnp.concatenate([top, bb[SUBLANES:]], axis=0)

    a3, b3 = a.reshape(grouped), bb.reshape(grouped)
    shift = 1
    while shift < SUBLANES:
        take = sub >= shift
        b3 = jnp.where(take, a3 * pltpu.roll(b3, shift, 1) + b3, b3)
        a3 = jnp.where(take, a3 * pltpu.roll(a3, shift, 1), a3)
        shift *= 2
    a_ref[...] = a3.reshape(a.shape)
    b_ref[...] = b3.reshape(a.shape)

    def group(gidx, carry):
        r0 = pl.multiple_of(gidx * SUBLANES, SUBLANES)
        hg = a_ref[pl.ds(r0, SUBLANES), :] * carry + b_ref[pl.ds(r0, SUBLANES), :]
        b_ref[pl.ds(r0, SUBLANES), :] = hg
        return hg[SUBLANES - 1:SUBLANES, :]

    h_ref[...] = lax.fori_loop(0, rows // SUBLANES, group, h_ref[...], unroll=True)
    return jax.nn.gelu(yb, approximate=True) * b_ref[...], tail


def _griffin_kernel(*refs, rows, streams, first_pos_zero, has_state):
    if has_state:
        (xb_ref, yb_ref, cw_ref, cb_ref, wa_ref, wx_ref, gab_ref, gxb_ref, lam_ref, sconv_ref, h0_ref,
         o_ref, hlast_ref, nconv_ref, xpad_ref, h_ref, a_ref, b_ref) = refs
    else:
        (xb_ref, yb_ref, cw_ref, cb_ref, wa_ref, wx_ref, gab_ref, gxb_ref, lam_ref,
         o_ref, hlast_ref, nconv_ref, xpad_ref, h_ref, a_ref, b_ref) = refs
        sconv_ref = h0_ref = None
    t = pl.program_id(2)

    @pl.when(t == 0)
    def _():
        for q in range(streams):
            xpad_ref[q] = jnp.zeros(xpad_ref.shape[1:], F32)
            if has_state:
                xpad_ref[q, SUBLANES - sconv_ref.shape[1]:SUBLANES, :] = sconv_ref[q]
                h_ref[q] = h0_ref[q]
            else:
                h_ref[q] = jnp.zeros(h_ref.shape[1:], F32)

    tails = []
    for q in range(streams):
        rs = slice(q * rows, (q + 1) * rows)
        o, tail = _rglru_block(xb_ref[rs, :], yb_ref[rs, :], t * rows, rows, first_pos_zero, cw_ref, cb_ref, wa_ref,
                               wx_ref, gab_ref, gxb_ref, lam_ref, xpad_ref.at[q], h_ref.at[q], a_ref.at[q],
                               b_ref.at[q])
        o_ref[rs, :] = o.astype(o_ref.dtype)
        tails.append(tail)

    @pl.when(t == pl.num_programs(2) - 1)
    def _():
        for q in range(streams):
            hlast_ref[q] = h_ref[q]
            nconv_ref[q] = tails[q]


def _pair_blockdiag(w, pair):
    n, c, _ = w.shape
    w = w.reshape(n // pair, pair, c, c)
    rows = []
    for p in range(pair):
        blocks = [w[:, p] if q == p else jnp.zeros_like(w[:, p]) for q in range(pair)]
        rows.append(jnp.concatenate(blocks, axis=-1))
    return jnp.concatenate(rows, axis=-2)


def _odd_mixer(xy, batch, seq, pos0, h0, s_conv, conv_w, conv_b, wa, wx, ga_b, gx_b, lam,
               rows_per_step=1024, short_streams_per_step=8):
    m = xy.shape[0]
    d_rnn = conv_w.shape[1]
    width = conv_w.shape[0]
    blk = wa.shape[1]
    nblk = d_rnn // blk
    rows = min(rows_per_step, seq)
    assert seq % rows == 0 and rows % SUBLANES == 0 and rows >= width - 1
    nt = seq // rows
    streams = short_streams_per_step if (nt == 1 and batch % short_streams_per_step == 0) else 1
    has_state = h0 is not None

    def vec(a):
        return a.reshape(1, d_rnn)

    def vspec():
        return pl.BlockSpec((1, blk), lambda b, j, t: (0, j))

    def rows_spec(col0):
        return pl.BlockSpec((streams * rows, blk), lambda b, j, t: (b * nt + t, col0 + j))

    def state_spec(n):
        return pl.BlockSpec((streams, n, blk), lambda b, j, t: (b, 0, j))

    in_specs = [rows_spec(0), rows_spec(nblk),
                pl.BlockSpec((width, blk), lambda b, j, t: (0, j)),
                vspec(),
                pl.BlockSpec((1, blk, blk), lambda b, j, t: (j, 0, 0)),
                pl.BlockSpec((1, blk, blk), lambda b, j, t: (j, 0, 0)),
                vspec(), vspec(), vspec()]
    args = [xy, xy, conv_w, vec(conv_b), wa, wx, vec(ga_b), vec(gx_b), vec(lam)]
    if has_state:
        in_specs += [state_spec(width - 1), state_spec(1)]
        args += [s_conv, h0.reshape(batch, 1, d_rnn)]
    o, h_last, new_conv = pl.pallas_call(
        functools.partial(_griffin_kernel, rows=rows, streams=streams, first_pos_zero=(pos0 == 0),
                          has_state=has_state),
        out_shape=(jax.ShapeDtypeStruct((m, d_rnn), BF16),
                   jax.ShapeDtypeStruct((batch, 1, d_rnn), F32),
                   jax.ShapeDtypeStruct((batch, width - 1, d_rnn), F32)),
        grid=(batch // streams, nblk, nt),
        in_specs=in_specs,
        out_specs=(rows_spec(0), state_spec(1), state_spec(width - 1)),
        scratch_shapes=[pltpu.VMEM((streams, SUBLANES, blk), F32),
                        pltpu.VMEM((streams, 1, blk), F32),
                        pltpu.VMEM((streams, rows, blk), F32),
                        pltpu.VMEM((streams, rows, blk), F32)],
        compiler_params=_cparams(("parallel", "parallel", "arbitrary"), 32 << 20),
        name="griffin",
    )(*args)
    return o, h_last.reshape(batch, d_rnn), new_conv


def _trunk(x, pos0, states, p):
    batch, seq, d = x.shape
    depth = p["norm_mix"].shape[0]
    pos = pos0 + jnp.arange(seq, dtype=jnp.int32)
    h = x.reshape(batch * seq, d)
    many_rows = h.shape[0] >= MANY_ROWS
    rounded = p["rounded_weights"]

    def project(xin, name, layer, k_slab=None, **kw):
        w = p[name]
        if many_rows:
            out, w_bf = _matmul_stationary(xin, w, layer, tn=(512 if w.shape[1] > D_WIDE else 1024), k_slab=k_slab,
                                           **kw)
            rounded[(name, layer, k_slab)] = w_bf
            return out
        if (name, layer, k_slab) in rounded:
            w_bf = rounded[(name, layer, k_slab)]
            if k_slab is not None:
                xin = xin[:, k_slab[0] * w_bf.shape[0]:(k_slab[0] + 1) * w_bf.shape[0]]
            return _matmul(xin, w_bf[None], 0, tn=(512 if w_bf.shape[0] > D_WIDE else 1024), tk=w_bf.shape[0], **kw)
        assert k_slab is None
        return _matmul(xin, w, layer, tn=512, tk=(w.shape[1] if w.shape[1] % D_WIDE else D_WIDE), **kw)

    n_hgrn, n_ret, n_rg, n_conv = [], [], [], []
    hn = _rms_norm(h, p["norm_mix"][0], BF16)
    for l in range(depth):
        if l % 2 == 0:
            e = l // 2
            proj = project(hn, "w_in_even", e)
            s_a, s_b = (None, None) if states is None else (states[0][e], states[1][e])
            o_a, o_b, sa, sb = _even_mixer(proj, batch, seq, pos, l, s_a, s_b, p["hgrn_shape"], p["ret_shape"],
                                           p["hgrn_lb_logits"], p["hgrn_norm"][e], p["ret_norm"][e])
            n_hgrn.append(sa)
            n_ret.append(sb)
            h = project((o_a, o_b), "w_out_even", e, mode="residual", res=h)
        else:
            od = l // 2
            xy = project(hn, "w_in_odd", od)
            h0, sc = (None, None) if states is None else (states[2][od], states[3][od])
            o, sh, scn = _odd_mixer(xy, batch, seq, pos0, h0, sc, p["conv_w"][od], p["conv_b"][od],
                                    p["gate_a_w"][od], p["gate_x_w"][od], p["gate_a_b"][od], p["gate_x_b"][od],
                                    p["rglru_lambda"][od])
            n_rg.append(sh)
            n_conv.append(scn)
            h = project(o, "w_out_odd", od, mode="residual", res=h)
        hm = _rms_norm(h, p["norm_mlp"][l], BF16)
        up = project(hm, "w_up", l, mode="relu2", out_dtype=BF16)
        n_slabs = p["w_down"].shape[1] // d
        if many_rows or ("w_down", l, (0, n_slabs)) in rounded:
            for s in range(n_slabs):
                h = project(up, "w_down", l, k_slab=(s, n_slabs), mode="residual", res=h)
        else:
            h = project(up, "w_down", l, mode="residual", res=h)
        if l + 1 < depth:
            hn = _rms_norm(h, p["norm_mix"][l + 1], BF16)
    out = _rms_norm(h, p["norm_final"], x.dtype).reshape(batch, seq, d)
    return out, jnp.stack(n_hgrn), jnp.stack(n_ret), jnp.stack(n_rg), jnp.stack(n_conv)


def kernel(x_prompt, x_sample, state_hgrn, state_ret, state_rglru, state_conv, norm_mix, norm_mlp, norm_final,
           w_in_even, hgrn_lb_logits, hgrn_norm, ret_norm, w_out_even, w_in_odd, conv_w, conv_b, gate_a_w,
           gate_a_b, gate_x_w, gate_x_b, rglru_lambda, w_out_odd, w_up, w_down):
    past_len = 4096

    p = dict(
        norm_mix=norm_mix, norm_mlp=norm_mlp, norm_final=norm_final,
        w_in_even=w_in_even, w_out_even=w_out_even, w_in_odd=w_in_odd, w_out_odd=w_out_odd, w_up=w_up,
        w_down=w_down, rounded_weights={},
        hgrn_lb_logits=hgrn_lb_logits, hgrn_norm=hgrn_norm, ret_norm=ret_norm,
        conv_w=conv_w, conv_b=conv_b,
        gate_a_w=jnp.stack([_pair_blockdiag(w, RG_PAIR) for w in gate_a_w.astype(BF16)]),
        gate_x_w=jnp.stack([_pair_blockdiag(w, RG_PAIR) for w in gate_x_w.astype(BF16)]),
        gate_a_b=gate_a_b, gate_x_b=gate_x_b, rglru_lambda=rglru_lambda,
        hgrn_shape=state_hgrn.shape[2:], ret_shape=state_ret.shape[2:],
    )
    y_p, hg_p, rt_p, rg_p, cv_p = _trunk(x_prompt, 0, None, p)
    y_s, hg_s, rt_s, rg_s, cv_s = _trunk(x_sample, past_len, (state_hgrn, state_ret, state_rglru, state_conv), p)
    return (y_p, y_s, hg_p, rt_p, rg_p, cv_p, hg_s, rt_s, rg_s, cv_s)
```

```python
import functools

import jax
import jax.numpy as jnp
from jax import lax
from jax.experimental import pallas as pl
from jax.experimental.pallas import tpu as pltpu

EPS = 1e-6
CHUNK = 64
ROPE_BASE = 10000.0
RG_C = 8.0
HEAD_LANES = 128
RG_PAIR = 2

V7X_VMEM_BYTES = 64 * 1024 * 1024
VMEM_CAP_BYTES = V7X_VMEM_BYTES - 6 * 1024 * 1024

F32 = jnp.float32
BF16 = jnp.bfloat16


def _cparams(semantics, vmem_bytes):
    return pltpu.CompilerParams(dimension_semantics=semantics,
                                vmem_limit_bytes=int(min(max(vmem_bytes, 32 * 1024 * 1024), VMEM_CAP_BYTES)))


def _rms_kernel(x_ref, g_ref, o_ref):
    x = x_ref[...]
    y = x * lax.rsqrt(jnp.mean(x * x, axis=-1, keepdims=True) + EPS)
    o_ref[...] = (y * g_ref[...]).astype(o_ref.dtype)


def _rms_norm(x, g, out_dtype, tm=512):
    m, d = x.shape
    tm = min(tm, m)
    io_bytes = 2 * tm * d * (4 + jnp.dtype(out_dtype).itemsize)
    return pl.pallas_call(
        _rms_kernel,
        out_shape=jax.ShapeDtypeStruct((m, d), out_dtype),
        grid=(m // tm,),
        in_specs=[pl.BlockSpec((tm, d), lambda i: (i, 0)),
                  pl.BlockSpec((1, d), lambda i: (0, 0))],
        out_specs=pl.BlockSpec((tm, d), lambda i: (i, 0)),
        compiler_params=_cparams(("parallel",), io_bytes + 3 * tm * d * 4),
        name="rms_norm",
    )(x, g.reshape(1, d))


NORM_LANES = 128


def _epilogue(acc, mode, res_ref, o_ref, scale_ref=None):
    if scale_ref is not None:
        acc = acc * jnp.tile(scale_ref[...], (1, acc.shape[1] // NORM_LANES))
    if mode == "relu2":
        acc = jnp.square(jnp.maximum(acc, 0.0))
    elif mode == "residual":
        acc = res_ref[...] + acc
    o_ref[...] = acc.astype(o_ref.dtype)
    return acc


def _split_dot(x_refs, k_sizes, w_rows):
    acc, off = None, 0
    for x_ref, ks in zip(x_refs, k_sizes):
        part = jnp.dot(x_ref[...], w_rows(off, ks), preferred_element_type=F32)
        acc = part if acc is None else acc + part
        off += ks
    return acc


def _mm_kernel_single(*refs, mode, k_sizes):
    n_x = len(k_sizes)
    x_refs, w_ref = refs[:n_x], refs[n_x]
    if mode == "residual":
        res_ref, o_ref = refs[n_x + 1:]
    else:
        (o_ref,), res_ref = refs[n_x + 1:], None
    acc = _split_dot(x_refs, k_sizes, lambda off, ks: w_ref[off:off + ks, :].astype(BF16))
    _epilogue(acc, mode, res_ref, o_ref)


def _mm_kernel_multi(*refs, mode):
    if mode == "residual":
        x_ref, w_ref, res_ref, o_ref = refs
    else:
        x_ref, w_ref, o_ref, acc_ref = refs
    k = pl.program_id(2)
    part = jnp.dot(x_ref[...], w_ref[...].astype(BF16), preferred_element_type=F32)

    if mode == "residual":
        @pl.when(k == 0)
        def _():
            o_ref[...] = res_ref[...] + part

        @pl.when(k > 0)
        def _():
            o_ref[...] += part
    else:
        @pl.when(k == 0)
        def _():
            acc_ref[...] = part

        @pl.when(k > 0)
        def _():
            acc_ref[...] += part

        @pl.when(k == pl.num_programs(2) - 1)
        def _():
            _epilogue(acc_ref[...], mode, None, o_ref)


def _weight_spec(layer, block, index_map):
    return pl.BlockSpec((None,) + block, lambda *g: (layer,) + tuple(index_map(*g)))


def _matmul(x, w, layer, *, mode="none", res=None, out_dtype=F32, tm=1024, tn=1024, tk=4096):
    xs = x if isinstance(x, (tuple, list)) else (x,)
    m = xs[0].shape[0]
    k_sizes = tuple(xi.shape[1] for xi in xs)
    _, kdim, n = w.shape
    assert sum(k_sizes) == kdim
    tm, tn, tk = min(tm, m), min(tn, n), min(tk, kdim)
    nk = kdim // tk
    assert m % tm == 0 and n % tn == 0 and kdim % tk == 0
    out_bytes = jnp.dtype(out_dtype).itemsize
    w_bytes = jnp.dtype(w.dtype).itemsize
    vmem = 2 * (tm * tk * 2 + tk * tn * w_bytes + tm * tn * out_bytes) + tm * tn * 4 + (4 << 20)
    if w_bytes != 2:
        vmem += tk * tn * 2
    if nk == 1:
        grid = (m // tm, n // tn)
        in_specs = [pl.BlockSpec((tm, ks), lambda i, j: (i, 0)) for ks in k_sizes]
        in_specs.append(_weight_spec(layer, (tk, tn), lambda i, j: (0, j)))
        res_spec = pl.BlockSpec((tm, tn), lambda i, j: (i, j))
        out_spec = pl.BlockSpec((tm, tn), lambda i, j: (i, j))
        body = functools.partial(_mm_kernel_single, mode=mode, k_sizes=k_sizes)
        scratch, sem = [], ("parallel", "parallel")
    else:
        assert len(xs) == 1
        grid = (m // tm, n // tn, nk)
        in_specs = [pl.BlockSpec((tm, tk), lambda i, j, k: (i, k)),
                    _weight_spec(layer, (tk, tn), lambda i, j, k: (k, j))]
        res_spec = pl.BlockSpec((tm, tn), lambda i, j, k: (i, j))
        out_spec = pl.BlockSpec((tm, tn), lambda i, j, k: (i, j))
        body = functools.partial(_mm_kernel_multi, mode=mode)
        sem = ("parallel", "parallel", "arbitrary")
        if mode == "residual":
            assert out_dtype == F32
            scratch = []
            vmem += tm * tn * 4
        else:
            scratch = [pltpu.VMEM((tm, tn), F32)]
            vmem += tm * tn * 4
    args = list(xs) + [w]
    if mode == "residual":
        in_specs.append(res_spec)
        args.append(res)
        vmem += 2 * tm * tn * 4
    return pl.pallas_call(
        body,
        out_shape=jax.ShapeDtypeStruct((m, n), out_dtype),
        grid=grid,
        in_specs=in_specs,
        out_specs=out_spec,
        scratch_shapes=scratch,
        compiler_params=_cparams(sem, vmem),
        name="proj_" + mode,
    )(*args)


def _mm_kernel_stationary(*refs, mode, k_sizes, chunk_rows, scaled, norm_out, norm_dim):
    it = iter(refs)
    x_refs = [next(it) for _ in k_sizes]
    wchunk_ref = next(it)
    scale_ref = next(it) if scaled else None
    res_ref = next(it) if mode == "residual" else None
    gain_ref = next(it) if norm_out else None
    o_ref = next(it)
    hg_ref, scale_out_ref = (next(it), next(it)) if norm_out else (None, None)
    wbf_ref = next(it)
    ssq_ref = next(it) if norm_out else None
    j, i = pl.program_id(0), pl.program_id(1)
    last = pl.num_programs(0) - 1
    tm = o_ref.shape[0]

    def stage():
        r0 = pl.multiple_of(i * chunk_rows, chunk_rows)
        wbf_ref[j % 2, pl.ds(r0, chunk_rows), :] = wchunk_ref[...].astype(BF16)

    def multiply():
        w_tile = wbf_ref.at[(j + 1) % 2]
        acc = _split_dot(x_refs, k_sizes, lambda off, ks: w_tile[off:off + ks, :])
        out = _epilogue(acc, mode, res_ref, o_ref, scale_ref)
        if norm_out:
            hg_ref[...] = (out * gain_ref[...]).astype(hg_ref.dtype)
            sq = out * out
            tot = sq[:, 0:NORM_LANES]
            for c in range(1, sq.shape[1] // NORM_LANES):
                tot = tot + sq[:, c * NORM_LANES:(c + 1) * NORM_LANES]
            rows = pl.ds(pl.multiple_of(i * tm, tm), tm)
            tot = ssq_ref[rows, :] + tot
            ssq_ref[rows, :] = tot
            mean = jnp.sum(tot, axis=-1, keepdims=True) * (1.0 / norm_dim)
            scale_out_ref[...] = jnp.broadcast_to(lax.rsqrt(mean + EPS), scale_out_ref.shape)

    @pl.when(j == 0)
    def _():
        stage()
        if norm_out:
            ssq_ref[pl.ds(pl.multiple_of(i * tm, tm), tm), :] = jnp.zeros((tm, NORM_LANES), F32)

    @pl.when(jnp.logical_and(j > 0, j < last))
    def _():
        multiply()
        stage()

    @pl.when(j == last)
    def _():
        multiply()


def _matmul_stationary(x, w, layer, *, mode="none", res=None, out_dtype=F32, tm=1024, tn=1024, k_slab=None,
                       row_scale=None, norm_gain=None):
    xs = x if isinstance(x, (tuple, list)) else (x,)
    m = xs[0].shape[0]
    slab, n_slabs = (0, 1) if k_slab is None else k_slab
    assert len(xs) == 1 or n_slabs == 1
    k_sizes = tuple(xi.shape[1] // n_slabs for xi in xs)
    _, k_total, n = w.shape
    kdim = k_total // n_slabs
    assert sum(k_sizes) == kdim and m % tm == 0 and n % tn == 0
    n_i, n_j = m // tm, n // tn
    assert kdim % n_i == 0
    chunk_rows = kdim // n_i
    assert chunk_rows % 16 == 0
    out_bytes = jnp.dtype(out_dtype).itemsize

    def row_tile(j, i):
        return jnp.where(j == 0, 0, i)

    in_specs = [pl.BlockSpec((tm, ks), lambda j, i: (row_tile(j, i), slab)) for ks in k_sizes]
    in_specs.append(_weight_spec(layer, (chunk_rows, tn), lambda j, i: (slab * n_i + i, jnp.minimum(j, n_j - 1))))
    io_spec = pl.BlockSpec((tm, tn), lambda j, i: (row_tile(j, i), jnp.maximum(j - 1, 0)))
    args = list(xs) + [w]
    vmem = (2 * (tm * kdim * 2 + chunk_rows * tn * 4 + tm * tn * out_bytes) + 2 * kdim * tn * 2
            + tm * tn * 4 + (4 << 20))
    scaled, norm_out = row_scale is not None, norm_gain is not None
    if scaled:
        in_specs.append(pl.BlockSpec((tm, NORM_LANES), lambda j, i: (row_tile(j, i), 0)))
        args.append(row_scale)
        vmem += 2 * tm * NORM_LANES * 4 + tm * tn * 4
    if mode == "residual":
        in_specs.append(io_spec)
        args.append(res)
        vmem += 2 * tm * tn * 4
    out_shape = jax.ShapeDtypeStruct((m, n), out_dtype)
    out_specs = io_spec
    scratch_shapes = [pltpu.VMEM((2, kdim, tn), BF16)]
    if norm_out:
        assert out_dtype == F32
        in_specs.append(pl.BlockSpec((1, tn), lambda j, i: (0, jnp.maximum(j - 1, 0))))
        args.append(norm_gain.reshape(1, n))
        out_shape = (out_shape, jax.ShapeDtypeStruct((m, n), BF16), jax.ShapeDtypeStruct((m, NORM_LANES), F32))
        scale_spec = pl.BlockSpec((tm, NORM_LANES), lambda j, i: (jnp.where(j == n_j, i, 0), 0))
        out_specs = (io_spec, io_spec, scale_spec)
        scratch_shapes.append(pltpu.VMEM((m, NORM_LANES), F32))
        vmem += 2 * tm * (tn * 2 + NORM_LANES * 4) + m * NORM_LANES * 4 + 2 * tm * tn * 4
    return pl.pallas_call(
        functools.partial(_mm_kernel_stationary, mode=mode, k_sizes=k_sizes, chunk_rows=chunk_rows, scaled=scaled,
                          norm_out=norm_out, norm_dim=n),
        out_shape=out_shape,
        grid=(n_j + 1, n_i),
        in_specs=in_specs,
        out_specs=out_specs,
        scratch_shapes=scratch_shapes,
        compiler_params=_cparams(("arbitrary", "arbitrary"), vmem),
        name="projws_" + mode,
    )(*args)


def _chunk_pos(shape, chunk):
    return lax.broadcasted_iota(jnp.int32, shape, 0) % chunk


def _cumsum_in_chunks(x, chunk):
    pos = _chunk_pos(x.shape, chunk)
    shift = 1
    while shift < chunk:
        x = x + jnp.where(pos >= shift, pltpu.roll(x, shift, 0), 0.0)
        shift *= 2
    return x


def _causal_mask(chunk):
    r = lax.broadcasted_iota(jnp.int32, (chunk, chunk), 0)
    c = lax.broadcasted_iota(jnp.int32, (chunk, chunk), 1)
    return r >= c


_NT = (((1,), (1,)), ((), ()))
_TN = (((0,), (0,)), ((), ()))


def _lower_bound(logits, layer):
    rows = [logits[i:i + 1] for i in range(logits.shape[0])]
    mx = functools.reduce(jnp.maximum, rows)
    es = [jnp.exp(r - mx) for r in rows]
    den = functools.reduce(lambda a, b: a + b, es)
    lb = es[0] / den
    for i in range(1, layer + 1):
        lb = lb + es[i] / den
    return lb


def _hgrn_kernel(*refs, chunk, n_chunks, layer, has_state, heads):
    if has_state:
        q_ref, f_ref, v_ref, g_ref, lbl_ref, gn_ref, s0_ref, o_ref, sout_ref, st_ref = refs
    else:
        (q_ref, f_ref, v_ref, g_ref, lbl_ref, gn_ref, o_ref, sout_ref, st_ref), s0_ref = refs, None
    t = pl.program_id(2)
    last = t == pl.num_programs(2) - 1
    mask = _causal_mask(chunk)

    @pl.when(t == 0)
    def _():
        for hh in range(heads):
            if has_state:
                st_ref[hh] = s0_ref[0, hh].T
            else:
                st_ref[hh] = jnp.zeros(st_ref.shape[1:], F32)

    finals = []
    for hh in range(heads):
        ls = slice(hh * HEAD_LANES, (hh + 1) * HEAD_LANES)
        lb = _lower_bound(lbl_ref[:, ls], layer)
        f = lb + (1.0 - lb) * jax.nn.sigmoid(f_ref[:, ls])
        q = jax.nn.silu(q_ref[:, ls])
        k = 1.0 - f
        b = _cumsum_in_chunks(jnp.log(f), chunk)
        q_in = (q * jnp.exp(b)).astype(BF16)
        k_in = (k * jnp.exp(-b)).astype(BF16)
        v = v_ref[:, ls].astype(BF16)
        st = st_ref[hh]
        outs = []
        for c in range(n_chunks):
            sl = slice(c * chunk, (c + 1) * chunk)
            b_c = b[sl]
            b_last = b_c[chunk - 1:chunk]
            k_out = (k[sl] * jnp.exp(b_last - b_c)).astype(BF16)
            attn = lax.dot_general(q_in[sl], k_in[sl], _NT, preferred_element_type=F32)
            attn = jnp.where(mask, attn, 0.0).astype(BF16)
            o_intra = jnp.dot(attn, v[sl], preferred_element_type=F32)
            o_inter = lax.dot_general(q_in[sl], st.astype(BF16), _NT, preferred_element_type=F32)
            kv_t = lax.dot_general(v[sl], k_out, _TN, preferred_element_type=F32)
            st = jnp.exp(b_last) * st + kv_t
            outs.append(o_intra + o_inter)
        st_ref[hh] = st
        o = outs[0] if n_chunks == 1 else jnp.concatenate(outs, axis=0)
        o = o * lax.rsqrt(jnp.mean(o * o, axis=-1, keepdims=True) + EPS)
        o = o * gn_ref[:, ls] * jax.nn.silu(g_ref[:, ls])
        o_ref[:, ls] = o.astype(o_ref.dtype)
        finals.append(st)

    @pl.when(last)
    def _():
        for hh in range(heads):
            sout_ref[0, hh] = finals[hh].T


def _ret_kernel(*refs, chunk, n_chunks, has_state, heads, dv):
    if has_state:
        q_ref, k_ref, v_ref, g_ref, cos_ref, sin_ref, lg_ref, gn_ref, s0_ref, o_ref, sout_ref, st_ref = refs
    else:
        (q_ref, k_ref, v_ref, g_ref, cos_ref, sin_ref, lg_ref, gn_ref, o_ref, sout_ref, st_ref), s0_ref = refs, None
    t = pl.program_id(2)
    last = t == pl.num_programs(2) - 1
    cosf, sinf = cos_ref[...], sin_ref[...]
    half = HEAD_LANES // 2
    mask = _causal_mask(chunk)
    pos1 = (_chunk_pos(cosf.shape, chunk) + 1).astype(F32)

    @pl.when(t == 0)
    def _():
        for hh in range(heads):
            if has_state:
                st_ref[hh] = s0_ref[0, hh]
            else:
                st_ref[hh] = jnp.zeros(st_ref.shape[1:], F32)

    finals = []
    for hh in range(heads):
        ls = slice(hh * HEAD_LANES, (hh + 1) * HEAD_LANES)
        ws = slice(hh * dv, (hh + 1) * dv)
        q, k = q_ref[:, ls], k_ref[:, ls]
        qr = q * cosf + pltpu.roll(q, half, 1) * sinf
        kr = (k * cosf + pltpu.roll(k, half, 1) * sinf) * (HEAD_LANES ** -0.5)
        lg_wide = lg_ref[hh]
        lg = lg_wide[:, :HEAD_LANES]
        b = pos1 * lg
        b_last = float(chunk) * lg
        q_in = (qr * jnp.exp(b)).astype(BF16)
        k_in = (kr * jnp.exp(-b)).astype(BF16)
        k_out = (kr * jnp.exp(b_last - b)).astype(BF16)
        decay = jnp.exp(float(chunk) * lg_wide)
        v = v_ref[:, ws].astype(BF16)
        st = st_ref[hh]
        outs = []
        for c in range(n_chunks):
            sl = slice(c * chunk, (c + 1) * chunk)
            attn = lax.dot_general(q_in[sl], k_in[sl], _NT, preferred_element_type=F32)
            attn = jnp.where(mask, attn, 0.0).astype(BF16)
            o_intra = jnp.dot(attn, v[sl], preferred_element_type=F32)
            o_inter = jnp.dot(q_in[sl], st.astype(BF16), preferred_element_type=F32)
            kv = lax.dot_general(k_out[sl], v[sl], _TN, preferred_element_type=F32)
            st = decay * st + kv
            outs.append(o_intra + o_inter)
        st_ref[hh] = st
        o = outs[0] if n_chunks == 1 else jnp.concatenate(outs, axis=0)
        c0 = o - jnp.mean(o, axis=-1, keepdims=True)
        o = c0 * lax.rsqrt(jnp.mean(c0 * c0, axis=-1, keepdims=True) + EPS)
        o = o * gn_ref[:, ws] * jax.nn.silu(g_ref[:, ws])
        o_ref[:, ws] = o.astype(o_ref.dtype)
        finals.append(st)

    @pl.when(last)
    def _():
        for hh in range(heads):
            sout_ref[0, hh] = finals[hh]


def _rotary_tables(pos):
    half = HEAD_LANES // 2
    inv = 1.0 / (ROPE_BASE ** jnp.linspace(0.0, 1.0, half, dtype=F32))
    ang = pos.astype(F32)[:, None] * inv[None, :]
    cos, sin = jnp.cos(ang), jnp.sin(ang)
    return jnp.concatenate([cos, cos], axis=-1), jnp.concatenate([-sin, sin], axis=-1)


def _even_mixer(proj, batch, seq, pos, layer, s_a, s_b, a_shape, b_shape, lb_logits, g_a, g_b,
                rows_per_step=1024, heads_per_step=4):
    m = proj.shape[0]
    a_heads, dk, dv_a = a_shape
    b_heads, dk_b, dv_b = b_shape
    assert dk == HEAD_LANES and dv_a == HEAD_LANES and dk_b == HEAD_LANES
    chunk = min(CHUNK, seq)
    rows = min(rows_per_step, seq)
    assert seq % rows == 0 and rows % chunk == 0
    nt = seq // rows
    has_state = s_a is not None
    hp_a = a_heads if nt == 1 else heads_per_step
    hp_b = b_heads if nt == 1 else heads_per_step
    assert a_heads % hp_a == 0 and b_heads % hp_b == 0
    a_w, a_v = a_heads * dk, a_heads * dv_a
    b_qk, b_v = b_heads * HEAD_LANES, b_heads * dv_b
    off_b = 2 * a_w + 2 * a_v
    grid_sem = ("parallel", "parallel", "arbitrary")

    def cols(offset, width, hp):
        blk = hp * width
        assert offset % blk == 0
        return pl.BlockSpec((rows, blk), lambda b, h, t: (b * nt + t, offset // blk + h))

    n_layers = lb_logits.shape[0]
    wa = hp_a * HEAD_LANES
    in_specs = [cols(0, dk, hp_a), cols(a_w, dk, hp_a), cols(2 * a_w, dv_a, hp_a), cols(2 * a_w + a_v, dv_a, hp_a),
                pl.BlockSpec((n_layers, wa), lambda b, h, t: (0, h)),
                pl.BlockSpec((1, wa), lambda b, h, t: (0, h))]
    args = [proj, proj, proj, proj, lb_logits, g_a.reshape(1, a_v)]
    if has_state:
        in_specs.append(pl.BlockSpec((1, hp_a, dk, dv_a), lambda b, h, t: (b, h, 0, 0)))
        args.append(s_a)
    o_a, new_a = pl.pallas_call(
        functools.partial(_hgrn_kernel, chunk=chunk, n_chunks=rows // chunk, layer=layer, has_state=has_state,
                          heads=hp_a),
        out_shape=(jax.ShapeDtypeStruct((m, a_v), BF16),
                   jax.ShapeDtypeStruct((batch, a_heads, dk, dv_a), F32)),
        grid=(batch, a_heads // hp_a, nt),
        in_specs=in_specs,
        out_specs=(pl.BlockSpec((rows, wa), lambda b, h, t: (b * nt + t, h)),
                   pl.BlockSpec((1, hp_a, dk, dv_a), lambda b, h, t: (b, h, 0, 0))),
        scratch_shapes=[pltpu.VMEM((hp_a, dv_a, dk), F32)],
        compiler_params=_cparams(grid_sem, 48 << 20),
        name="hgrn2",
    )(*args)

    cosf, sinf = _rotary_tables(pos)
    log_gamma = jnp.log(1.0 - jnp.exp2(-5.0 - jnp.arange(b_heads, dtype=F32)))
    lg = jnp.broadcast_to(log_gamma[:, None, None], (b_heads, 1, dv_b))
    wv = hp_b * dv_b
    in_specs = [cols(off_b, HEAD_LANES, hp_b), cols(off_b + b_qk, HEAD_LANES, hp_b),
                cols(off_b + 2 * b_qk, dv_b, hp_b), cols(off_b + 2 * b_qk + b_v, dv_b, hp_b),
                pl.BlockSpec((rows, HEAD_LANES), lambda b, h, t: (t, 0)),
                pl.BlockSpec((rows, HEAD_LANES), lambda b, h, t: (t, 0)),
                pl.BlockSpec((hp_b, 1, dv_b), lambda b, h, t: (h, 0, 0)),
                pl.BlockSpec((1, wv), lambda b, h, t: (0, h))]
    args = [proj, proj, proj, proj, cosf, sinf, lg, g_b.reshape(1, b_v)]
    if has_state:
        in_specs.append(pl.BlockSpec((1, hp_b, HEAD_LANES, dv_b), lambda b, h, t: (b, h, 0, 0)))
        args.append(s_b)
    o_b, new_b = pl.pallas_call(
        functools.partial(_ret_kernel, chunk=chunk, n_chunks=rows // chunk, has_state=has_state, heads=hp_b,
                          dv=dv_b),
        out_shape=(jax.ShapeDtypeStruct((m, b_v), BF16),
                   jax.ShapeDtypeStruct((batch, b_heads, HEAD_LANES, dv_b), F32)),
        grid=(batch, b_heads // hp_b, nt),
        in_specs=in_specs,
        out_specs=(pl.BlockSpec((rows, wv), lambda b, h, t: (b * nt + t, h)),
                   pl.BlockSpec((1, hp_b, HEAD_LANES, dv_b), lambda b, h, t: (b, h, 0, 0))),
        scratch_shapes=[pltpu.VMEM((hp_b, HEAD_LANES, dv_b), F32)],
        compiler_params=_cparams(grid_sem, 48 << 20),
        name="retention",
    )(*args)
    return o_a, o_b, new_a, new_b


SUBLANES = 8


def _rglru_block(xb, yb, pos_base, rows, first_pos_zero, cw_ref, cb_ref, wa_ref, wx_ref, gab_ref, gxb_ref, lam_ref,
                 prev_ref, h_ref, a_ref, b_ref):
    width = cw_ref.shape[0]
    assert width - 1 <= SUBLANES <= rows
    grouped = (rows // SUBLANES, SUBLANES, xb.shape[1])
    sub = lax.broadcasted_iota(jnp.int32, grouped, 1)
    x3 = xb.reshape(grouped)
    prev = prev_ref[...]
    cw = cw_ref[...]
    acc = None
    for j in range(width):
        back = width - 1 - j
        if back == 0:
            shifted = x3
        else:
            rot = pltpu.roll(x3, back, 1)
            rot_before = jnp.concatenate([pltpu.roll(prev, back, 0)[None], rot[:-1]], axis=0)
            shifted = jnp.where(sub >= back, rot, rot_before)
        term = shifted * cw[j:j + 1]
        acc = term if acc is None else acc + term
    conv = (cb_ref[...] + acc).reshape(xb.shape)
    prev_ref[...] = xb[rows - SUBLANES:rows]
    tail = xb[rows - (width - 1):rows]

    conv_bf = conv.astype(BF16)
    r = jax.nn.sigmoid(jnp.dot(conv_bf, wa_ref[0], preferred_element_type=F32) + gab_ref[...])
    i = jax.nn.sigmoid(jnp.dot(conv_bf, wx_ref[0], preferred_element_type=F32) + gxb_ref[...])
    log_a = -RG_C * r * jax.nn.softplus(-lam_ref[...])
    a = jnp.exp(log_a)
    gap = 1.0 - a * a
    mult = jnp.where(gap == 0.0, 0.0, gap * lax.rsqrt(gap))
    gated = i * conv
    bb = mult * gated
    if first_pos_zero:
        row = lax.broadcasted_iota(jnp.int32, (SUBLANES, a.shape[1]), 0)
        top = jnp.where(row + pos_base == 0, gated[:SUBLANES], bb[:SUBLANES])
        bb = jnp.concatenate([top, bb[SUBLANES:]], axis=0)

    a3, b3 = a.reshape(grouped), bb.reshape(grouped)
    shift = 1
    while shift < SUBLANES:
        take = sub >= shift
        b3 = jnp.where(take, a3 * pltpu.roll(b3, shift, 1) + b3, b3)
        a3 = jnp.where(take, a3 * pltpu.roll(a3, shift, 1), a3)
        shift *= 2
    a_ref[...] = a3.reshape(a.shape)
    b_ref[...] = b3.reshape(a.shape)

    def group(gidx, carry):
        r0 = pl.multiple_of(gidx * SUBLANES, SUBLANES)
        hg = a_ref[pl.ds(r0, SUBLANES), :] * carry + b_ref[pl.ds(r0, SUBLANES), :]
        b_ref[pl.ds(r0, SUBLANES), :] = hg
        return hg[SUBLANES - 1:SUBLANES, :]

    h_ref[...] = lax.fori_loop(0, rows // SUBLANES, group, h_ref[...], unroll=True)
    return jax.nn.gelu(yb, approximate=True) * b_ref[...], tail


def _griffin_kernel(*refs, rows, streams, first_pos_zero, has_state):
    if has_state:
        (xb_ref, yb_ref, cw_ref, cb_ref, wa_ref, wx_ref, gab_ref, gxb_ref, lam_ref, sconv_ref, h0_ref,
         o_ref, hlast_ref, nconv_ref, xpad_ref, h_ref, a_ref, b_ref) = refs
    else:
        (xb_ref, yb_ref, cw_ref, cb_ref, wa_ref, wx_ref, gab_ref, gxb_ref, lam_ref,
         o_ref, hlast_ref, nconv_ref, xpad_ref, h_ref, a_ref, b_ref) = refs
        sconv_ref = h0_ref = None
    t = pl.program_id(2)

    @pl.when(t == 0)
    def _():
        for q in range(streams):
            xpad_ref[q] = jnp.zeros(xpad_ref.shape[1:], F32)
            if has_state:
                xpad_ref[q, SUBLANES - sconv_ref.shape[1]:SUBLANES, :] = sconv_ref[q]
                h_ref[q] = h0_ref[q]
            else:
                h_ref[q] = jnp.zeros(h_ref.shape[1:], F32)

    tails = []
    for q in range(streams):
        rs = slice(q * rows, (q + 1) * rows)
        o, tail = _rglru_block(xb_ref[rs, :], yb_ref[rs, :], t * rows, rows, first_pos_zero, cw_ref, cb_ref, wa_ref,
                               wx_ref, gab_ref, gxb_ref, lam_ref, xpad_ref.at[q], h_ref.at[q], a_ref.at[q],
                               b_ref.at[q])
        o_ref[rs, :] = o.astype(o_ref.dtype)
        tails.append(tail)

    @pl.when(t == pl.num_programs(2) - 1)
    def _():
        for q in range(streams):
            hlast_ref[q] = h_ref[q]
            nconv_ref[q] = tails[q]


def _pair_blockdiag(w, pair):
    n, c, _ = w.shape
    w = w.reshape(n // pair, pair, c, c)
    rows = []
    for p in range(pair):
        blocks = [w[:, p] if q == p else jnp.zeros_like(w[:, p]) for q in range(pair)]
        rows.append(jnp.concatenate(blocks, axis=-1))
    return jnp.concatenate(rows, axis=-2)


def _odd_mixer(xy, batch, seq, pos0, h0, s_conv, conv_w, conv_b, wa, wx, ga_b, gx_b, lam,
               rows_per_step=1024, short_streams_per_step=8):
    m = xy.shape[0]
    d_rnn = conv_w.shape[1]
    width = conv_w.shape[0]
    blk = wa.shape[1]
    nblk = d_rnn // blk
    rows = min(rows_per_step, seq)
    assert seq % rows == 0 and rows % SUBLANES == 0 and rows >= width - 1
    nt = seq // rows
    streams = short_streams_per_step if (nt == 1 and batch % short_streams_per_step == 0) else 1
    has_state = h0 is not None

    def vec(a):
        return a.reshape(1, d_rnn)

    def vspec():
        return pl.BlockSpec((1, blk), lambda b, j, t: (0, j))

    def rows_spec(col0):
        return pl.BlockSpec((streams * rows, blk), lambda b, j, t: (b * nt + t, col0 + j))

    def state_spec(n):
        return pl.BlockSpec((streams, n, blk), lambda b, j, t: (b, 0, j))

    in_specs = [rows_spec(0), rows_spec(nblk),
                pl.BlockSpec((width, blk), lambda b, j, t: (0, j)),
                vspec(),
                pl.BlockSpec((1, blk, blk), lambda b, j, t: (j, 0, 0)),
                pl.BlockSpec((1, blk, blk), lambda b, j, t: (j, 0, 0)),
                vspec(), vspec(), vspec()]
    args = [xy, xy, conv_w, vec(conv_b), wa, wx, vec(ga_b), vec(gx_b), vec(lam)]
    if has_state:
        in_specs += [state_spec(width - 1), state_spec(1)]
        args += [s_conv, h0.reshape(batch, 1, d_rnn)]
    o, h_last, new_conv = pl.pallas_call(
        functools.partial(_griffin_kernel, rows=rows, streams=streams, first_pos_zero=(pos0 == 0),
                          has_state=has_state),
        out_shape=(jax.ShapeDtypeStruct((m, d_rnn), BF16),
                   jax.ShapeDtypeStruct((batch, 1, d_rnn), F32),
                   jax.ShapeDtypeStruct((batch, width - 1, d_rnn), F32)),
        grid=(batch // streams, nblk, nt),
        in_specs=in_specs,
        out_specs=(rows_spec(0), state_spec(1), state_spec(width - 1)),
        scratch_shapes=[pltpu.VMEM((streams, SUBLANES, blk), F32),
                        pltpu.VMEM((streams, 1, blk), F32),
                        pltpu.VMEM((streams, rows, blk), F32),
                        pltpu.VMEM((streams, rows, blk), F32)],
        compiler_params=_cparams(("parallel", "parallel", "arbitrary"), 32 << 20),
        name="griffin",
    )(*args)
    return o, h_last.reshape(batch, d_rnn), new_conv


def _trunk(x, pos0, states, p):
    batch, seq, d = x.shape
    depth = p["norm_mix"].shape[0]
    pos = pos0 + jnp.arange(seq, dtype=jnp.int32)
    h = x.reshape(batch * seq, d)
    many_rows = h.shape[0] >= 8 * 1024

    def project(xin, w, layer, row_scale=None, **kw):
        if many_rows:
            return _matmul_stationary(xin, w, layer, tn=(512 if w.shape[1] > 4096 else 1024), row_scale=row_scale,
                                      **kw)
        assert row_scale is None
        return _matmul(xin, w, layer, tn=512, tk=w.shape[1], **kw)

    def residual_and_norm(xin, w, layer, res, gain):
        if many_rows:
            return _matmul_stationary(xin, w, layer, mode="residual", res=res, tn=512, norm_gain=gain)
        hh = project(xin, w, layer, mode="residual", res=res)
        return hh, _rms_norm(hh, gain, BF16), None

    n_hgrn, n_ret, n_rg, n_conv = [], [], [], []
    hn, scale = _rms_norm(h, p["norm_mix"][0], BF16), None
    for l in range(depth):
        if l % 2 == 0:
            e = l // 2
            proj = project(hn, p["w_in_even"], e, row_scale=scale)
            s_a, s_b = (None, None) if states is None else (states[0][e], states[1][e])
            o_a, o_b, sa, sb = _even_mixer(proj, batch, seq, pos, l, s_a, s_b, p["hgrn_shape"], p["ret_shape"],
                                           p["hgrn_lb_logits"], p["hgrn_norm"][e], p["ret_norm"][e])
            n_hgrn.append(sa)
            n_ret.append(sb)
            h, hm, scale_m = residual_and_norm((o_a, o_b), p["w_out_even"], e, h, p["norm_mlp"][l])
        else:
            od = l // 2
            xy = project(hn, p["w_in_odd"], od, row_scale=scale)
            h0, sc = (None, None) if states is None else (states[2][od], states[3][od])
            o, sh, scn = _odd_mixer(xy, batch, seq, pos0, h0, sc, p["conv_w"][od], p["conv_b"][od],
                                    p["gate_a_w"][od], p["gate_x_w"][od], p["gate_a_b"][od], p["gate_x_b"][od],
                                    p["rglru_lambda"][od])
            n_rg.append(sh)
            n_conv.append(scn)
            h, hm, scale_m = residual_and_norm(o, p["w_out_odd"], od, h, p["norm_mlp"][l])
        up = project(hm, p["w_up"], l, row_scale=scale_m, mode="relu2", out_dtype=BF16)
        more = l + 1 < depth
        if many_rows:
            n_slabs = p["w_down"].shape[1] // d
            for s in range(n_slabs - 1):
                h = _matmul_stationary(up, p["w_down"], l, mode="residual", res=h, k_slab=(s, n_slabs))
            last_slab = dict(mode="residual", res=h, k_slab=(n_slabs - 1, n_slabs))
            if more:
                h, hn, scale = _matmul_stationary(up, p["w_down"], l, tn=512, norm_gain=p["norm_mix"][l + 1],
                                                  **last_slab)
            else:
                h = _matmul_stationary(up, p["w_down"], l, **last_slab)
        else:
            h = _matmul(up, p["w_down"], l, mode="residual", res=h, tn=512, tk=4096)
            if more:
                hn, scale = _rms_norm(h, p["norm_mix"][l + 1], BF16), None
    out = _rms_norm(h, p["norm_final"], x.dtype).reshape(batch, seq, d)
    return out, jnp.stack(n_hgrn), jnp.stack(n_ret), jnp.stack(n_rg), jnp.stack(n_conv)


def kernel(x_prompt, x_sample, state_hgrn, state_ret, state_rglru, state_conv, norm_mix, norm_mlp, norm_final,
           w_in_even, hgrn_lb_logits, hgrn_norm, ret_norm, w_out_even, w_in_odd, conv_w, conv_b, gate_a_w,
           gate_a_b, gate_x_w, gate_x_b, rglru_lambda, w_out_odd, w_up, w_down):
    past_len = 4096

    p = dict(
        norm_mix=norm_mix, norm_mlp=norm_mlp, norm_final=norm_final,
        w_in_even=w_in_even, w_out_even=w_out_even, w_in_odd=w_in_odd, w_out_odd=w_out_odd, w_up=w_up,
        w_down=w_down,
        hgrn_lb_logits=hgrn_lb_logits, hgrn_norm=hgrn_norm, ret_norm=ret_norm,
        conv_w=conv_w, conv_b=conv_b,
        gate_a_w=jnp.stack([_pair_blockdiag(w, RG_PAIR) for w in gate_a_w.astype(BF16)]),
        gate_x_w=jnp.stack([_pair_blockdiag(w, RG_PAIR) for w in gate_x_w.astype(BF16)]),
        gate_a_b=gate_a_b, gate_x_b=gate_x_b, rglru_lambda=rglru_lambda,
        hgrn_shape=state_hgrn.shape[2:], ret_shape=state_ret.shape[2:],
    )
    y_p, hg_p, rt_p, rg_p, cv_p = _trunk(x_prompt, 0, None, p)
    y_s, hg_s, rt_s, rg_s, cv_s = _trunk(x_sample, past_len, (state_hgrn, state_ret, state_rglru, state_conv), p)
    return (y_p, y_s, hg_p, rt_p, rg_p, cv_p, hg_s, rt_s, rg_s, cv_s)
```

```python
import functools

import jax
import jax.numpy as jnp
from jax import lax
from jax.experimental import pallas as pl
from jax.experimental.pallas import tpu as pltpu

EPS = 1e-6
CHUNK = 64
ROPE_BASE = 10000.0
RG_C = 8.0
HEAD_LANES = 128
RG_PAIR = 2

V7X_VMEM_BYTES = 64 * 1024 * 1024
VMEM_CAP_BYTES = V7X_VMEM_BYTES - 6 * 1024 * 1024

F32 = jnp.float32
BF16 = jnp.bfloat16


def _cparams(semantics, vmem_bytes):
    return pltpu.CompilerParams(dimension_semantics=semantics,
                                vmem_limit_bytes=int(min(max(vmem_bytes, 32 * 1024 * 1024), VMEM_CAP_BYTES)))


def _rms_kernel(x_ref, g_ref, o_ref):
    x = x_ref[...]
    y = x * lax.rsqrt(jnp.mean(x * x, axis=-1, keepdims=True) + EPS)
    o_ref[...] = (y * g_ref[...]).astype(o_ref.dtype)


def _rms_norm(x, g, out_dtype, tm=512):
    m, d = x.shape
    tm = min(tm, m)
    io_bytes = 2 * tm * d * (4 + jnp.dtype(out_dtype).itemsize)
    return pl.pallas_call(
        _rms_kernel,
        out_shape=jax.ShapeDtypeStruct((m, d), out_dtype),
        grid=(m // tm,),
        in_specs=[pl.BlockSpec((tm, d), lambda i: (i, 0)),
                  pl.BlockSpec((1, d), lambda i: (0, 0))],
        out_specs=pl.BlockSpec((tm, d), lambda i: (i, 0)),
        compiler_params=_cparams(("parallel",), io_bytes + 3 * tm * d * 4),
        name="rms_norm",
    )(x, g.reshape(1, d))


NORM_LANES = 128


def _epilogue(acc, mode, res_ref, o_ref, scale_ref=None):
    if scale_ref is not None:
        acc = acc * jnp.tile(scale_ref[...], (1, acc.shape[1] // NORM_LANES))
    if mode == "relu2":
        acc = jnp.square(jnp.maximum(acc, 0.0))
    elif mode == "residual":
        acc = res_ref[...] + acc
    o_ref[...] = acc.astype(o_ref.dtype)
    return acc


def _split_dot(x_refs, k_sizes, w_rows):
    acc, off = None, 0
    for x_ref, ks in zip(x_refs, k_sizes):
        part = jnp.dot(x_ref[...], w_rows(off, ks), preferred_element_type=F32)
        acc = part if acc is None else acc + part
        off += ks
    return acc


def _mm_kernel_single(*refs, mode, k_sizes):
    n_x = len(k_sizes)
    x_refs, w_ref = refs[:n_x], refs[n_x]
    if mode == "residual":
        res_ref, o_ref = refs[n_x + 1:]
    else:
        (o_ref,), res_ref = refs[n_x + 1:], None
    acc = _split_dot(x_refs, k_sizes, lambda off, ks: w_ref[off:off + ks, :].astype(BF16))
    _epilogue(acc, mode, res_ref, o_ref)


def _mm_kernel_multi(*refs, mode):
    if mode == "residual":
        x_ref, w_ref, res_ref, o_ref = refs
    else:
        x_ref, w_ref, o_ref, acc_ref = refs
    k = pl.program_id(2)
    part = jnp.dot(x_ref[...], w_ref[...].astype(BF16), preferred_element_type=F32)

    if mode == "residual":
        @pl.when(k == 0)
        def _():
            o_ref[...] = res_ref[...] + part

        @pl.when(k > 0)
        def _():
            o_ref[...] += part
    else:
        @pl.when(k == 0)
        def _():
            acc_ref[...] = part

        @pl.when(k > 0)
        def _():
            acc_ref[...] += part

        @pl.when(k == pl.num_programs(2) - 1)
        def _():
            _epilogue(acc_ref[...], mode, None, o_ref)


def _weight_spec(layer, block, index_map):
    return pl.BlockSpec((None,) + block, lambda *g: (layer,) + tuple(index_map(*g)))


def _matmul(x, w, layer, *, mode="none", res=None, out_dtype=F32, tm=1024, tn=1024, tk=4096):
    xs = x if isinstance(x, (tuple, list)) else (x,)
    m = xs[0].shape[0]
    k_sizes = tuple(xi.shape[1] for xi in xs)
    _, kdim, n = w.shape
    assert sum(k_sizes) == kdim
    tm, tn, tk = min(tm, m), min(tn, n), min(tk, kdim)
    nk = kdim // tk
    assert m % tm == 0 and n % tn == 0 and kdim % tk == 0
    out_bytes = jnp.dtype(out_dtype).itemsize
    w_bytes = jnp.dtype(w.dtype).itemsize
    vmem = 2 * (tm * tk * 2 + tk * tn * w_bytes + tm * tn * out_bytes) + tm * tn * 4 + (4 << 20)
    if w_bytes != 2:
        vmem += tk * tn * 2
    if nk == 1:
        grid = (m // tm, n // tn)
        in_specs = [pl.BlockSpec((tm, ks), lambda i, j: (i, 0)) for ks in k_sizes]
        in_specs.append(_weight_spec(layer, (tk, tn), lambda i, j: (0, j)))
        res_spec = pl.BlockSpec((tm, tn), lambda i, j: (i, j))
        out_spec = pl.BlockSpec((tm, tn), lambda i, j: (i, j))
        body = functools.partial(_mm_kernel_single, mode=mode, k_sizes=k_sizes)
        scratch, sem = [], ("parallel", "parallel")
    else:
        assert len(xs) == 1
        grid = (m // tm, n // tn, nk)
        in_specs = [pl.BlockSpec((tm, tk), lambda i, j, k: (i, k)),
                    _weight_spec(layer, (tk, tn), lambda i, j, k: (k, j))]
        res_spec = pl.BlockSpec((tm, tn), lambda i, j, k: (i, j))
        out_spec = pl.BlockSpec((tm, tn), lambda i, j, k: (i, j))
        body = functools.partial(_mm_kernel_multi, mode=mode)
        sem = ("parallel", "parallel", "arbitrary")
        if mode == "residual":
            assert out_dtype == F32
            scratch = []
            vmem += tm * tn * 4
        else:
            scratch = [pltpu.VMEM((tm, tn), F32)]
            vmem += tm * tn * 4
    args = list(xs) + [w]
    if mode == "residual":
        in_specs.append(res_spec)
        args.append(res)
        vmem += 2 * tm * tn * 4
    return pl.pallas_call(
        body,
        out_shape=jax.ShapeDtypeStruct((m, n), out_dtype),
        grid=grid,
        in_specs=in_specs,
        out_specs=out_spec,
        scratch_shapes=scratch,
        compiler_params=_cparams(sem, vmem),
        name="proj_" + mode,
    )(*args)


def _mm_kernel_stationary(*refs, mode, k_sizes, chunk_rows, scaled, norm_out, norm_dim):
    it = iter(refs)
    x_refs = [next(it) for _ in k_sizes]
    wchunk_ref = next(it)
    scale_ref = next(it) if scaled else None
    res_ref = next(it) if mode == "residual" else None
    gain_ref = next(it) if norm_out else None
    o_ref = next(it)
    hg_ref, scale_out_ref = (next(it), next(it)) if norm_out else (None, None)
    wbf_ref = next(it)
    ssq_ref = next(it) if norm_out else None
    j, i = pl.program_id(0), pl.program_id(1)
    last = pl.num_programs(0) - 1
    tm = o_ref.shape[0]

    def stage():
        r0 = pl.multiple_of(i * chunk_rows, chunk_rows)
        wbf_ref[j % 2, pl.ds(r0, chunk_rows), :] = wchunk_ref[...].astype(BF16)

    def multiply(final=False):
        w_tile = wbf_ref.at[(j + 1) % 2]
        acc = _split_dot(x_refs, k_sizes, lambda off, ks: w_tile[off:off + ks, :])
        out = _epilogue(acc, mode, res_ref, o_ref, scale_ref)
        if norm_out:
            hg_ref[...] = (out * gain_ref[...]).astype(hg_ref.dtype)
            sq = out * out
            tot = sq[:, 0:NORM_LANES]
            for c in range(1, sq.shape[1] // NORM_LANES):
                tot = tot + sq[:, c * NORM_LANES:(c + 1) * NORM_LANES]
            rows = pl.ds(pl.multiple_of(i * tm, tm), tm)
            tot = ssq_ref[rows, :] + tot
            if final:
                mean = jnp.sum(tot, axis=-1, keepdims=True) * (1.0 / norm_dim)
                scale_out_ref[...] = jnp.broadcast_to(lax.rsqrt(mean + EPS), scale_out_ref.shape)
            else:
                ssq_ref[rows, :] = tot

    @pl.when(j == 0)
    def _():
        stage()
        if norm_out:
            ssq_ref[pl.ds(pl.multiple_of(i * tm, tm), tm), :] = jnp.zeros((tm, NORM_LANES), F32)

    @pl.when(jnp.logical_and(j > 0, j < last))
    def _():
        multiply()
        stage()

    @pl.when(j == last)
    def _():
        multiply(final=True)


def _matmul_stationary(x, w, layer, *, mode="none", res=None, out_dtype=F32, tm=1024, tn=1024, k_slab=None,
                       row_scale=None, norm_gain=None):
    xs = x if isinstance(x, (tuple, list)) else (x,)
    m = xs[0].shape[0]
    slab, n_slabs = (0, 1) if k_slab is None else k_slab
    assert len(xs) == 1 or n_slabs == 1
    k_sizes = tuple(xi.shape[1] // n_slabs for xi in xs)
    _, k_total, n = w.shape
    kdim = k_total // n_slabs
    assert sum(k_sizes) == kdim and m % tm == 0 and n % tn == 0
    n_i, n_j = m // tm, n // tn
    assert kdim % n_i == 0
    chunk_rows = kdim // n_i
    assert chunk_rows % 16 == 0
    out_bytes = jnp.dtype(out_dtype).itemsize

    def row_tile(j, i):
        return jnp.where(j == 0, 0, i)

    in_specs = [pl.BlockSpec((tm, ks), lambda j, i: (row_tile(j, i), slab)) for ks in k_sizes]
    in_specs.append(_weight_spec(layer, (chunk_rows, tn), lambda j, i: (slab * n_i + i, jnp.minimum(j, n_j - 1))))
    io_spec = pl.BlockSpec((tm, tn), lambda j, i: (row_tile(j, i), jnp.maximum(j - 1, 0)))
    args = list(xs) + [w]
    vmem = (2 * (tm * kdim * 2 + chunk_rows * tn * 4 + tm * tn * out_bytes) + 2 * kdim * tn * 2
            + tm * tn * 4 + (4 << 20))
    scaled, norm_out = row_scale is not None, norm_gain is not None
    if scaled:
        in_specs.append(pl.BlockSpec((tm, NORM_LANES), lambda j, i: (row_tile(j, i), 0)))
        args.append(row_scale)
        vmem += 2 * tm * NORM_LANES * 4 + tm * tn * 4
    if mode == "residual":
        in_specs.append(io_spec)
        args.append(res)
        vmem += 2 * tm * tn * 4
    out_shape = jax.ShapeDtypeStruct((m, n), out_dtype)
    out_specs = io_spec
    scratch_shapes = [pltpu.VMEM((2, kdim, tn), BF16)]
    if norm_out:
        assert out_dtype == F32
        in_specs.append(pl.BlockSpec((1, tn), lambda j, i: (0, jnp.maximum(j - 1, 0))))
        args.append(norm_gain.reshape(1, n))
        out_shape = (out_shape, jax.ShapeDtypeStruct((m, n), BF16), jax.ShapeDtypeStruct((m, NORM_LANES), F32))
        scale_spec = pl.BlockSpec((tm, NORM_LANES), lambda j, i: (jnp.where(j == n_j, i, 0), 0))
        out_specs = (io_spec, io_spec, scale_spec)
        scratch_shapes.append(pltpu.VMEM((m, NORM_LANES), F32))
        vmem += 2 * tm * (tn * 2 + NORM_LANES * 4) + m * NORM_LANES * 4 + 2 * tm * tn * 4
    return pl.pallas_call(
        functools.partial(_mm_kernel_stationary, mode=mode, k_sizes=k_sizes, chunk_rows=chunk_rows, scaled=scaled,
                          norm_out=norm_out, norm_dim=n),
        out_shape=out_shape,
        grid=(n_j + 1, n_i),
        in_specs=in_specs,
        out_specs=out_specs,
        scratch_shapes=scratch_shapes,
        compiler_params=_cparams(("arbitrary", "arbitrary"), vmem),
        name="projws_" + mode,
    )(*args)


def _chunk_pos(shape, chunk):
    return lax.broadcasted_iota(jnp.int32, shape, 0) % chunk


def _cumsum_in_chunks(x, chunk):
    pos = _chunk_pos(x.shape, chunk)
    shift = 1
    while shift < chunk:
        x = x + jnp.where(pos >= shift, pltpu.roll(x, shift, 0), 0.0)
        shift *= 2
    return x


def _causal_mask(chunk):
    r = lax.broadcasted_iota(jnp.int32, (chunk, chunk), 0)
    c = lax.broadcasted_iota(jnp.int32, (chunk, chunk), 1)
    return r >= c


_NT = (((1,), (1,)), ((), ()))
_TN = (((0,), (0,)), ((), ()))


def _lower_bound(logits, layer):
    rows = [logits[i:i + 1] for i in range(logits.shape[0])]
    mx = functools.reduce(jnp.maximum, rows)
    es = [jnp.exp(r - mx) for r in rows]
    den = functools.reduce(lambda a, b: a + b, es)
    lb = es[0] / den
    for i in range(1, layer + 1):
        lb = lb + es[i] / den
    return lb


def _hgrn_kernel(*refs, chunk, n_chunks, layer, has_state, heads):
    if has_state:
        q_ref, f_ref, v_ref, g_ref, lbl_ref, gn_ref, s0_ref, o_ref, sout_ref, st_ref = refs
    else:
        (q_ref, f_ref, v_ref, g_ref, lbl_ref, gn_ref, o_ref, sout_ref, st_ref), s0_ref = refs, None
    t = pl.program_id(2)
    last = t == pl.num_programs(2) - 1
    mask = _causal_mask(chunk)

    @pl.when(t == 0)
    def _():
        for hh in range(heads):
            if has_state:
                st_ref[hh] = s0_ref[0, hh].T
            else:
                st_ref[hh] = jnp.zeros(st_ref.shape[1:], F32)

    finals = []
    for hh in range(heads):
        ls = slice(hh * HEAD_LANES, (hh + 1) * HEAD_LANES)
        lb = _lower_bound(lbl_ref[:, ls], layer)
        f = lb + (1.0 - lb) * jax.nn.sigmoid(f_ref[:, ls])
        q = jax.nn.silu(q_ref[:, ls])
        k = 1.0 - f
        b = _cumsum_in_chunks(jnp.log(f), chunk)
        q_in = (q * jnp.exp(b)).astype(BF16)
        k_in = (k * jnp.exp(-b)).astype(BF16)
        v = v_ref[:, ls].astype(BF16)
        st = st_ref[hh]
        outs = []
        for c in range(n_chunks):
            sl = slice(c * chunk, (c + 1) * chunk)
            b_c = b[sl]
            b_last = b_c[chunk - 1:chunk]
            k_out = (k[sl] * jnp.exp(b_last - b_c)).astype(BF16)
            attn = lax.dot_general(q_in[sl], k_in[sl], _NT, preferred_element_type=F32)
            attn = jnp.where(mask, attn, 0.0).astype(BF16)
            o_intra = jnp.dot(attn, v[sl], preferred_element_type=F32)
            o_inter = lax.dot_general(q_in[sl], st.astype(BF16), _NT, preferred_element_type=F32)
            kv_t = lax.dot_general(v[sl], k_out, _TN, preferred_element_type=F32)
            st = jnp.exp(b_last) * st + kv_t
            outs.append(o_intra + o_inter)
        st_ref[hh] = st
        o = outs[0] if n_chunks == 1 else jnp.concatenate(outs, axis=0)
        o = o * lax.rsqrt(jnp.mean(o * o, axis=-1, keepdims=True) + EPS)
        o = o * gn_ref[:, ls] * jax.nn.silu(g_ref[:, ls])
        o_ref[:, ls] = o.astype(o_ref.dtype)
        finals.append(st)

    @pl.when(last)
    def _():
        for hh in range(heads):
            sout_ref[0, hh] = finals[hh].T


def _ret_kernel(*refs, chunk, n_chunks, has_state, heads, dv):
    if has_state:
        q_ref, k_ref, v_ref, g_ref, cos_ref, sin_ref, lg_ref, gn_ref, s0_ref, o_ref, sout_ref, st_ref = refs
    else:
        (q_ref, k_ref, v_ref, g_ref, cos_ref, sin_ref, lg_ref, gn_ref, o_ref, sout_ref, st_ref), s0_ref = refs, None
    t = pl.program_id(2)
    last = t == pl.num_programs(2) - 1
    cosf, sinf = cos_ref[...], sin_ref[...]
    half = HEAD_LANES // 2
    mask = _causal_mask(chunk)
    pos1 = (_chunk_pos(cosf.shape, chunk) + 1).astype(F32)

    @pl.when(t == 0)
    def _():
        for hh in range(heads):
            if has_state:
                st_ref[hh] = s0_ref[0, hh]
            else:
                st_ref[hh] = jnp.zeros(st_ref.shape[1:], F32)

    finals = []
    for hh in range(heads):
        ls = slice(hh * HEAD_LANES, (hh + 1) * HEAD_LANES)
        ws = slice(hh * dv, (hh + 1) * dv)
        q, k = q_ref[:, ls], k_ref[:, ls]
        qr = q * cosf + pltpu.roll(q, half, 1) * sinf
        kr = (k * cosf + pltpu.roll(k, half, 1) * sinf) * (HEAD_LANES ** -0.5)
        lg_wide = lg_ref[hh]
        lg = lg_wide[:, :HEAD_LANES]
        b = pos1 * lg
        b_last = float(chunk) * lg
        q_in = (qr * jnp.exp(b)).astype(BF16)
        k_in = (kr * jnp.exp(-b)).astype(BF16)
        k_out = (kr * jnp.exp(b_last - b)).astype(BF16)
        decay = jnp.exp(float(chunk) * lg_wide)
        v = v_ref[:, ws].astype(BF16)
        st = st_ref[hh]
        outs = []
        for c in range(n_chunks):
            sl = slice(c * chunk, (c + 1) * chunk)
            attn = lax.dot_general(q_in[sl], k_in[sl], _NT, preferred_element_type=F32)
            attn = jnp.where(mask, attn, 0.0).astype(BF16)
            o_intra = jnp.dot(attn, v[sl], preferred_element_type=F32)
            o_inter = jnp.dot(q_in[sl], st.astype(BF16), preferred_element_type=F32)
            kv = lax.dot_general(k_out[sl], v[sl], _TN, preferred_element_type=F32)
            st = decay * st + kv
            outs.append(o_intra + o_inter)
        st_ref[hh] = st
        o = outs[0] if n_chunks == 1 else jnp.concatenate(outs, axis=0)
        c0 = o - jnp.mean(o, axis=-1, keepdims=True)
        o = c0 * lax.rsqrt(jnp.mean(c0 * c0, axis=-1, keepdims=True) + EPS)
        o = o * gn_ref[:, ws] * jax.nn.silu(g_ref[:, ws])
        o_ref[:, ws] = o.astype(o_ref.dtype)
        finals.append(st)

    @pl.when(last)
    def _():
        for hh in range(heads):
            sout_ref[0, hh] = finals[hh]


def _rotary_tables(pos):
    half = HEAD_LANES // 2
    inv = 1.0 / (ROPE_BASE ** jnp.linspace(0.0, 1.0, half, dtype=F32))
    ang = pos.astype(F32)[:, None] * inv[None, :]
    cos, sin = jnp.cos(ang), jnp.sin(ang)
    return jnp.concatenate([cos, cos], axis=-1), jnp.concatenate([-sin, sin], axis=-1)


def _even_mixer(proj, batch, seq, pos, layer, s_a, s_b, a_shape, b_shape, lb_logits, g_a, g_b,
                rows_per_step=1024, heads_per_step=4):
    m = proj.shape[0]
    a_heads, dk, dv_a = a_shape
    b_heads, dk_b, dv_b = b_shape
    assert dk == HEAD_LANES and dv_a == HEAD_LANES and dk_b == HEAD_LANES
    chunk = min(CHUNK, seq)
    rows = min(rows_per_step, seq)
    assert seq % rows == 0 and rows % chunk == 0
    nt = seq // rows
    has_state = s_a is not None
    hp_a = a_heads if nt == 1 else heads_per_step
    hp_b = b_heads if nt == 1 else heads_per_step
    assert a_heads % hp_a == 0 and b_heads % hp_b == 0
    a_w, a_v = a_heads * dk, a_heads * dv_a
    b_qk, b_v = b_heads * HEAD_LANES, b_heads * dv_b
    off_b = 2 * a_w + 2 * a_v
    grid_sem = ("parallel", "parallel", "arbitrary")

    def cols(offset, width, hp):
        blk = hp * width
        assert offset % blk == 0
        return pl.BlockSpec((rows, blk), lambda b, h, t: (b * nt + t, offset // blk + h))

    n_layers = lb_logits.shape[0]
    wa = hp_a * HEAD_LANES
    in_specs = [cols(0, dk, hp_a), cols(a_w, dk, hp_a), cols(2 * a_w, dv_a, hp_a), cols(2 * a_w + a_v, dv_a, hp_a),
                pl.BlockSpec((n_layers, wa), lambda b, h, t: (0, h)),
                pl.BlockSpec((1, wa), lambda b, h, t: (0, h))]
    args = [proj, proj, proj, proj, lb_logits, g_a.reshape(1, a_v)]
    if has_state:
        in_specs.append(pl.BlockSpec((1, hp_a, dk, dv_a), lambda b, h, t: (b, h, 0, 0)))
        args.append(s_a)
    o_a, new_a = pl.pallas_call(
        functools.partial(_hgrn_kernel, chunk=chunk, n_chunks=rows // chunk, layer=layer, has_state=has_state,
                          heads=hp_a),
        out_shape=(jax.ShapeDtypeStruct((m, a_v), BF16),
                   jax.ShapeDtypeStruct((batch, a_heads, dk, dv_a), F32)),
        grid=(batch, a_heads // hp_a, nt),
        in_specs=in_specs,
        out_specs=(pl.BlockSpec((rows, wa), lambda b, h, t: (b * nt + t, h)),
                   pl.BlockSpec((1, hp_a, dk, dv_a), lambda b, h, t: (b, h, 0, 0))),
        scratch_shapes=[pltpu.VMEM((hp_a, dv_a, dk), F32)],
        compiler_params=_cparams(grid_sem, 48 << 20),
        name="hgrn2",
    )(*args)

    cosf, sinf = _rotary_tables(pos)
    log_gamma = jnp.log(1.0 - jnp.exp2(-5.0 - jnp.arange(b_heads, dtype=F32)))
    lg = jnp.broadcast_to(log_gamma[:, None, None], (b_heads, 1, dv_b))
    wv = hp_b * dv_b
    in_specs = [cols(off_b, HEAD_LANES, hp_b), cols(off_b + b_qk, HEAD_LANES, hp_b),
                cols(off_b + 2 * b_qk, dv_b, hp_b), cols(off_b + 2 * b_qk + b_v, dv_b, hp_b),
                pl.BlockSpec((rows, HEAD_LANES), lambda b, h, t: (t, 0)),
                pl.BlockSpec((rows, HEAD_LANES), lambda b, h, t: (t, 0)),
                pl.BlockSpec((hp_b, 1, dv_b), lambda b, h, t: (h, 0, 0)),
                pl.BlockSpec((1, wv), lambda b, h, t: (0, h))]
    args = [proj, proj, proj, proj, cosf, sinf, lg, g_b.reshape(1, b_v)]
    if has_state:
        in_specs.append(pl.BlockSpec((1, hp_b, HEAD_LANES, dv_b), lambda b, h, t: (b, h, 0, 0)))
        args.append(s_b)
    o_b, new_b = pl.pallas_call(
        functools.partial(_ret_kernel, chunk=chunk, n_chunks=rows // chunk, has_state=has_state, heads=hp_b,
                          dv=dv_b),
        out_shape=(jax.ShapeDtypeStruct((m, b_v), BF16),
                   jax.ShapeDtypeStruct((batch, b_heads, HEAD_LANES, dv_b), F32)),
        grid=(batch, b_heads // hp_b, nt),
        in_specs=in_specs,
        out_specs=(pl.BlockSpec((rows, wv), lambda b, h, t: (b * nt + t, h)),
                   pl.BlockSpec((1, hp_b, HEAD_LANES, dv_b), lambda b, h, t: (b, h, 0, 0))),
        scratch_shapes=[pltpu.VMEM((hp_b, HEAD_LANES, dv_b), F32)],
        compiler_params=_cparams(grid_sem, 48 << 20),
        name="retention",
    )(*args)
    return o_a, o_b, new_a, new_b


SUBLANES = 8


def _rglru_block(xb, yb, pos_base, rows, first_pos_zero, cw_ref, cb_ref, wa_ref, wx_ref, gab_ref, gxb_ref, lam_ref,
                 prev_ref, h_ref, a_ref, b_ref):
    width = cw_ref.shape[0]
    assert width - 1 <= SUBLANES <= rows
    grouped = (rows // SUBLANES, SUBLANES, xb.shape[1])
    sub = lax.broadcasted_iota(jnp.int32, grouped, 1)
    x3 = xb.reshape(grouped)
    prev = prev_ref[...]
    cw = cw_ref[...]
    acc = None
    for j in range(width):
        back = width - 1 - j
        if back == 0:
            shifted = x3
        else:
            rot = pltpu.roll(x3, back, 1)
            rot_before = jnp.concatenate([pltpu.roll(prev, back, 0)[None], rot[:-1]], axis=0)
            shifted = jnp.where(sub >= back, rot, rot_before)
        term = shifted * cw[j:j + 1]
        acc = term if acc is None else acc + term
    conv = (cb_ref[...] + acc).reshape(xb.shape)
    prev_ref[...] = xb[rows - SUBLANES:rows]
    tail = xb[rows - (width - 1):rows]

    conv_bf = conv.astype(BF16)
    r = jax.nn.sigmoid(jnp.dot(conv_bf, wa_ref[0], preferred_element_type=F32) + gab_ref[...])
    i = jax.nn.sigmoid(jnp.dot(conv_bf, wx_ref[0], preferred_element_type=F32) + gxb_ref[...])
    log_a = -RG_C * r * jax.nn.softplus(-lam_ref[...])
    a = jnp.exp(log_a)
    gap = 1.0 - a * a
    mult = jnp.where(gap == 0.0, 0.0, gap * lax.rsqrt(gap))
    gated = i * conv
    bb = mult * gated
    if first_pos_zero:
        row = lax.broadcasted_iota(jnp.int32, (SUBLANES, a.shape[1]), 0)
        top = jnp.where(row + pos_base == 0, gated[:SUBLANES], bb[:SUBLANES])
        bb = jnp.concatenate([top, bb[SUBLANES:]], axis=0)

    a3, b3 = a.reshape(grouped), bb.reshape(grouped)
    shift = 1
    while shift < SUBLANES:
        take = sub >= shift
        b3 = jnp.where(take, a3 * pltpu.roll(b3, shift, 1) + b3, b3)
        a3 = jnp.where(take, a3 * pltpu.roll(a3, shift, 1), a3)
        shift *= 2
    a_ref[...] = a3.reshape(a.shape)
    b_ref[...] = b3.reshape(a.shape)

    def group(gidx, carry):
        r0 = pl.multiple_of(gidx * SUBLANES, SUBLANES)
        hg = a_ref[pl.ds(r0, SUBLANES), :] * carry + b_ref[pl.ds(r0, SUBLANES), :]
        b_ref[pl.ds(r0, SUBLANES), :] = hg
        return hg[SUBLANES - 1:SUBLANES, :]

    h_ref[...] = lax.fori_loop(0, rows // SUBLANES, group, h_ref[...], unroll=True)
    return jax.nn.gelu(yb, approximate=True) * b_ref[...], tail


def _griffin_kernel(*refs, rows, streams, first_pos_zero, has_state):
    if has_state:
        (xb_ref, yb_ref, cw_ref, cb_ref, wa_ref, wx_ref, gab_ref, gxb_ref, lam_ref, sconv_ref, h0_ref,
         o_ref, hlast_ref, nconv_ref, xpad_ref, h_ref, a_ref, b_ref) = refs
    else:
        (xb_ref, yb_ref, cw_ref, cb_ref, wa_ref, wx_ref, gab_ref, gxb_ref, lam_ref,
         o_ref, hlast_ref, nconv_ref, xpad_ref, h_ref, a_ref, b_ref) = refs
        sconv_ref = h0_ref = None
    t = pl.program_id(2)

    @pl.when(t == 0)
    def _():
        for q in range(streams):
            xpad_ref[q] = jnp.zeros(xpad_ref.shape[1:], F32)
            if has_state:
                xpad_ref[q, SUBLANES - sconv_ref.shape[1]:SUBLANES, :] = sconv_ref[q]
                h_ref[q] = h0_ref[q]
            else:
                h_ref[q] = jnp.zeros(h_ref.shape[1:], F32)

    tails = []
    for q in range(streams):
        rs = slice(q * rows, (q + 1) * rows)
        o, tail = _rglru_block(xb_ref[rs, :], yb_ref[rs, :], t * rows, rows, first_pos_zero, cw_ref, cb_ref, wa_ref,
                               wx_ref, gab_ref, gxb_ref, lam_ref, xpad_ref.at[q], h_ref.at[q], a_ref.at[q],
                               b_ref.at[q])
        o_ref[rs, :] = o.astype(o_ref.dtype)
        tails.append(tail)

    @pl.when(t == pl.num_programs(2) - 1)
    def _():
        for q in range(streams):
            hlast_ref[q] = h_ref[q]
            nconv_ref[q] = tails[q]


def _pair_blockdiag(w, pair):
    n, c, _ = w.shape
    w = w.reshape(n // pair, pair, c, c)
    rows = []
    for p in range(pair):
        blocks = [w[:, p] if q == p else jnp.zeros_like(w[:, p]) for q in range(pair)]
        rows.append(jnp.concatenate(blocks, axis=-1))
    return jnp.concatenate(rows, axis=-2)


def _odd_mixer(xy, batch, seq, pos0, h0, s_conv, conv_w, conv_b, wa, wx, ga_b, gx_b, lam,
               rows_per_step=1024, short_streams_per_step=8):
    m = xy.shape[0]
    d_rnn = conv_w.shape[1]
    width = conv_w.shape[0]
    blk = wa.shape[1]
    nblk = d_rnn // blk
    rows = min(rows_per_step, seq)
    assert seq % rows == 0 and rows % SUBLANES == 0 and rows >= width - 1
    nt = seq // rows
    streams = short_streams_per_step if (nt == 1 and batch % short_streams_per_step == 0) else 1
    has_state = h0 is not None

    def vec(a):
        return a.reshape(1, d_rnn)

    def vspec():
        return pl.BlockSpec((1, blk), lambda b, j, t: (0, j))

    def rows_spec(col0):
        return pl.BlockSpec((streams * rows, blk), lambda b, j, t: (b * nt + t, col0 + j))

    def state_spec(n):
        return pl.BlockSpec((streams, n, blk), lambda b, j, t: (b, 0, j))

    in_specs = [rows_spec(0), rows_spec(nblk),
                pl.BlockSpec((width, blk), lambda b, j, t: (0, j)),
                vspec(),
                pl.BlockSpec((1, blk, blk), lambda b, j, t: (j, 0, 0)),
                pl.BlockSpec((1, blk, blk), lambda b, j, t: (j, 0, 0)),
                vspec(), vspec(), vspec()]
    args = [xy, xy, conv_w, vec(conv_b), wa, wx, vec(ga_b), vec(gx_b), vec(lam)]
    if has_state:
        in_specs += [state_spec(width - 1), state_spec(1)]
        args += [s_conv, h0.reshape(batch, 1, d_rnn)]
    o, h_last, new_conv = pl.pallas_call(
        functools.partial(_griffin_kernel, rows=rows, streams=streams, first_pos_zero=(pos0 == 0),
                          has_state=has_state),
        out_shape=(jax.ShapeDtypeStruct((m, d_rnn), BF16),
                   jax.ShapeDtypeStruct((batch, 1, d_rnn), F32),
                   jax.ShapeDtypeStruct((batch, width - 1, d_rnn), F32)),
        grid=(batch // streams, nblk, nt),
        in_specs=in_specs,
        out_specs=(rows_spec(0), state_spec(1), state_spec(width - 1)),
        scratch_shapes=[pltpu.VMEM((streams, SUBLANES, blk), F32),
                        pltpu.VMEM((streams, 1, blk), F32),
                        pltpu.VMEM((streams, rows, blk), F32),
                        pltpu.VMEM((streams, rows, blk), F32)],
        compiler_params=_cparams(("parallel", "parallel", "arbitrary"), 32 << 20),
        name="griffin",
    )(*args)
    return o, h_last.reshape(batch, d_rnn), new_conv


def _trunk(x, pos0, states, p):
    batch, seq, d = x.shape
    depth = p["norm_mix"].shape[0]
    pos = pos0 + jnp.arange(seq, dtype=jnp.int32)
    h = x.reshape(batch * seq, d)
    many_rows = h.shape[0] >= 8 * 1024

    def project(xin, w, layer, row_scale=None, **kw):
        if many_rows:
            return _matmul_stationary(xin, w, layer, tn=(512 if w.shape[1] > 4096 else 1024), row_scale=row_scale,
                                      **kw)
        assert row_scale is None
        return _matmul(xin, w, layer, tn=512, tk=w.shape[1], **kw)

    def residual_and_norm(xin, w, layer, res, gain):
        if many_rows:
            return _matmul_stationary(xin, w, layer, mode="residual", res=res, tn=512, norm_gain=gain)
        hh = project(xin, w, layer, mode="residual", res=res)
        return hh, _rms_norm(hh, gain, BF16), None

    n_hgrn, n_ret, n_rg, n_conv = [], [], [], []
    hn, scale = _rms_norm(h, p["norm_mix"][0], BF16), None
    for l in range(depth):
        if l % 2 == 0:
            e = l // 2
            proj = project(hn, p["w_in_even"], e, row_scale=scale)
            s_a, s_b = (None, None) if states is None else (states[0][e], states[1][e])
            o_a, o_b, sa, sb = _even_mixer(proj, batch, seq, pos, l, s_a, s_b, p["hgrn_shape"], p["ret_shape"],
                                           p["hgrn_lb_logits"], p["hgrn_norm"][e], p["ret_norm"][e])
            n_hgrn.append(sa)
            n_ret.append(sb)
            h, hm, scale_m = residual_and_norm((o_a, o_b), p["w_out_even"], e, h, p["norm_mlp"][l])
        else:
            od = l // 2
            xy = project(hn, p["w_in_odd"], od, row_scale=scale)
            h0, sc = (None, None) if states is None else (states[2][od], states[3][od])
            o, sh, scn = _odd_mixer(xy, batch, seq, pos0, h0, sc, p["conv_w"][od], p["conv_b"][od],
                                    p["gate_a_w"][od], p["gate_x_w"][od], p["gate_a_b"][od], p["gate_x_b"][od],
                                    p["rglru_lambda"][od])
            n_rg.append(sh)
            n_conv.append(scn)
            h, hm, scale_m = residual_and_norm(o, p["w_out_odd"], od, h, p["norm_mlp"][l])
        up = project(hm, p["w_up"], l, row_scale=scale_m, mode="relu2", out_dtype=BF16)
        more = l + 1 < depth
        if many_rows:
            n_slabs = p["w_down"].shape[1] // d
            for s in range(n_slabs - 1):
                h = _matmul_stationary(up, p["w_down"], l, mode="residual", res=h, k_slab=(s, n_slabs))
            last_slab = dict(mode="residual", res=h, k_slab=(n_slabs - 1, n_slabs))
            if more:
                h, hn, scale = _matmul_stationary(up, p["w_down"], l, tn=512, norm_gain=p["norm_mix"][l + 1],
                                                  **last_slab)
            else:
                h = _matmul_stationary(up, p["w_down"], l, **last_slab)
        else:
            h = _matmul(up, p["w_down"], l, mode="residual", res=h, tn=512, tk=4096)
            if more:
                hn, scale = _rms_norm(h, p["norm_mix"][l + 1], BF16), None
    out = _rms_norm(h, p["norm_final"], x.dtype).reshape(batch, seq, d)
    return out, jnp.stack(n_hgrn), jnp.stack(n_ret), jnp.stack(n_rg), jnp.stack(n_conv)


def kernel(x_prompt, x_sample, state_hgrn, state_ret, state_rglru, state_conv, norm_mix, norm_mlp, norm_final,
           w_in_even, hgrn_lb_logits, hgrn_norm, ret_norm, w_out_even, w_in_odd, conv_w, conv_b, gate_a_w,
           gate_a_b, gate_x_w, gate_x_b, rglru_lambda, w_out_odd, w_up, w_down):
    past_len = 4096

    p = dict(
        norm_mix=norm_mix, norm_mlp=norm_mlp, norm_final=norm_final,
        w_in_even=w_in_even, w_out_even=w_out_even, w_in_odd=w_in_odd, w_out_odd=w_out_odd, w_up=w_up,
        w_down=w_down,
        hgrn_lb_logits=hgrn_lb_logits, hgrn_norm=hgrn_norm, ret_norm=ret_norm,
        conv_w=conv_w, conv_b=conv_b,
        gate_a_w=jnp.stack([_pair_blockdiag(w, RG_PAIR) for w in gate_a_w.astype(BF16)]),
        gate_x_w=jnp.stack([_pair_blockdiag(w, RG_PAIR) for w in gate_x_w.astype(BF16)]),
        gate_a_b=gate_a_b, gate_x_b=gate_x_b, rglru_lambda=rglru_lambda,
        hgrn_shape=state_hgrn.shape[2:], ret_shape=state_ret.shape[2:],
    )
    y_p, hg_p, rt_p, rg_p, cv_p = _trunk(x_prompt, 0, None, p)
    y_s, hg_s, rt_s, rg_s, cv_s = _trunk(x_sample, past_len, (state_hgrn, state_ret, state_rglru, state_conv), p)
    return (y_p, y_s, hg_p, rt_p, rg_p, cv_p, hg_s, rt_s, rg_s, cv_s)
```
